```python
import jax, jax.numpy as jnp
from jax import lax
import numpy as np

D_MODEL = 1024
BATCH = 16
SEQ = 4096
DEPTH = 2
DEC_BATCH = 2
DEC_SEQ = 8192
PAST_LEN = 128

N_META = 16
GRID_W = 64
RMS_EPS = 1e-6
POOL_WINDOWS = (2, 4, 8, 16)
N_POOL_GROUPS = len(POOL_WINDOWS)
POOL_WIDTH = D_MODEL // 2
POOL_GROUP = POOL_WIDTH // N_POOL_GROUPS
CONV_WIDTH = D_MODEL // 2
CONV_K = 3
EVEN_IN_WIDTH = POOL_WIDTH + 3 * CONV_WIDTH
EVEN_OUT_WIDTH = POOL_WIDTH + CONV_WIDTH
NA_HEADS = 16
HEAD_DIM = D_MODEL // NA_HEADS
NA_WIDTH = NA_HEADS * HEAD_DIM
NA_WIN_ROWS = 8
NA_WIN_COLS = 16
REL_ROWS = 2 * NA_WIN_ROWS - 1
REL_COLS = 2 * NA_WIN_COLS - 1
D_FF = 2816
N_EXPERTS = 8
TOP_K = 2
D_FF_EXPERT = 3584
N_EVEN = (DEPTH + 1) // 2
N_ODD = DEPTH // 2

kernel_name = "hybrid_pool_conv_natten_moe_encoder"


def rms_norm(x, g):
    xf = x.astype(jnp.float32)
    y = xf * lax.rsqrt(jnp.mean(xf * xf, axis=-1, keepdims=True) + RMS_EPS)
    return (y * g.astype(jnp.float32)).astype(x.dtype)


def centred_mean_minus_self(p, k):
    B, L, C = p.shape
    left = k // 2
    right = k - 1 - left
    pf = p.astype(jnp.float32)
    cs = jnp.concatenate([jnp.zeros((B, 1, C), jnp.float32), jnp.cumsum(pf, axis=1)], axis=1)
    t = jnp.arange(L)
    lo = jnp.clip(t - left, 0, L)
    hi = jnp.clip(t + right + 1, 0, L)
    s = jnp.take(cs, hi, axis=1) - jnp.take(cs, lo, axis=1)
    cnt = (hi - lo).astype(jnp.float32)
    return (s / cnt[None, :, None] - pf).astype(p.dtype)


def short_conv(u, w):
    up = jnp.pad(u, ((0, 0), (1, 1), (0, 0)))
    return up[:, :-2] * w[0] + up[:, 1:-1] * w[1] + up[:, 2:] * w[2]


def even_mixer(h, w_in, pool_w, pool_scale, conv_w, w_out):
    z = h @ w_in
    p = z[..., :POOL_WIDTH]
    gb, gc, hc = jnp.split(z[..., POOL_WIDTH:], 3, axis=-1)
    pooled = jnp.stack([
        centred_mean_minus_self(p[..., g * POOL_GROUP:(g + 1) * POOL_GROUP], k)
        for g, k in enumerate(POOL_WINDOWS)], axis=2)
    a = jnp.einsum('blgc,gcd->blgd', pooled, pool_w)
    a = a.reshape(a.shape[0], a.shape[1], POOL_WIDTH) * pool_scale
    c = gb * short_conv(gc * hc, conv_w)
    return jnp.concatenate([a, c], axis=-1) @ w_out


def na_mixer(h, rows, w_qkv, rel_bias, meta_bias, w_out):
    B, L, _ = h.shape
    T = L - N_META
    kh = min(NA_WIN_ROWS, rows)
    scale = HEAD_DIM ** -0.5
    qkv = (h @ w_qkv).reshape(B, L, 3, NA_HEADS, HEAD_DIM)
    q, k, v = qkv[:, :, 0], qkv[:, :, 1], qkv[:, :, 2]
    qm, km, vm = q[:, :N_META], k[:, :N_META], v[:, :N_META]
    qg = q[:, N_META:].reshape(B, rows, GRID_W, NA_HEADS, HEAD_DIM)
    kg = k[:, N_META:].reshape(B, rows, GRID_W, NA_HEADS, HEAD_DIM)
    vg = v[:, N_META:].reshape(B, rows, GRID_W, NA_HEADS, HEAD_DIM)
    sm = jnp.einsum('bqhd,bmhd->bhqm', qm, km).astype(jnp.float32) * scale + meta_bias[None, :, None, :]
    pm = jax.nn.softmax(sm, axis=-1).astype(vm.dtype)
    o_meta = jnp.einsum('bhqm,bmhd->bqhd', pm, vm)
    qc = np.arange(GRID_W)[:, None]
    kc = np.arange(GRID_W)[None, :]
    cstart = np.clip(qc - NA_WIN_COLS // 2, 0, GRID_W - NA_WIN_COLS)
    col_mask = jnp.asarray((kc >= cstart) & (kc < cstart + NA_WIN_COLS))
    col_off = jnp.asarray(np.clip(kc - qc + NA_WIN_COLS - 1, 0, REL_COLS - 1))

    def row_block(r):
        rs = jnp.clip(r - kh // 2, 0, rows - kh)
        q_r = lax.dynamic_index_in_dim(qg, r, axis=1, keepdims=False)
        k_s = lax.dynamic_slice_in_dim(kg, rs, kh, axis=1)
        v_s = lax.dynamic_slice_in_dim(vg, rs, kh, axis=1)
        s = jnp.einsum('bqhd,bikhd->bhqik', q_r, k_s).astype(jnp.float32) * scale
        row_off = rs + jnp.arange(kh) - r + NA_WIN_ROWS - 1
        bias = rel_bias[:, row_off[:, None, None], col_off[None, :, :]]
        s = s + jnp.transpose(bias, (0, 2, 1, 3)).astype(jnp.float32)[None]
        s = jnp.where(col_mask[None, None, :, None, :], s, -1e30)
        s_m = jnp.einsum('bqhd,bmhd->bhqm', q_r, km).astype(jnp.float32) * scale + meta_bias[None, :, None, :]
        s_all = jnp.concatenate([s.reshape(B, NA_HEADS, GRID_W, kh * GRID_W), s_m], axis=-1)
        pr = jax.nn.softmax(s_all, axis=-1)
        p_grid = pr[..., :kh * GRID_W].reshape(B, NA_HEADS, GRID_W, kh, GRID_W).astype(v_s.dtype)
        p_meta = pr[..., kh * GRID_W:].astype(vm.dtype)
        return (jnp.einsum('bhqik,bikhd->bqhd', p_grid, v_s)
                + jnp.einsum('bhqm,bmhd->bqhd', p_meta, vm))

    o_grid = lax.map(row_block, jnp.arange(rows))
    o_grid = jnp.moveaxis(o_grid, 0, 1).reshape(B, T, NA_HEADS, HEAD_DIM)
    o = jnp.concatenate([o_meta, o_grid], axis=1).reshape(B, L, NA_WIDTH)
    return o @ w_out


def swiglu(h, w_gate, w_up, w_down):
    return (jax.nn.silu(h @ w_gate) * (h @ w_up)) @ w_down


def moe_swiglu(h, w_router, b_router, w_gate, w_up, w_down):
    logits = (h @ w_router).astype(jnp.float32) + b_router.astype(jnp.float32)
    top_val, top_idx = lax.top_k(logits, TOP_K)
    gates = jax.nn.softmax(top_val, axis=-1)
    combine = jnp.sum(jax.nn.one_hot(top_idx, N_EXPERTS, dtype=jnp.float32) * gates[..., None], axis=-2)
    combine = combine.astype(h.dtype)
    out = jnp.zeros_like(h)
    for e in range(N_EXPERTS):
        out = out + combine[..., e:e + 1] * swiglu(h, w_gate[e], w_up[e], w_down[e])
    return out


def trunk(x, meta_tokens, norm_mix, norm_ffn, norm_final, even_w_in, pool_w, pool_scale,
          conv_w, even_w_out, ffn_w_gate, ffn_w_up, ffn_w_down, na_w_qkv, na_rel_bias,
          na_meta_bias, na_w_out, moe_w_router, moe_b_router, moe_w_gate, moe_w_up, moe_w_down):
    B, T, D = x.shape
    rows = T // GRID_W
    meta = jnp.broadcast_to(meta_tokens.astype(x.dtype)[None], (B, N_META, D))
    h = jnp.concatenate([meta, x], axis=1)
    for i in range(DEPTH):
        j = i // 2
        if i % 2 == 0:
            h = h + even_mixer(rms_norm(h, norm_mix[i]), even_w_in[j], pool_w[j], pool_scale[j],
                               conv_w[j], even_w_out[j])
            h = h + swiglu(rms_norm(h, norm_ffn[i]), ffn_w_gate[j], ffn_w_up[j], ffn_w_down[j])
        else:
            h = h + na_mixer(rms_norm(h, norm_mix[i]), rows, na_w_qkv[j], na_rel_bias[j],
                             na_meta_bias[j], na_w_out[j])
            h = h + moe_swiglu(rms_norm(h, norm_ffn[i]), moe_w_router[j], moe_b_router[j],
                               moe_w_gate[j], moe_w_up[j], moe_w_down[j])
    h = rms_norm(h, norm_final)
    return h[:, N_META:]


def setup_inputs(seed: int = 0) -> dict:
    key = jax.random.key(seed)
    ks = jax.random.split(key, 24)
    f32 = jnp.float32
    D = D_MODEL

    def nrm(k, shape, scale):
        return jax.random.normal(k, shape, f32) * scale

    return {
        "x_prompt": nrm(ks[0], (BATCH, SEQ, D), 1.0),
        "x_sample": nrm(ks[1], (DEC_BATCH, DEC_SEQ, D), 1.0),
        "meta_tokens": nrm(ks[2], (N_META, D), 1.0),
        "norm_mix": 1.0 + nrm(ks[3], (DEPTH, D), 0.05),
        "norm_ffn": 1.0 + nrm(ks[4], (DEPTH, D), 0.05),
        "norm_final": 1.0 + nrm(ks[5], (D,), 0.05),
        "even_w_in": nrm(ks[6], (N_EVEN, D, EVEN_IN_WIDTH), D ** -0.5),
        "pool_w": nrm(ks[7], (N_EVEN, N_POOL_GROUPS, POOL_GROUP, POOL_GROUP), POOL_GROUP ** -0.5),
        "pool_scale": 1.0 + nrm(ks[8], (N_EVEN, POOL_WIDTH), 0.1),
        "conv_w": nrm(ks[9], (N_EVEN, CONV_K, CONV_WIDTH), CONV_K ** -0.5),
        "even_w_out": nrm(ks[10], (N_EVEN, EVEN_OUT_WIDTH, D), EVEN_OUT_WIDTH ** -0.5),
        "ffn_w_gate": nrm(ks[11], (N_EVEN, D, D_FF), D ** -0.5),
        "ffn_w_up": nrm(ks[12], (N_EVEN, D, D_FF), D ** -0.5),
        "ffn_w_down": nrm(ks[13], (N_EVEN, D_FF, D), D_FF ** -0.5),
        "na_w_qkv": nrm(ks[14], (N_ODD, D, 3 * NA_WIDTH), D ** -0.5),
        "na_rel_bias": nrm(ks[15], (N_ODD, NA_HEADS, REL_ROWS, REL_COLS), 0.1),
        "na_meta_bias": nrm(ks[16], (N_ODD, NA_HEADS, N_META), 0.1),
        "na_w_out": nrm(ks[17], (N_ODD, NA_WIDTH, D), NA_WIDTH ** -0.5),
        "moe_w_router": nrm(ks[18], (N_ODD, D, N_EXPERTS), D ** -0.5),
        "moe_b_router": nrm(ks[19], (N_ODD, N_EXPERTS), 0.01),
        "moe_w_gate": nrm(ks[20], (N_ODD, N_EXPERTS, D, D_FF_EXPERT), D ** -0.5),
        "moe_w_up": nrm(ks[21], (N_ODD, N_EXPERTS, D, D_FF_EXPERT), D ** -0.5),
        "moe_w_down": nrm(ks[22], (N_ODD, N_EXPERTS, D_FF_EXPERT, D), D_FF_EXPERT ** -0.5),
    }


def reference(x_prompt, x_sample, meta_tokens, norm_mix, norm_ffn, norm_final, even_w_in, pool_w,
              pool_scale, conv_w, even_w_out, ffn_w_gate, ffn_w_up, ffn_w_down, na_w_qkv,
              na_rel_bias, na_meta_bias, na_w_out, moe_w_router, moe_b_router, moe_w_gate,
              moe_w_up, moe_w_down):
    y_prompt = trunk(x_prompt, meta_tokens, norm_mix, norm_ffn, norm_final, even_w_in, pool_w,
                     pool_scale, conv_w, even_w_out, ffn_w_gate, ffn_w_up, ffn_w_down, na_w_qkv,
                     na_rel_bias, na_meta_bias, na_w_out, moe_w_router, moe_b_router, moe_w_gate,
                     moe_w_up, moe_w_down)
    y_sample = trunk(x_sample, meta_tokens, norm_mix, norm_ffn, norm_final, even_w_in, pool_w,
                     pool_scale, conv_w, even_w_out, ffn_w_gate, ffn_w_up, ffn_w_down, na_w_qkv,
                     na_rel_bias, na_meta_bias, na_w_out, moe_w_router, moe_b_router, moe_w_gate,
                     moe_w_up, moe_w_down)
    return (y_prompt, y_sample)
```

```python
import functools

import jax
import jax.numpy as jnp
from jax import lax
from jax.experimental import pallas as pl
from jax.experimental.pallas import tpu as pltpu

F32 = jnp.float32
BF16 = jnp.bfloat16

N_META = 16
GRID_W = 64
RMS_EPS = 1e-6
POOL_WINDOWS = (2, 4, 8, 16)
POOL_GROUP = 128
POOL_WIDTH = 512
CONV_WIDTH = 512
NA_HEADS = 16
HEAD_DIM = 64
NA_WIN_ROWS = 8
NA_WIN_COLS = 16
N_EXPERTS = 8
LANES = 128
HEADS_PER_BLOCK = LANES // HEAD_DIM
HALO_ROWS = NA_WIN_ROWS // 2
EXT = 16
VMEM_LIMIT = 56 * 1024 * 1024


def _const_spec(shape):
    nd = len(shape)
    return pl.BlockSpec(shape, lambda *_: (0,) * nd, pipeline_mode=pl.Buffered(1))


def _rms(x, g):
    ms = jnp.mean(x * x, axis=-1, keepdims=True)
    return x * lax.rsqrt(ms + RMS_EPS) * g


def _dot(a, b):
    return jnp.dot(a, b, preferred_element_type=F32)


def _dot_nt(a, b):
    return lax.dot_general(a, b, (((1,), (1,)), ((), ())), preferred_element_type=F32)


def _silu(x):
    return x * jax.nn.sigmoid(x)


def _even_mix(xe, pos, seq_len, g, win_ref, pw_ref, ps, cw, wout_ref):
    n = xe.shape[0]
    xn = _rms(xe, g).astype(BF16)
    z = _dot(xn, win_ref[...])
    a_parts = []
    for gi, k in enumerate(POOL_WINDOWS):
        p = z[:, gi * POOL_GROUP:(gi + 1) * POOL_GROUP]
        s = p
        w = 1
        while w < k:
            s = s + pltpu.roll(s, n - w, 0)
            w *= 2
        left = k // 2
        right = k - 1 - left
        s = pltpu.roll(s, left, 0)
        cnt = jnp.minimum(pos + right + 1, seq_len) - jnp.maximum(pos - left, 0)
        inv = 1.0 / jnp.maximum(cnt, 1).astype(F32)
        pooled = (s * inv - p).astype(BF16)
        a_parts.append(_dot(pooled, pw_ref[gi]))
    a = jnp.concatenate(a_parts, axis=-1) * ps
    gb = z[:, POOL_WIDTH:POOL_WIDTH + CONV_WIDTH]
    gc = z[:, POOL_WIDTH + CONV_WIDTH:POOL_WIDTH + 2 * CONV_WIDTH]
    hc = z[:, POOL_WIDTH + 2 * CONV_WIDTH:]
    u = gc * hc
    conv = pltpu.roll(u, 1, 0) * cw[0:1] + u * cw[1:2] + pltpu.roll(u, n - 1, 0) * cw[2:3]
    c = gb * conv
    return (_dot(a.astype(BF16), wout_ref[0:POOL_WIDTH, :])
            + _dot(c.astype(BF16), wout_ref[POOL_WIDTH:, :]))


def _even_kernel(tile_ref, prev_ref, next_ref, meta_ref, first_ref, g_ref, win_ref, pw_ref,
                 ps_ref, cw_ref, wout_ref, out_ref, xe_ref, *, tm, n_tiles, tiles_p, n_p_tiles,
                 tiles_s, t_p, t_s, n_seq):
    i = pl.program_id(0)
    weights = (g_ref[...], win_ref, pw_ref, ps_ref[...], cw_ref[...], wout_ref)

    @pl.when(i < n_tiles)
    def _main():
        is_p = i < n_p_tiles
        tin = jnp.where(is_p, i % tiles_p, (i - n_p_tiles) % tiles_s)
        last = jnp.where(is_p, tiles_p, tiles_s) - 1
        seq_len = jnp.where(is_p, t_p, t_s) + N_META
        n = tm + 2 * EXT

        @pl.when(tin == 0)
        def _():
            xe_ref[0:EXT, :] = meta_ref[...]

        @pl.when(tin != 0)
        def _():
            xe_ref[0:EXT, :] = prev_ref[...]

        xe_ref[EXT:EXT + tm, :] = tile_ref[...]

        @pl.when(tin == last)
        def _():
            xe_ref[EXT + tm:n, :] = jnp.zeros((EXT, xe_ref.shape[1]), F32)

        @pl.when(tin != last)
        def _():
            xe_ref[EXT + tm:n, :] = next_ref[...]

        pos = lax.broadcasted_iota(jnp.int32, (n, 1), 0) + tin * tm
        y = _even_mix(xe_ref[0:n, :], pos, seq_len, *weights)
        out_ref[...] = tile_ref[...] + y[EXT:EXT + tm]

    @pl.when(i == n_tiles)
    def _meta():
        grp = 3 * EXT
        n = grp * n_seq
        d = xe_ref.shape[1]
        for b in range(n_seq):
            xe_ref[grp * b:grp * b + EXT, :] = jnp.zeros((EXT, d), F32)
            xe_ref[grp * b + EXT:grp * b + 2 * EXT, :] = tile_ref[EXT * b:EXT * (b + 1), :]
            xe_ref[grp * b + 2 * EXT:grp * (b + 1), :] = first_ref[EXT * b:EXT * (b + 1), :]
        pos = lax.broadcasted_iota(jnp.int32, (n, 1), 0) % grp - EXT
        y = _even_mix(xe_ref[0:n, :], pos, jnp.int32(1 << 30), *weights)
        out_ref[...] = jnp.zeros(out_ref.shape, F32)
        for b in range(n_seq):
            out_ref[EXT * b:EXT * (b + 1), :] = (
                tile_ref[EXT * b:EXT * (b + 1), :] + y[grp * b + EXT:grp * b + 2 * EXT])


def _even_layer(h, meta, xfirst, g, win, pw, ps, cw, wout, plan):
    tm = plan["tm"]
    d = h.shape[1]
    n_tiles = plan["n_main"] // tm
    blk = tm // EXT
    last_blk = h.shape[0] // EXT - 1
    kern = functools.partial(
        _even_kernel, tm=tm, n_tiles=n_tiles, tiles_p=plan["t_p"] // tm,
        n_p_tiles=plan["n_p"] // tm, tiles_s=plan["t_s"] // tm, t_p=plan["t_p"], t_s=plan["t_s"],
        n_seq=plan["n_seq"])
    xe_rows = max(tm + 2 * EXT, 3 * EXT * plan["n_seq"])
    return pl.pallas_call(
        kern,
        grid=(n_tiles + 1,),
        in_specs=[
            pl.BlockSpec((tm, d), lambda i: (i, 0)),
            pl.BlockSpec((EXT, d), lambda i: (jnp.maximum(i * blk - 1, 0), 0)),
            pl.BlockSpec((EXT, d), lambda i: (jnp.minimum((i + 1) * blk, last_blk), 0)),
            _const_spec(meta.shape), _const_spec(xfirst.shape), _const_spec(g.shape),
            _const_spec(win.shape), _const_spec(pw.shape), _const_spec(ps.shape),
            _const_spec(cw.shape), _const_spec(wout.shape),
        ],
        out_specs=pl.BlockSpec((tm, d), lambda i: (i, 0)),
        out_shape=jax.ShapeDtypeStruct(h.shape, F32),
        scratch_shapes=[pltpu.VMEM((xe_rows, d), F32)],
        compiler_params=pltpu.CompilerParams(
            dimension_semantics=("arbitrary",), vmem_limit_bytes=VMEM_LIMIT),
        name="even_mixer",
    )(h, h, h, meta, xfirst, g, win, pw, ps, cw, wout)


def _ffn_kernel(x_ref, g_ref, wg_ref, wu_ref, wd_ref, out_ref):
    x = x_ref[...]
    xn = _rms(x, g_ref[...]).astype(BF16)
    act = (_silu(_dot(xn, wg_ref[...])) * _dot(xn, wu_ref[...])).astype(BF16)
    out_ref[...] = x + _dot(act, wd_ref[...])


def _ffn_layer(h, g, wg, wu, wd, plan):
    tm = plan["tm"]
    d = h.shape[1]
    return pl.pallas_call(
        _ffn_kernel,
        grid=(h.shape[0] // tm,),
        in_specs=[pl.BlockSpec((tm, d), lambda i: (i, 0)), _const_spec(g.shape),
                  _const_spec(wg.shape), _const_spec(wu.shape), _const_spec(wd.shape)],
        out_specs=pl.BlockSpec((tm, d), lambda i: (i, 0)),
        out_shape=jax.ShapeDtypeStruct(h.shape, F32),
        compiler_params=pltpu.CompilerParams(
            dimension_semantics=("arbitrary",), vmem_limit_bytes=VMEM_LIMIT),
        name="dense_swiglu",
    )(h, g, wg, wu, wd)


def _qkv_kernel(x_ref, g_ref, w_ref, out_ref):
    xn = _rms(x_ref[...], g_ref[...]).astype(BF16)
    out_ref[...] = _dot(xn, w_ref[...]).astype(BF16)


def _qkv_layer(h, g, w, plan):
    tm = plan["tm"]
    d = h.shape[1]
    nq = w.shape[1]
    return pl.pallas_call(
        _qkv_kernel,
        grid=(h.shape[0] // tm,),
        in_specs=[pl.BlockSpec((tm, d), lambda i: (i, 0)), _const_spec(g.shape),
                  _const_spec(w.shape)],
        out_specs=pl.BlockSpec((tm, nq), lambda i: (i, 0)),
        out_shape=jax.ShapeDtypeStruct((h.shape[0], nq), BF16),
        compiler_params=pltpu.CompilerParams(
            dimension_semantics=("arbitrary",), vmem_limit_bytes=VMEM_LIMIT),
        name="qkv_proj",
    )(h, g, w)


def _attn_kernel(q_ref, kc_ref, kp_ref, kn_ref, vc_ref, vp_ref, vn_ref, qm_ref, km_ref, vm_ref,
                 bias_ref, mb_ref, o_ref, om_ref, kx_ref, vx_ref, *, cr, chunks_p, n_p_chunks,
                 chunks_s, rows_p, rows_s):
    ci = pl.program_id(1)
    is_p = ci < n_p_chunks
    cin = jnp.where(is_p, ci % chunks_p, (ci - n_p_chunks) % chunks_s)
    rows = jnp.where(is_p, rows_p, rows_s)
    halo = HALO_ROWS * GRID_W
    ct = cr * GRID_W
    win = NA_WIN_ROWS * GRID_W

    kx_ref[0:halo, :] = kp_ref[...]
    kx_ref[halo:halo + ct, :] = kc_ref[...]
    kx_ref[halo + ct:, :] = kn_ref[...]
    vx_ref[0:halo, :] = vp_ref[...]
    vx_ref[halo:halo + ct, :] = vc_ref[...]
    vx_ref[halo + ct:, :] = vn_ref[...]

    lane = lax.broadcasted_iota(jnp.int32, (1, LANES), 1)
    head_masks = [(lane // HEAD_DIM) == hh for hh in range(HEADS_PER_BLOCK)]
    km = km_ref[...]
    vm = vm_ref[...]
    r0 = cin * cr

    def attend(q, kw, vw, bias_of_head):
        out = None
        for hh in range(HEADS_PER_BLOCK):
            qh = jnp.where(head_masks[hh], q, jnp.zeros_like(q))
            sm = _dot_nt(qh, km) + mb_ref[hh]
            m = jnp.max(sm, axis=-1, keepdims=True)
            if kw is not None:
                s = _dot_nt(qh, kw) + bias_of_head(hh)
                m = jnp.maximum(m, jnp.max(s, axis=-1, keepdims=True))
                p = jnp.exp(s - m)
            pm = jnp.exp(sm - m)
            l = jnp.sum(pm, axis=-1, keepdims=True)
            o = _dot(pm.astype(BF16), vm)
            if kw is not None:
                l = l + jnp.sum(p, axis=-1, keepdims=True)
                o = o + _dot(p.astype(BF16), vw)
            o = o / l
            out = o if out is None else jnp.where(head_masks[hh], o, out)
        return out

    def row_body(rl, carry):
        r = r0 + rl
        rs = jnp.clip(r - HALO_ROWS, 0, rows - NA_WIN_ROWS)
        start = rs - r + (NA_WIN_ROWS - 1)
        koff = pl.multiple_of((rs - r0 + HALO_ROWS) * GRID_W, GRID_W)
        qoff = pl.multiple_of(rl * GRID_W, GRID_W)
        q = q_ref[pl.ds(qoff, GRID_W), :]
        kw = kx_ref[pl.ds(koff, win), :]
        vw = vx_ref[pl.ds(koff, win), :]
        o = attend(q, kw, vw, lambda hh: bias_ref[hh, start])
        o_ref[pl.ds(qoff, GRID_W), :] = o.astype(BF16)
        return carry

    lax.fori_loop(0, cr, row_body, 0)
    om_ref[...] = attend(qm_ref[...], None, None, None).astype(BF16)


def _attn_layer(qkv, bias_tab, meta_bias, plan):
    cr = plan["cr"]
    ct = cr * GRID_W
    halo = HALO_ROWS * GRID_W
    n_main = plan["n_main"]
    n_chunks = n_main // ct
    n_hb = NA_HEADS // HEADS_PER_BLOCK
    hb = ct // halo
    last_halo = n_main // halo - 1
    meta_blk0 = n_main // N_META
    n_p_chunks = plan["n_p"] // ct
    chunks_p = plan["t_p"] // ct
    chunks_s = plan["t_s"] // ct
    b_p = plan["n_p"] // plan["t_p"]

    def seq_of(ci):
        return jnp.where(ci < n_p_chunks, ci // chunks_p, b_p + (ci - n_p_chunks) // chunks_s)

    def cur(off):
        return pl.BlockSpec((ct, LANES), lambda h, c: (c, off + h))

    def prev(off):
        return pl.BlockSpec((halo, LANES), lambda h, c: (jnp.maximum(c * hb - 1, 0), off + h))

    def nxt(off):
        return pl.BlockSpec((halo, LANES),
                            lambda h, c: (jnp.minimum((c + 1) * hb, last_halo), off + h))

    def meta(off):
        return pl.BlockSpec((N_META, LANES), lambda h, c: (meta_blk0 + seq_of(c), off + h))

    kern = functools.partial(
        _attn_kernel, cr=cr, chunks_p=chunks_p, n_p_chunks=n_p_chunks, chunks_s=chunks_s,
        rows_p=plan["t_p"] // GRID_W, rows_s=plan["t_s"] // GRID_W)
    width = NA_HEADS * HEAD_DIM
    return pl.pallas_call(
        kern,
        grid=(n_hb, n_chunks),
        in_specs=[
            cur(0), cur(n_hb), prev(n_hb), nxt(n_hb), cur(2 * n_hb), prev(2 * n_hb),
            nxt(2 * n_hb), meta(0), meta(n_hb), meta(2 * n_hb),
            pl.BlockSpec((HEADS_PER_BLOCK,) + bias_tab.shape[1:], lambda h, c: (h, 0, 0, 0)),
            pl.BlockSpec((None, HEADS_PER_BLOCK, 1, N_META), lambda h, c: (h, 0, 0, 0)),
        ],
        out_specs=[
            pl.BlockSpec((ct, LANES), lambda h, c: (c, h)),
            pl.BlockSpec((N_META, LANES), lambda h, c: (seq_of(c), h)),
        ],
        out_shape=[jax.ShapeDtypeStruct((n_main, width), BF16),
                   jax.ShapeDtypeStruct((plan["n_seq"] * N_META, width), BF16)],
        scratch_shapes=[pltpu.VMEM((ct + 2 * halo, LANES), BF16),
                        pltpu.VMEM((ct + 2 * halo, LANES), BF16)],
        compiler_params=pltpu.CompilerParams(
            dimension_semantics=("arbitrary", "arbitrary"), vmem_limit_bytes=VMEM_LIMIT),
        name="neighbourhood_attention",
    )(qkv, qkv, qkv, qkv, qkv, qkv, qkv, qkv, qkv, qkv, bias_tab, meta_bias)


def _attn_bias_table(rel_bias):
    qc = jnp.arange(GRID_W)[:, None]
    kc = jnp.arange(GRID_W)[None, :]
    cstart = jnp.clip(qc - NA_WIN_COLS // 2, 0, GRID_W - NA_WIN_COLS)
    col_mask = (kc >= cstart) & (kc < cstart + NA_WIN_COLS)
    col_off = jnp.clip(kc - qc + NA_WIN_COLS - 1, 0, 2 * NA_WIN_COLS - 2)
    row_off = jnp.arange(NA_WIN_ROWS)[:, None] + jnp.arange(NA_WIN_ROWS)[None, :]
    tab = rel_bias[:, row_off[:, :, None, None], col_off[None, None, :, :]]
    tab = jnp.where(col_mask[None, None, None], tab, -1e30)
    tab = jnp.transpose(tab, (0, 1, 3, 2, 4))
    return tab.reshape(NA_HEADS, NA_WIN_ROWS, GRID_W, NA_WIN_ROWS * GRID_W).astype(F32)


def _route_kernel(o_ref, om_ref, h_ref, wout_ref, g_ref, wr_ref, br_ref, tri_ref,
                  h_out, xn_out, route_out, cnt_out, cnt_ref, *, n_tiles):
    i = pl.program_id(0)

    @pl.when(i == 0)
    def _():
        cnt_ref[...] = jnp.zeros(cnt_ref.shape, F32)

    def body(o):
        h = h_ref[...] + _dot(o, wout_ref[...])
        h_out[...] = h
        xn = _rms(h, g_ref[...])
        xn_out[...] = xn
        logits = _dot(xn.astype(BF16), wr_ref[...]) + br_ref[...]
        tm = logits.shape[0]
        lane = lax.broadcasted_iota(jnp.int32, (tm, N_EXPERTS), 1)
        m1 = jnp.max(logits, axis=-1, keepdims=True)
        i1 = jnp.min(jnp.where(logits == m1, lane, N_EXPERTS), axis=-1, keepdims=True)
        oh1 = lane == i1
        rest = jnp.where(oh1, -jnp.inf, logits)
        m2 = jnp.max(rest, axis=-1, keepdims=True)
        i2 = jnp.min(jnp.where(rest == m2, lane, N_EXPERTS), axis=-1, keepdims=True)
        oh2 = lane == i2
        t = jnp.exp(m2 - m1)
        g1 = 1.0 / (1.0 + t)
        g2 = t * g1
        oh = jnp.where(oh1 | oh2, 1.0, 0.0)
        cum = _dot(tri_ref[...], oh.astype(BF16)) + cnt_ref[0:1, 0:N_EXPERTS]
        r1 = jnp.sum(jnp.where(oh1, cum, 0.0), axis=-1, keepdims=True)
        r2 = jnp.sum(jnp.where(oh2, cum, 0.0), axis=-1, keepdims=True)
        cnt_ref[0:1, 0:N_EXPERTS] = (cnt_ref[0:1, 0:N_EXPERTS]
                                     + jnp.sum(oh, axis=0, keepdims=True))
        route = jnp.zeros((tm, N_EXPERTS), F32)
        for col, val in enumerate([i1.astype(F32), i2.astype(F32), r1, r2, g1, g2]):
            route = jnp.where(lane == col, val, route)
        route_out[...] = route
        cnt_out[...] = cnt_ref[...]

    @pl.when(i < n_tiles)
    def _():
        body(o_ref[...])

    @pl.when(i == n_tiles)
    def _():
        body(om_ref[...])


def _route_layer(o_main, o_meta, h, wout, g, wr, br, plan):
    tm = plan["tm"]
    d = h.shape[1]
    n_tiles = plan["n_main"] // tm
    tri = (jnp.arange(tm)[:, None] > jnp.arange(tm)[None, :]).astype(BF16)
    kern = functools.partial(_route_kernel, n_tiles=n_tiles)
    return pl.pallas_call(
        kern,
        grid=(n_tiles + 1,),
        in_specs=[
            pl.BlockSpec((tm, d), lambda i: (jnp.minimum(i, n_tiles - 1), 0)),
            _const_spec(o_meta.shape),
            pl.BlockSpec((tm, d), lambda i: (i, 0)),
            _const_spec(wout.shape), _const_spec(g.shape), _const_spec(wr.shape),
            _const_spec(br.shape), _const_spec(tri.shape),
        ],
        out_specs=[
            pl.BlockSpec((tm, d), lambda i: (i, 0)),
            pl.BlockSpec((tm, d), lambda i: (i, 0)),
            pl.BlockSpec((tm, N_EXPERTS), lambda i: (i, 0)),
            pl.BlockSpec((8, LANES), lambda i: (0, 0)),
        ],
        out_shape=[jax.ShapeDtypeStruct(((n_tiles + 1) * tm, d), F32),
                   jax.ShapeDtypeStruct(((n_tiles + 1) * tm, d), F32),
                   jax.ShapeDtypeStruct(((n_tiles + 1) * tm, N_EXPERTS), F32),
                   jax.ShapeDtypeStruct((8, LANES), F32)],
        scratch_shapes=[pltpu.VMEM((8, LANES), F32)],
        compiler_params=pltpu.CompilerParams(
            dimension_semantics=("arbitrary",), vmem_limit_bytes=VMEM_LIMIT),
        name="outproj_router",
    )(o_main, o_meta, h, wout, g, wr, br, tri)


def _dispatch_kernel(pos_ref, tail_ref, x_ref, xs_ref, zero_ref, sem, zsem, *, tm, te):
    i = pl.program_id(0)

    @pl.when(i == 0)
    def _():
        zero_ref[...] = jnp.zeros(zero_ref.shape, F32)

        def zero_tile(start):
            cp = pltpu.make_async_copy(
                zero_ref, xs_ref.at[pl.ds(pl.multiple_of(start, te), te)], zsem)
            cp.start()
            cp.wait()

        for e in range(N_EXPERTS):
            @pl.when(tail_ref[e] >= 0)
            def _():
                zero_tile(tail_ref[e])

        def zero_unused(u, carry):
            zero_tile(u * te)
            return carry

        lax.fori_loop(tail_ref[N_EXPERTS], xs_ref.shape[0] // te, zero_unused, 0)

    def issue(t, carry):
        for k in range(2):
            pltpu.make_async_copy(
                x_ref.at[pl.ds(t, 1)], xs_ref.at[pl.ds(pos_ref[0, k, t], 1)], sem).start()
        return carry

    lax.fori_loop(0, tm, issue, 0)
    for k in range(2):
        pltpu.make_async_copy(x_ref, xs_ref.at[pl.ds(0, tm)], sem).wait()


def _dispatch(xn, pos, tails, plan):
    tm = plan["tm"]
    te = plan["te"]
    d = xn.shape[1]
    kern = functools.partial(_dispatch_kernel, tm=tm, te=te)
    return pl.pallas_call(
        kern,
        grid=(xn.shape[0] // tm,),
        in_specs=[
            pl.BlockSpec((1, 2, tm), lambda i: (i, 0, 0), memory_space=pltpu.SMEM),
            pl.BlockSpec(memory_space=pltpu.SMEM),
            pl.BlockSpec((tm, d), lambda i: (i, 0)),
        ],
        out_specs=pl.BlockSpec(memory_space=pl.ANY),
        out_shape=jax.ShapeDtypeStruct((plan["rows_sorted"], d), F32),
        scratch_shapes=[pltpu.VMEM((te, d), F32), pltpu.SemaphoreType.DMA(()),
                        pltpu.SemaphoreType.DMA(())],
        compiler_params=pltpu.CompilerParams(
            dimension_semantics=("arbitrary",), vmem_limit_bytes=VMEM_LIMIT),
        name="moe_dispatch",
    )(pos, tails, xn)


def _expert_kernel(te_ref, nu_ref, x_ref, wg_ref, wu_ref, wd_ref, out_ref):
    j = pl.program_id(0)
    c = pl.program_id(1)

    @pl.when(j < nu_ref[0])
    def _():
        xb = x_ref[...].astype(BF16)
        act = (_silu(_dot(xb, wg_ref[...])) * _dot(xb, wu_ref[...])).astype(BF16)
        y = _dot(act, wd_ref[...])

        @pl.when(c == 0)
        def _():
            out_ref[...] = y

        @pl.when(c != 0)
        def _():
            out_ref[...] += y

    @pl.when((j >= nu_ref[0]) & (c == 0))
    def _():
        out_ref[...] = jnp.zeros(out_ref.shape, F32)


def _experts(xs, tile_expert, n_used, wg, wu, wd, plan):
    te = plan["te"]
    fc = plan["fc"]
    d = xs.shape[1]
    n_tiles = xs.shape[0] // te
    n_c = wg.shape[2] // fc

    def row(j, c, te_ref, nu_ref):
        return (jnp.minimum(j, nu_ref[0] - 1), 0)

    def chunk(j, c, nu_ref):
        return jnp.where(j < nu_ref[0], c, n_c - 1)

    def w_in(j, c, te_ref, nu_ref):
        return (te_ref[jnp.minimum(j, nu_ref[0] - 1)], 0, chunk(j, c, nu_ref))

    def w_out(j, c, te_ref, nu_ref):
        return (te_ref[jnp.minimum(j, nu_ref[0] - 1)], chunk(j, c, nu_ref), 0)

    grid_spec = pltpu.PrefetchScalarGridSpec(
        num_scalar_prefetch=2,
        grid=(n_tiles, n_c),
        in_specs=[
            pl.BlockSpec((te, d), row),
            pl.BlockSpec((None, d, fc), w_in),
            pl.BlockSpec((None, d, fc), w_in),
            pl.BlockSpec((None, fc, d), w_out),
        ],
        out_specs=pl.BlockSpec((te, d), lambda j, c, te_ref, nu_ref: (j, 0)),
    )
    return pl.pallas_call(
        _expert_kernel,
        grid_spec=grid_spec,
        out_shape=jax.ShapeDtypeStruct(xs.shape, F32),
        compiler_params=pltpu.CompilerParams(
            dimension_semantics=("arbitrary", "arbitrary"), vmem_limit_bytes=VMEM_LIMIT),
        name="moe_experts",
    )(tile_expert, n_used, xs, wg, wu, wd)


def _combine_kernel(pos_ref, h_ref, route_ref, g_ref, ys_ref, outp_ref, outs_ref, y_ref, sem,
                    *, tm, n_p_tiles):
    i = pl.program_id(0)

    def issue(t, carry):
        for k in range(2):
            pltpu.make_async_copy(
                ys_ref.at[pl.ds(pos_ref[0, k, t], 1)], y_ref.at[k, pl.ds(t, 1)], sem).start()
        return carry

    lax.fori_loop(0, tm, issue, 0)
    for k in range(2):
        pltpu.make_async_copy(ys_ref.at[pl.ds(0, tm)], y_ref.at[k], sem).wait()

    g1 = route_ref[:, 4:5]
    g2 = route_ref[:, 5:6]
    h = h_ref[...] + g1 * y_ref[0] + g2 * y_ref[1]
    y = _rms(h, g_ref[...])

    @pl.when(i < n_p_tiles)
    def _():
        outp_ref[...] = y

    @pl.when(i >= n_p_tiles)
    def _():
        outs_ref[...] = y


def _combine(h, route, pos, ys, g, plan):
    tm = plan["tm"]
    d = h.shape[1]
    n_tiles = plan["n_main"] // tm
    n_p_tiles = plan["n_p"] // tm
    kern = functools.partial(_combine_kernel, tm=tm, n_p_tiles=n_p_tiles)
    return pl.pallas_call(
        kern,
        grid=(n_tiles,),
        in_specs=[
            pl.BlockSpec((1, 2, tm), lambda i: (i, 0, 0), memory_space=pltpu.SMEM),
            pl.BlockSpec((tm, d), lambda i: (i, 0)),
            pl.BlockSpec((tm, N_EXPERTS), lambda i: (i, 0)),
            _const_spec(g.shape),
            pl.BlockSpec(memory_space=pl.ANY),
        ],
        out_specs=[
            pl.BlockSpec((tm, d), lambda i: (jnp.minimum(i, n_p_tiles - 1), 0)),
            pl.BlockSpec((tm, d), lambda i: (jnp.maximum(i - n_p_tiles, 0), 0)),
        ],
        out_shape=[jax.ShapeDtypeStruct((plan["n_p"], d), F32),
                   jax.ShapeDtypeStruct((plan["n_main"] - plan["n_p"], d), F32)],
        scratch_shapes=[pltpu.VMEM((2, tm, d), F32), pltpu.SemaphoreType.DMA(())],
        compiler_params=pltpu.CompilerParams(
            dimension_semantics=("arbitrary",), vmem_limit_bytes=VMEM_LIMIT),
        name="moe_combine_norm",
    )(pos, h, route, g, ys)


def _make_plan(b_p, t_p, b_s, t_s, d_ff_expert):
    tm = 512 if t_p % 512 == 0 and t_s % 512 == 0 else 128
    te = 1024 if tm == 512 else 256
    n_p = b_p * t_p
    n_main = n_p + b_s * t_s
    n_seq = b_p + b_s
    assert n_seq * N_META <= tm and t_p % tm == 0 and t_s % tm == 0
    np_rows = n_main + tm
    rows_p, rows_s = t_p // GRID_W, t_s // GRID_W
    cr = 64 if rows_p % 64 == 0 and rows_s % 64 == 0 else 8
    assert rows_p % cr == 0 and rows_s % cr == 0 and cr >= NA_WIN_ROWS
    fc = 896 if d_ff_expert % 896 == 0 else d_ff_expert
    return dict(tm=tm, te=te, n_p=n_p, n_main=n_main, n_seq=n_seq, np_rows=np_rows, t_p=t_p,
                t_s=t_s, cr=cr, fc=fc, rows_sorted=2 * np_rows + N_EXPERTS * te)


def kernel(x_prompt, x_sample, meta_tokens, norm_mix, norm_ffn, norm_final, even_w_in, pool_w,
           pool_scale, conv_w, even_w_out, ffn_w_gate, ffn_w_up, ffn_w_down, na_w_qkv,
           na_rel_bias, na_meta_bias, na_w_out, moe_w_router, moe_b_router, moe_w_gate,
           moe_w_up, moe_w_down):
    b_p, t_p, d = x_prompt.shape
    b_s, t_s, _ = x_sample.shape
    plan = _make_plan(b_p, t_p, b_s, t_s, moe_w_gate.shape[-1])
    tm, te = plan["tm"], plan["te"]
    n_main, n_seq, np_rows = plan["n_main"], plan["n_seq"], plan["np_rows"]
    n_meta_rows = n_seq * N_META

    h = jnp.concatenate([
        x_prompt.reshape(-1, d), x_sample.reshape(-1, d),
        jnp.tile(meta_tokens, (n_seq, 1)),
        jnp.zeros((np_rows - n_main - n_meta_rows, d), F32)], axis=0)
    xfirst = jnp.concatenate([x_prompt[:, :EXT].reshape(-1, d), x_sample[:, :EXT].reshape(-1, d)])

    row = lambda v: v.reshape(1, -1).astype(F32)
    h = _even_layer(h, meta_tokens, xfirst, row(norm_mix[0]), even_w_in[0].astype(BF16),
                    pool_w[0].astype(BF16), row(pool_scale[0]), conv_w[0],
                    even_w_out[0].astype(BF16), plan)
    h = _ffn_layer(h, row(norm_ffn[0]), ffn_w_gate[0].astype(BF16), ffn_w_up[0].astype(BF16),
                   ffn_w_down[0].astype(BF16), plan)

    width = NA_HEADS * HEAD_DIM
    qscale = jnp.concatenate([jnp.full((width,), HEAD_DIM ** -0.5, F32), jnp.ones((2 * width,), F32)])
    qkv = _qkv_layer(h, row(norm_mix[1]), (na_w_qkv[0] * qscale).astype(BF16), plan)
    n_hb = NA_HEADS // HEADS_PER_BLOCK
    o_main, o_meta = _attn_layer(
        qkv, _attn_bias_table(na_rel_bias[0]),
        na_meta_bias[0].reshape(n_hb, HEADS_PER_BLOCK, 1, N_META), plan)
    o_meta = jnp.concatenate([o_meta, jnp.zeros((tm - n_meta_rows, width), BF16)])

    h, xn, route, cnt = _route_layer(
        o_main, o_meta, h, na_w_out[0].astype(BF16), row(norm_ffn[1]),
        moe_w_router[0].astype(BF16), row(moe_b_router[0]), plan)
    counts = cnt[0, :N_EXPERTS].astype(jnp.int32)
    padded = (counts + te - 1) // te * te
    ends = jnp.cumsum(padded)
    starts = ends - padded
    e12 = route[:, 0:2].astype(jnp.int32)
    pos = starts[e12] + route[:, 2:4].astype(jnp.int32)
    pos = pos.reshape(np_rows // tm, tm, 2).transpose(0, 2, 1)
    tails = jnp.concatenate([jnp.where(padded > 0, ends - te, -1), ends[-1:] // te])
    tails = tails.astype(jnp.int32)
    n_row_tiles = plan["rows_sorted"] // te
    tile_expert = jnp.minimum(
        jnp.searchsorted(ends, jnp.arange(n_row_tiles) * te, side="right"),
        N_EXPERTS - 1).astype(jnp.int32)
    n_used = (ends[-1:] // te).astype(jnp.int32)

    xs = _dispatch(xn, pos, tails, plan)
    ys = _experts(xs, tile_expert, n_used, moe_w_gate[0].astype(BF16), moe_w_up[0].astype(BF16),
                  moe_w_down[0].astype(BF16), plan)
    y_p, y_s = _combine(h, route, pos, ys, row(norm_final), plan)
    return (y_p.reshape(b_p, t_p, d), y_s.reshape(b_s, t_s, d))
```

```python
import functools

import jax
import jax.numpy as jnp
import numpy as np
from jax import lax
from jax.experimental import pallas as pl
from jax.experimental.pallas import tpu as pltpu

F32 = jnp.float32
BF16 = jnp.bfloat16

N_META = 16
GRID_W = 64
RMS_EPS = 1e-6
POOL_WINDOWS = (2, 4, 8, 16)
POOL_GROUP = 128
POOL_WIDTH = 512
CONV_WIDTH = 512
NA_HEADS = 16
HEAD_DIM = 64
NA_WIN_ROWS = 8
NA_WIN_COLS = 16
N_EXPERTS = 8
LANES = 128
HEADS_PER_BLOCK = LANES // HEAD_DIM
HALO_ROWS = NA_WIN_ROWS // 2
Q_ROWS = 4
WIN_ROWS = Q_ROWS + NA_WIN_ROWS
NQ = Q_ROWS * GRID_W
NK = WIN_ROWS * GRID_W
EXT = 16
VMEM_LIMIT = 56 * 1024 * 1024


def _const_spec(shape):
    nd = len(shape)
    return pl.BlockSpec(shape, lambda *_: (0,) * nd, pipeline_mode=pl.Buffered(1))


def _rms(x, g):
    ms = jnp.mean(x * x, axis=-1, keepdims=True)
    return x * lax.rsqrt(ms + RMS_EPS) * g


def _dot(a, b):
    return jnp.dot(a, b, preferred_element_type=F32)


def _dot_nt(a, b):
    return lax.dot_general(a, b, (((1,), (1,)), ((), ())), preferred_element_type=F32)


def _silu(x):
    return x * jax.nn.sigmoid(x)


def _even_mix(xe, pos, seq_len, g, win_ref, pw_ref, ps, cw, wout_ref):
    n = xe.shape[0]
    xn = _rms(xe, g).astype(BF16)
    z = _dot(xn, win_ref[...])
    a_parts = []
    for gi, k in enumerate(POOL_WINDOWS):
        p = z[:, gi * POOL_GROUP:(gi + 1) * POOL_GROUP]
        s = p
        w = 1
        while w < k:
            s = s + pltpu.roll(s, n - w, 0)
            w *= 2
        left = k // 2
        right = k - 1 - left
        s = pltpu.roll(s, left, 0)
        cnt = jnp.minimum(pos + right + 1, seq_len) - jnp.maximum(pos - left, 0)
        inv = 1.0 / jnp.maximum(cnt, 1).astype(F32)
        pooled = (s * inv - p).astype(BF16)
        a_parts.append(_dot(pooled, pw_ref[gi]))
    a = jnp.concatenate(a_parts, axis=-1) * ps
    gb = z[:, POOL_WIDTH:POOL_WIDTH + CONV_WIDTH]
    gc = z[:, POOL_WIDTH + CONV_WIDTH:POOL_WIDTH + 2 * CONV_WIDTH]
    hc = z[:, POOL_WIDTH + 2 * CONV_WIDTH:]
    u = gc * hc
    conv = pltpu.roll(u, 1, 0) * cw[0:1] + u * cw[1:2] + pltpu.roll(u, n - 1, 0) * cw[2:3]
    c = gb * conv
    return (_dot(a.astype(BF16), wout_ref[0:POOL_WIDTH, :])
            + _dot(c.astype(BF16), wout_ref[POOL_WIDTH:, :]))


def _even_kernel(tile_ref, prev_ref, next_ref, meta_ref, first_ref, g_ref, win_ref, pw_ref,
                 ps_ref, cw_ref, wout_ref, out_ref, xe_ref, *, tm, n_tiles, tiles_p, n_p_tiles,
                 tiles_s, t_p, t_s, n_seq):
    i = pl.program_id(0)
    weights = (g_ref[...], win_ref, pw_ref, ps_ref[...], cw_ref[...], wout_ref)

    @pl.when(i < n_tiles)
    def _main():
        is_p = i < n_p_tiles
        tin = jnp.where(is_p, i % tiles_p, (i - n_p_tiles) % tiles_s)
        last = jnp.where(is_p, tiles_p, tiles_s) - 1
        seq_len = jnp.where(is_p, t_p, t_s) + N_META
        n = tm + 2 * EXT

        @pl.when(tin == 0)
        def _():
            xe_ref[0:EXT, :] = meta_ref[...]

        @pl.when(tin != 0)
        def _():
            xe_ref[0:EXT, :] = prev_ref[...]

        xe_ref[EXT:EXT + tm, :] = tile_ref[...]

        @pl.when(tin == last)
        def _():
            xe_ref[EXT + tm:n, :] = jnp.zeros((EXT, xe_ref.shape[1]), F32)

        @pl.when(tin != last)
        def _():
            xe_ref[EXT + tm:n, :] = next_ref[...]

        pos = lax.broadcasted_iota(jnp.int32, (n, 1), 0) + tin * tm
        y = _even_mix(xe_ref[0:n, :], pos, seq_len, *weights)
        out_ref[...] = tile_ref[...] + y[EXT:EXT + tm]

    @pl.when(i == n_tiles)
    def _meta():
        grp = 3 * EXT
        n = grp * n_seq
        d = xe_ref.shape[1]
        for b in range(n_seq):
            xe_ref[grp * b:grp * b + EXT, :] = jnp.zeros((EXT, d), F32)
            xe_ref[grp * b + EXT:grp * b + 2 * EXT, :] = tile_ref[EXT * b:EXT * (b + 1), :]
            xe_ref[grp * b + 2 * EXT:grp * (b + 1), :] = first_ref[EXT * b:EXT * (b + 1), :]
        pos = lax.broadcasted_iota(jnp.int32, (n, 1), 0) % grp - EXT
        y = _even_mix(xe_ref[0:n, :], pos, jnp.int32(1 << 30), *weights)
        out_ref[...] = jnp.zeros(out_ref.shape, F32)
        for b in range(n_seq):
            out_ref[EXT * b:EXT * (b + 1), :] = (
                tile_ref[EXT * b:EXT * (b + 1), :] + y[grp * b + EXT:grp * b + 2 * EXT])


def _even_layer(h, meta, xfirst, g, win, pw, ps, cw, wout, plan):
    tm = plan["tm"]
    d = h.shape[1]
    n_tiles = plan["n_main"] // tm
    blk = tm // EXT
    last_blk = h.shape[0] // EXT - 1
    kern = functools.partial(
        _even_kernel, tm=tm, n_tiles=n_tiles, tiles_p=plan["t_p"] // tm,
        n_p_tiles=plan["n_p"] // tm, tiles_s=plan["t_s"] // tm, t_p=plan["t_p"], t_s=plan["t_s"],
        n_seq=plan["n_seq"])
    xe_rows = max(tm + 2 * EXT, 3 * EXT * plan["n_seq"])
    return pl.pallas_call(
        kern,
        grid=(n_tiles + 1,),
        in_specs=[
            pl.BlockSpec((tm, d), lambda i: (i, 0)),
            pl.BlockSpec((EXT, d), lambda i: (jnp.maximum(i * blk - 1, 0), 0)),
            pl.BlockSpec((EXT, d), lambda i: (jnp.minimum((i + 1) * blk, last_blk), 0)),
            _const_spec(meta.shape), _const_spec(xfirst.shape), _const_spec(g.shape),
            _const_spec(win.shape), _const_spec(pw.shape), _const_spec(ps.shape),
            _const_spec(cw.shape), _const_spec(wout.shape),
        ],
        out_specs=pl.BlockSpec((tm, d), lambda i: (i, 0)),
        out_shape=jax.ShapeDtypeStruct(h.shape, F32),
        scratch_shapes=[pltpu.VMEM((xe_rows, d), F32)],
        compiler_params=pltpu.CompilerParams(
            dimension_semantics=("arbitrary",), vmem_limit_bytes=VMEM_LIMIT),
        name="even_mixer",
    )(h, h, h, meta, xfirst, g, win, pw, ps, cw, wout)


def _ffn_kernel(x_ref, g_ref, wg_ref, wu_ref, wd_ref, out_ref):
    x = x_ref[...]
    xn = _rms(x, g_ref[...]).astype(BF16)
    act = (_silu(_dot(xn, wg_ref[...])) * _dot(xn, wu_ref[...])).astype(BF16)
    out_ref[...] = x + _dot(act, wd_ref[...])


def _ffn_layer(h, g, wg, wu, wd, plan):
    tm = plan["tm"]
    d = h.shape[1]
    return pl.pallas_call(
        _ffn_kernel,
        grid=(h.shape[0] // tm,),
        in_specs=[pl.BlockSpec((tm, d), lambda i: (i, 0)), _const_spec(g.shape),
                  _const_spec(wg.shape), _const_spec(wu.shape), _const_spec(wd.shape)],
        out_specs=pl.BlockSpec((tm, d), lambda i: (i, 0)),
        out_shape=jax.ShapeDtypeStruct(h.shape, F32),
        compiler_params=pltpu.CompilerParams(
            dimension_semantics=("arbitrary",), vmem_limit_bytes=VMEM_LIMIT),
        name="dense_swiglu",
    )(h, g, wg, wu, wd)


def _qkv_kernel(x_ref, g_ref, wqk_ref, wvt_ref, qk_ref, vt_ref):
    xn = _rms(x_ref[...], g_ref[...]).astype(BF16)
    qk_ref[...] = _dot(xn, wqk_ref[...]).astype(BF16)
    vt_ref[...] = _dot_nt(wvt_ref[...], xn).astype(BF16)


def _qkv_layer(h, g, wqk, wvt, plan):
    tm = plan["tm"]
    d = h.shape[1]
    return pl.pallas_call(
        _qkv_kernel,
        grid=(h.shape[0] // tm,),
        in_specs=[pl.BlockSpec((tm, d), lambda i: (i, 0)), _const_spec(g.shape),
                  _const_spec(wqk.shape), _const_spec(wvt.shape)],
        out_specs=[pl.BlockSpec((tm, wqk.shape[1]), lambda i: (i, 0)),
                   pl.BlockSpec((wvt.shape[0], tm), lambda i: (0, i))],
        out_shape=[jax.ShapeDtypeStruct((h.shape[0], wqk.shape[1]), BF16),
                   jax.ShapeDtypeStruct((wvt.shape[0], h.shape[0]), BF16)],
        compiler_params=pltpu.CompilerParams(
            dimension_semantics=("arbitrary",), vmem_limit_bytes=VMEM_LIMIT),
        name="qkv_proj",
    )(h, g, wqk, wvt)


def _attn_kernel(q_ref, kc_ref, kp_ref, kn_ref, vc_ref, vp_ref, vn_ref, qm_ref, km_ref, vm_ref,
                 vmt_ref, tab_ref, mb_ref, o_ref, om_ref, kx_ref, vx_ref, *, cr, chunks_p,
                 n_p_chunks, chunks_s, rows_p, rows_s):
    ci = pl.program_id(1)
    is_p = ci < n_p_chunks
    cin = jnp.where(is_p, ci % chunks_p, (ci - n_p_chunks) % chunks_s)
    rows = jnp.where(is_p, rows_p, rows_s)
    halo = HALO_ROWS * GRID_W
    ct = cr * GRID_W

    kx_ref[0:halo, :] = kp_ref[...]
    kx_ref[halo:halo + ct, :] = kc_ref[...]
    kx_ref[halo + ct:, :] = kn_ref[...]
    vx_ref[:, 0:halo] = vp_ref[...]
    vx_ref[:, halo:halo + ct] = vc_ref[...]
    vx_ref[:, halo + ct:] = vn_ref[...]

    lane = lax.broadcasted_iota(jnp.int32, (1, LANES), 1)
    head_masks = [(lane // HEAD_DIM) == hh for hh in range(HEADS_PER_BLOCK)]
    km = km_ref[...]
    vmt = vmt_ref[...]
    r0c = cin * cr

    def block_body(bi, carry):
        r0 = r0c + bi * Q_ROWS
        edge = jnp.where(r0 == 0, 0, jnp.where(r0 == rows - Q_ROWS, 2, 1))
        off = pl.multiple_of(bi * NQ, NQ)
        q = q_ref[pl.ds(off, NQ), :]
        kw = kx_ref[pl.ds(off, NK), :]
        vtw = vx_ref[:, pl.ds(off, NK)]
        outs = []
        for hh in range(HEADS_PER_BLOCK):
            qh = jnp.where(head_masks[hh], q, jnp.zeros_like(q))
            s = _dot_nt(kw, qh) + tab_ref[hh, edge, 0:NK, :]
            sm = _dot_nt(km, qh) + tab_ref[hh, edge, NK:NK + N_META, :]
            m = jnp.maximum(jnp.max(s, axis=0, keepdims=True),
                            jnp.max(sm, axis=0, keepdims=True))
            p = jnp.exp(s - m)
            pm = jnp.exp(sm - m)
            l = jnp.sum(p, axis=0, keepdims=True) + jnp.sum(pm, axis=0, keepdims=True)
            hs = slice(hh * HEAD_DIM, (hh + 1) * HEAD_DIM)
            o = _dot(vtw[hs, :], p.astype(BF16)) + _dot(vmt[hs, :], pm.astype(BF16))
            outs.append(o / l)
        o_ref[pl.ds(off, NQ), :] = jnp.concatenate(outs, axis=0).T.astype(BF16)
        return carry

    lax.fori_loop(0, cr // Q_ROWS, block_body, 0)

    qm = qm_ref[...]
    vm = vm_ref[...]
    out = None
    for hh in range(HEADS_PER_BLOCK):
        qh = jnp.where(head_masks[hh], qm, jnp.zeros_like(qm))
        sm = _dot_nt(qh, km) + mb_ref[hh]
        pm = jnp.exp(sm - jnp.max(sm, axis=-1, keepdims=True))
        o = _dot(pm.astype(BF16), vm) / jnp.sum(pm, axis=-1, keepdims=True)
        out = o if out is None else jnp.where(head_masks[hh], o, out)
    om_ref[...] = out.astype(BF16)


def _attn_layer(qk, vt, vm, vmt, bias_tab, meta_bias, plan):
    cr = plan["cr"]
    ct = cr * GRID_W
    halo = HALO_ROWS * GRID_W
    n_main = plan["n_main"]
    n_chunks = n_main // ct
    n_hb = NA_HEADS // HEADS_PER_BLOCK
    hb = ct // halo
    last_halo = n_main // halo - 1
    meta_blk0 = n_main // N_META
    n_p_chunks = plan["n_p"] // ct
    chunks_p = plan["t_p"] // ct
    chunks_s = plan["t_s"] // ct
    b_p = plan["n_p"] // plan["t_p"]

    def seq_of(ci):
        return jnp.where(ci < n_p_chunks, ci // chunks_p, b_p + (ci - n_p_chunks) // chunks_s)

    def cur(off):
        return pl.BlockSpec((ct, LANES), lambda h, c: (c, off + h))

    def prev(off):
        return pl.BlockSpec((halo, LANES), lambda h, c: (jnp.maximum(c * hb - 1, 0), off + h))

    def nxt(off):
        return pl.BlockSpec((halo, LANES),
                            lambda h, c: (jnp.minimum((c + 1) * hb, last_halo), off + h))

    def meta(off):
        return pl.BlockSpec((N_META, LANES), lambda h, c: (meta_blk0 + seq_of(c), off + h))

    def vt_cur():
        return pl.BlockSpec((LANES, ct), lambda h, c: (h, c))

    def vt_prev():
        return pl.BlockSpec((LANES, halo), lambda h, c: (h, jnp.maximum(c * hb - 1, 0)))

    def vt_next():
        return pl.BlockSpec((LANES, halo), lambda h, c: (h, jnp.minimum((c + 1) * hb, last_halo)))

    kern = functools.partial(
        _attn_kernel, cr=cr, chunks_p=chunks_p, n_p_chunks=n_p_chunks, chunks_s=chunks_s,
        rows_p=plan["t_p"] // GRID_W, rows_s=plan["t_s"] // GRID_W)
    width = NA_HEADS * HEAD_DIM
    return pl.pallas_call(
        kern,
        grid=(n_hb, n_chunks),
        in_specs=[
            cur(0), cur(n_hb), prev(n_hb), nxt(n_hb), vt_cur(), vt_prev(), vt_next(),
            meta(0), meta(n_hb),
            pl.BlockSpec((N_META, LANES), lambda h, c: (seq_of(c), h)),
            pl.BlockSpec((None, LANES, N_META), lambda h, c: (seq_of(c), h, 0)),
            pl.BlockSpec((HEADS_PER_BLOCK,) + bias_tab.shape[1:], lambda h, c: (h, 0, 0, 0)),
            pl.BlockSpec((None, HEADS_PER_BLOCK, 1, N_META), lambda h, c: (h, 0, 0, 0)),
        ],
        out_specs=[
            pl.BlockSpec((ct, LANES), lambda h, c: (c, h)),
            pl.BlockSpec((N_META, LANES), lambda h, c: (seq_of(c), h)),
        ],
        out_shape=[jax.ShapeDtypeStruct((n_main, width), BF16),
                   jax.ShapeDtypeStruct((plan["n_seq"] * N_META, width), BF16)],
        scratch_shapes=[pltpu.VMEM((ct + 2 * halo, LANES), BF16),
                        pltpu.VMEM((LANES, ct + 2 * halo), BF16)],
        compiler_params=pltpu.CompilerParams(
            dimension_semantics=("arbitrary", "arbitrary"), vmem_limit_bytes=VMEM_LIMIT),
        name="neighbourhood_attention",
    )(qk, qk, qk, qk, vt, vt, vt, qk, qk, vm, vmt, bias_tab, meta_bias)


def _attn_bias_table(rel_bias, meta_bias):
    n_heads, n_row_off, n_col_off = rel_bias.shape
    half = n_col_off // 2
    ring = jnp.concatenate([
        rel_bias[:, :, half::-1],
        jnp.zeros((n_heads, n_row_off, 2 * GRID_W - n_col_off), F32),
        rel_bias[:, :, :half:-1]], axis=-1)
    toe = jnp.tile(ring, (1, 1, GRID_W))[:, :, :GRID_W * (2 * GRID_W - 1)]
    toe = toe.reshape(n_heads, n_row_off, GRID_W, 2 * GRID_W - 1)[..., :GRID_W]
    qc = np.arange(GRID_W)[None, :]
    kc = np.arange(GRID_W)[:, None]
    cstart = np.clip(qc - NA_WIN_COLS // 2, 0, GRID_W - NA_WIN_COLS)
    col_ok = (kc >= cstart) & (kc < cstart + NA_WIN_COLS)
    toe = jnp.where(col_ok[None, None], toe, -1e30)
    per_a = jnp.stack([toe[:, HALO_ROWS - 1 - a:HALO_ROWS - 1 - a + WIN_ROWS]
                       for a in range(Q_ROWS)], axis=1)
    a = np.arange(Q_ROWS)[:, None]
    b = np.arange(WIN_ROWS)[None, :]
    row_ok = np.stack([(b >= HALO_ROWS) & (a >= 0),
                       (b - a >= 0) & (b - a < NA_WIN_ROWS),
                       (b < NA_WIN_ROWS) & (a >= 0)])
    tab = jnp.where(row_ok[None, :, :, :, None, None], per_a[:, None], -1e30)
    tab = jnp.transpose(tab, (0, 1, 3, 4, 2, 5)).reshape(n_heads, 3, NK, NQ)
    meta = jnp.broadcast_to(meta_bias[:, None, :, None], (n_heads, 3, N_META, NQ))
    return jnp.concatenate([tab, meta], axis=2).astype(F32)


def _route_kernel(o_ref, om_ref, h_ref, wout_ref, g_ref, wr_ref, br_ref, tri_ref,
                  h_out, xn_out, route_out, cnt_out, cnt_ref, *, n_tiles):
    i = pl.program_id(0)

    @pl.when(i == 0)
    def _():
        cnt_ref[...] = jnp.zeros(cnt_ref.shape, F32)

    def body(o):
        h = h_ref[...] + _dot(o, wout_ref[...])
        h_out[...] = h
        xn = _rms(h, g_ref[...])
        xn_out[...] = xn
        logits = _dot(xn.astype(BF16), wr_ref[...]) + br_ref[...]
        tm = logits.shape[0]
        lane = lax.broadcasted_iota(jnp.int32, (tm, N_EXPERTS), 1)
        m1 = jnp.max(logits, axis=-1, keepdims=True)
        i1 = jnp.min(jnp.where(logits == m1, lane, N_EXPERTS), axis=-1, keepdims=True)
        oh1 = lane == i1
        rest = jnp.where(oh1, -jnp.inf, logits)
        m2 = jnp.max(rest, axis=-1, keepdims=True)
        i2 = jnp.min(jnp.where(rest == m2, lane, N_EXPERTS), axis=-1, keepdims=True)
        oh2 = lane == i2
        t = jnp.exp(m2 - m1)
        g1 = 1.0 / (1.0 + t)
        g2 = t * g1
        oh = jnp.where(oh1 | oh2, 1.0, 0.0)
        cum = _dot(tri_ref[...], oh.astype(BF16)) + cnt_ref[0:1, 0:N_EXPERTS]
        r1 = jnp.sum(jnp.where(oh1, cum, 0.0), axis=-1, keepdims=True)
        r2 = jnp.sum(jnp.where(oh2, cum, 0.0), axis=-1, keepdims=True)
        cnt_ref[0:1, 0:N_EXPERTS] = (cnt_ref[0:1, 0:N_EXPERTS]
                                     + jnp.sum(oh, axis=0, keepdims=True))
        route = jnp.zeros((tm, N_EXPERTS), F32)
        for col, val in enumerate([i1.astype(F32), i2.astype(F32), r1, r2, g1, g2]):
            route = jnp.where(lane == col, val, route)
        route_out[...] = route
        cnt_out[...] = cnt_ref[...]

    @pl.when(i < n_tiles)
    def _():
        body(o_ref[...])

    @pl.when(i == n_tiles)
    def _():
        body(om_ref[...])


def _route_layer(o_main, o_meta, h, wout, g, wr, br, plan):
    tm = plan["tm"]
    d = h.shape[1]
    n_tiles = plan["n_main"] // tm
    tri = (jnp.arange(tm)[:, None] > jnp.arange(tm)[None, :]).astype(BF16)
    kern = functools.partial(_route_kernel, n_tiles=n_tiles)
    return pl.pallas_call(
        kern,
        grid=(n_tiles + 1,),
        in_specs=[
            pl.BlockSpec((tm, d), lambda i: (jnp.minimum(i, n_tiles - 1), 0)),
            _const_spec(o_meta.shape),
            pl.BlockSpec((tm, d), lambda i: (i, 0)),
            _const_spec(wout.shape), _const_spec(g.shape), _const_spec(wr.shape),
            _const_spec(br.shape), _const_spec(tri.shape),
        ],
        out_specs=[
            pl.BlockSpec((tm, d), lambda i: (i, 0)),
            pl.BlockSpec((tm, d), lambda i: (i, 0)),
            pl.BlockSpec((tm, N_EXPERTS), lambda i: (i, 0)),
            pl.BlockSpec((8, LANES), lambda i: (0, 0)),
        ],
        out_shape=[jax.ShapeDtypeStruct(((n_tiles + 1) * tm, d), F32),
                   jax.ShapeDtypeStruct(((n_tiles + 1) * tm, d), F32),
                   jax.ShapeDtypeStruct(((n_tiles + 1) * tm, N_EXPERTS), F32),
                   jax.ShapeDtypeStruct((8, LANES), F32)],
        scratch_shapes=[pltpu.VMEM((8, LANES), F32)],
        compiler_params=pltpu.CompilerParams(
            dimension_semantics=("arbitrary",), vmem_limit_bytes=VMEM_LIMIT),
        name="outproj_router",
    )(o_main, o_meta, h, wout, g, wr, br, tri)


def _dispatch_kernel(pos_ref, tail_ref, x_ref, xs_ref, zero_ref, sem, zsem, *, tm, te):
    i = pl.program_id(0)

    @pl.when(i == 0)
    def _():
        zero_ref[...] = jnp.zeros(zero_ref.shape, F32)

        def zero_tile(start):
            cp = pltpu.make_async_copy(
                zero_ref, xs_ref.at[pl.ds(pl.multiple_of(start, te), te)], zsem)
            cp.start()
            cp.wait()

        for e in range(N_EXPERTS):
            @pl.when(tail_ref[e] >= 0)
            def _():
                zero_tile(tail_ref[e])

        def zero_unused(u, carry):
            zero_tile(u * te)
            return carry

        lax.fori_loop(tail_ref[N_EXPERTS], xs_ref.shape[0] // te, zero_unused, 0)

    def issue(t, carry):
        for k in range(2):
            pltpu.make_async_copy(
                x_ref.at[pl.ds(t, 1)], xs_ref.at[pl.ds(pos_ref[0, k, t], 1)], sem).start()
        return carry

    lax.fori_loop(0, tm, issue, 0)
    for k in range(2):
        pltpu.make_async_copy(x_ref, xs_ref.at[pl.ds(0, tm)], sem).wait()


def _dispatch(xn, pos, tails, plan):
    tm = plan["tm"]
    te = plan["te"]
    d = xn.shape[1]
    kern = functools.partial(_dispatch_kernel, tm=tm, te=te)
    return pl.pallas_call(
        kern,
        grid=(xn.shape[0] // tm,),
        in_specs=[
            pl.BlockSpec((1, 2, tm), lambda i: (i, 0, 0), memory_space=pltpu.SMEM),
            pl.BlockSpec(memory_space=pltpu.SMEM),
            pl.BlockSpec((tm, d), lambda i: (i, 0)),
        ],
        out_specs=pl.BlockSpec(memory_space=pl.ANY),
        out_shape=jax.ShapeDtypeStruct((plan["rows_sorted"], d), F32),
        scratch_shapes=[pltpu.VMEM((te, d), F32), pltpu.SemaphoreType.DMA(()),
                        pltpu.SemaphoreType.DMA(())],
        compiler_params=pltpu.CompilerParams(
            dimension_semantics=("arbitrary",), vmem_limit_bytes=VMEM_LIMIT),
        name="moe_dispatch",
    )(pos, tails, xn)


def _expert_kernel(te_ref, nu_ref, x_ref, wg_ref, wu_ref, wd_ref, out_ref):
    j = pl.program_id(0)
    c = pl.program_id(1)

    @pl.when(j < nu_ref[0])
    def _():
        xb = x_ref[...].astype(BF16)
        act = (_silu(_dot(xb, wg_ref[...])) * _dot(xb, wu_ref[...])).astype(BF16)
        y = _dot(act, wd_ref[...])

        @pl.when(c == 0)
        def _():
            out_ref[...] = y

        @pl.when(c != 0)
        def _():
            out_ref[...] += y

    @pl.when((j >= nu_ref[0]) & (c == 0))
    def _():
        out_ref[...] = jnp.zeros(out_ref.shape, F32)


def _experts(xs, tile_expert, n_used, wg, wu, wd, plan):
    te = plan["te"]
    fc = plan["fc"]
    d = xs.shape[1]
    n_tiles = xs.shape[0] // te
    n_c = wg.shape[2] // fc

    def row(j, c, te_ref, nu_ref):
        return (jnp.minimum(j, nu_ref[0] - 1), 0)

    def chunk(j, c, nu_ref):
        return jnp.where(j < nu_ref[0], c, n_c - 1)

    def w_in(j, c, te_ref, nu_ref):
        return (te_ref[jnp.minimum(j, nu_ref[0] - 1)], 0, chunk(j, c, nu_ref))

    def w_out(j, c, te_ref, nu_ref):
        return (te_ref[jnp.minimum(j, nu_ref[0] - 1)], chunk(j, c, nu_ref), 0)

    grid_spec = pltpu.PrefetchScalarGridSpec(
        num_scalar_prefetch=2,
        grid=(n_tiles, n_c),
        in_specs=[
            pl.BlockSpec((te, d), row),
            pl.BlockSpec((None, d, fc), w_in),
            pl.BlockSpec((None, d, fc), w_in),
            pl.BlockSpec((None, fc, d), w_out),
        ],
        out_specs=pl.BlockSpec((te, d), lambda j, c, te_ref, nu_ref: (j, 0)),
    )
    return pl.pallas_call(
        _expert_kernel,
        grid_spec=grid_spec,
        out_shape=jax.ShapeDtypeStruct(xs.shape, F32),
        compiler_params=pltpu.CompilerParams(
            dimension_semantics=("arbitrary", "arbitrary"), vmem_limit_bytes=VMEM_LIMIT),
        name="moe_experts",
    )(tile_expert, n_used, xs, wg, wu, wd)


def _combine_kernel(pos_ref, h_ref, route_ref, g_ref, ys_ref, outp_ref, outs_ref, y_ref, sem,
                    *, tm, n_p_tiles):
    i = pl.program_id(0)

    def issue(t, carry):
        for k in range(2):
            pltpu.make_async_copy(
                ys_ref.at[pl.ds(pos_ref[0, k, t], 1)], y_ref.at[k, pl.ds(t, 1)], sem).start()
        return carry

    lax.fori_loop(0, tm, issue, 0)
    for k in range(2):
        pltpu.make_async_copy(ys_ref.at[pl.ds(0, tm)], y_ref.at[k], sem).wait()

    g1 = route_ref[:, 4:5]
    g2 = route_ref[:, 5:6]
    h = h_ref[...] + g1 * y_ref[0] + g2 * y_ref[1]
    y = _rms(h, g_ref[...])

    @pl.when(i < n_p_tiles)
    def _():
        outp_ref[...] = y

    @pl.when(i >= n_p_tiles)
    def _():
        outs_ref[...] = y


def _combine(h, route, pos, ys, g, plan):
    tm = plan["tm"]
    d = h.shape[1]
    n_tiles = plan["n_main"] // tm
    n_p_tiles = plan["n_p"] // tm
    kern = functools.partial(_combine_kernel, tm=tm, n_p_tiles=n_p_tiles)
    return pl.pallas_call(
        kern,
        grid=(n_tiles,),
        in_specs=[
            pl.BlockSpec((1, 2, tm), lambda i: (i, 0, 0), memory_space=pltpu.SMEM),
            pl.BlockSpec((tm, d), lambda i: (i, 0)),
            pl.BlockSpec((tm, N_EXPERTS), lambda i: (i, 0)),
            _const_spec(g.shape),
            pl.BlockSpec(memory_space=pl.ANY),
        ],
        out_specs=[
            pl.BlockSpec((tm, d), lambda i: (jnp.minimum(i, n_p_tiles - 1), 0)),
            pl.BlockSpec((tm, d), lambda i: (jnp.maximum(i - n_p_tiles, 0), 0)),
        ],
        out_shape=[jax.ShapeDtypeStruct((plan["n_p"], d), F32),
                   jax.ShapeDtypeStruct((plan["n_main"] - plan["n_p"], d), F32)],
        scratch_shapes=[pltpu.VMEM((2, tm, d), F32), pltpu.SemaphoreType.DMA(())],
        compiler_params=pltpu.CompilerParams(
            dimension_semantics=("arbitrary",), vmem_limit_bytes=VMEM_LIMIT),
        name="moe_combine_norm",
    )(pos, h, route, g, ys)


def _make_plan(b_p, t_p, b_s, t_s, d_ff_expert):
    tm = 512 if t_p % 512 == 0 and t_s % 512 == 0 else 128
    te = 1024 if tm == 512 else 256
    n_p = b_p * t_p
    n_main = n_p + b_s * t_s
    n_seq = b_p + b_s
    assert n_seq * N_META <= tm and t_p % tm == 0 and t_s % tm == 0
    np_rows = n_main + tm
    rows_p, rows_s = t_p // GRID_W, t_s // GRID_W
    cr = 64 if rows_p % 64 == 0 and rows_s % 64 == 0 else 8
    assert rows_p % cr == 0 and rows_s % cr == 0 and cr >= NA_WIN_ROWS
    fc = 896 if d_ff_expert % 896 == 0 else d_ff_expert
    return dict(tm=tm, te=te, n_p=n_p, n_main=n_main, n_seq=n_seq, np_rows=np_rows, t_p=t_p,
                t_s=t_s, cr=cr, fc=fc, rows_sorted=2 * np_rows + N_EXPERTS * te)


def kernel(x_prompt, x_sample, meta_tokens, norm_mix, norm_ffn, norm_final, even_w_in, pool_w,
           pool_scale, conv_w, even_w_out, ffn_w_gate, ffn_w_up, ffn_w_down, na_w_qkv,
           na_rel_bias, na_meta_bias, na_w_out, moe_w_router, moe_b_router, moe_w_gate,
           moe_w_up, moe_w_down):
    b_p, t_p, d = x_prompt.shape
    b_s, t_s, _ = x_sample.shape
    plan = _make_plan(b_p, t_p, b_s, t_s, moe_w_gate.shape[-1])
    tm, te = plan["tm"], plan["te"]
    n_main, n_seq, np_rows = plan["n_main"], plan["n_seq"], plan["np_rows"]
    n_meta_rows = n_seq * N_META

    h = jnp.concatenate([
        x_prompt.reshape(-1, d), x_sample.reshape(-1, d),
        jnp.tile(meta_tokens, (n_seq, 1)),
        jnp.zeros((np_rows - n_main - n_meta_rows, d), F32)], axis=0)
    xfirst = jnp.concatenate([x_prompt[:, :EXT].reshape(-1, d), x_sample[:, :EXT].reshape(-1, d)])

    row = lambda v: v.reshape(1, -1).astype(F32)
    h = _even_layer(h, meta_tokens, xfirst, row(norm_mix[0]), even_w_in[0].astype(BF16),
                    pool_w[0].astype(BF16), row(pool_scale[0]), conv_w[0],
                    even_w_out[0].astype(BF16), plan)
    h = _ffn_layer(h, row(norm_ffn[0]), ffn_w_gate[0].astype(BF16), ffn_w_up[0].astype(BF16),
                   ffn_w_down[0].astype(BF16), plan)

    width = NA_HEADS * HEAD_DIM
    qscale = jnp.concatenate([jnp.full((width,), HEAD_DIM ** -0.5, F32), jnp.ones((width,), F32)])
    qk, vt = _qkv_layer(h, row(norm_mix[1]), (na_w_qkv[0][:, :2 * width] * qscale).astype(BF16),
                        na_w_qkv[0][:, 2 * width:].T.astype(BF16), plan)
    vt_meta = vt[:, n_main:n_main + n_meta_rows]
    n_hb = NA_HEADS // HEADS_PER_BLOCK
    o_main, o_meta = _attn_layer(
        qk, vt, vt_meta.T, vt_meta.reshape(width, n_seq, N_META).transpose(1, 0, 2),
        _attn_bias_table(na_rel_bias[0], na_meta_bias[0]),
        na_meta_bias[0].reshape(n_hb, HEADS_PER_BLOCK, 1, N_META), plan)
    o_meta = jnp.concatenate([o_meta, jnp.zeros((tm - n_meta_rows, width), BF16)])

    h, xn, route, cnt = _route_layer(
        o_main, o_meta, h, na_w_out[0].astype(BF16), row(norm_ffn[1]),
        moe_w_router[0].astype(BF16), row(moe_b_router[0]), plan)
    counts = cnt[0, :N_EXPERTS].astype(jnp.int32)
    padded = (counts + te - 1) // te * te
    ends = jnp.cumsum(padded)
    starts = ends - padded
    e12 = route[:, 0:2].astype(jnp.int32)
    start_of = jnp.sum(jnp.where(e12[..., None] == jnp.arange(N_EXPERTS), starts, 0), axis=-1)
    pos = start_of + route[:, 2:4].astype(jnp.int32)
    pos = pos.reshape(np_rows // tm, tm, 2).transpose(0, 2, 1)
    tails = jnp.concatenate([jnp.where(padded > 0, ends - te, -1), ends[-1:] // te])
    tails = tails.astype(jnp.int32)
    n_row_tiles = plan["rows_sorted"] // te
    tile_expert = jnp.minimum(
        jnp.sum(ends[None, :] <= (jnp.arange(n_row_tiles) * te)[:, None], axis=1),
        N_EXPERTS - 1).astype(jnp.int32)
    n_used = (ends[-1:] // te).astype(jnp.int32)

    xs = _dispatch(xn, pos, tails, plan)
    ys = _experts(xs, tile_expert, n_used, moe_w_gate[0].astype(BF16), moe_w_up[0].astype(BF16),
                  moe_w_down[0].astype(BF16), plan)
    y_p, y_s = _combine(h, route, pos, ys, row(norm_final), plan)
    return (y_p.reshape(b_p, t_p, d), y_s.reshape(b_s, t_s, d))
```

```python
import functools

import jax
import jax.numpy as jnp
import numpy as np
from jax import lax
from jax.experimental import pallas as pl
from jax.experimental.pallas import tpu as pltpu

F32 = jnp.float32
BF16 = jnp.bfloat16

N_META = 16
GRID_W = 64
RMS_EPS = 1e-6
POOL_WINDOWS = (2, 4, 8, 16)
POOL_GROUP = 128
POOL_WIDTH = 512
CONV_WIDTH = 512
NA_HEADS = 16
HEAD_DIM = 64
NA_WIN_ROWS = 8
NA_WIN_COLS = 16
N_EXPERTS = 8
LANES = 128
MXU_DIM = 256
HEADS_PER_BLOCK = LANES // HEAD_DIM
HALO_ROWS = NA_WIN_ROWS // 2
Q_ROWS = 4
WIN_ROWS = Q_ROWS + NA_WIN_ROWS
NQ = Q_ROWS * GRID_W
NK = WIN_ROWS * GRID_W
EXT = 16
VMEM_LIMIT = 56 * 1024 * 1024


def _const_spec(shape):
    nd = len(shape)
    return pl.BlockSpec(shape, lambda *_: (0,) * nd, pipeline_mode=pl.Buffered(1))


def _rms(x, g):
    ms = jnp.mean(x * x, axis=-1, keepdims=True)
    return x * lax.rsqrt(ms + RMS_EPS) * g


def _dot(a, b):
    return jnp.dot(a, b, preferred_element_type=F32)


def _dot_nt(a, b):
    return lax.dot_general(a, b, (((1,), (1,)), ((), ())), preferred_element_type=F32)


def _silu(x):
    return x * jax.nn.sigmoid(x)


def _even_mix(xe, pos, seq_len, g, win_ref, pw_ref, ps, cw, wout_ref):
    n = xe.shape[0]
    xn = _rms(xe, g).astype(BF16)
    z = _dot(xn, win_ref[...])
    a_parts = []
    for gi, k in enumerate(POOL_WINDOWS):
        p = z[:, gi * POOL_GROUP:(gi + 1) * POOL_GROUP]
        s = p
        w = 1
        while w < k:
            s = s + pltpu.roll(s, n - w, 0)
            w *= 2
        left = k // 2
        right = k - 1 - left
        s = pltpu.roll(s, left, 0)
        cnt = jnp.minimum(pos + right + 1, seq_len) - jnp.maximum(pos - left, 0)
        inv = 1.0 / jnp.maximum(cnt, 1).astype(F32)
        pooled = (s * inv - p).astype(BF16)
        a_parts.append(_dot(pooled, pw_ref[gi]))
    a = jnp.concatenate(a_parts, axis=-1) * ps
    gb = z[:, POOL_WIDTH:POOL_WIDTH + CONV_WIDTH]
    gc = z[:, POOL_WIDTH + CONV_WIDTH:POOL_WIDTH + 2 * CONV_WIDTH]
    hc = z[:, POOL_WIDTH + 2 * CONV_WIDTH:]
    u = gc * hc
    conv = pltpu.roll(u, 1, 0) * cw[0:1] + u * cw[1:2] + pltpu.roll(u, n - 1, 0) * cw[2:3]
    c = gb * conv
    return (_dot(a.astype(BF16), wout_ref[0:POOL_WIDTH, :])
            + _dot(c.astype(BF16), wout_ref[POOL_WIDTH:, :]))


def _even_kernel(tile_ref, prev_ref, next_ref, meta_ref, first_ref, g_ref, win_ref, pw_ref,
                 ps_ref, cw_ref, wout_ref, out_ref, xe_ref, *, tm, n_tiles, tiles_p, n_p_tiles,
                 tiles_s, t_p, t_s, n_seq):
    i = pl.program_id(0)
    weights = (g_ref[...], win_ref, pw_ref, ps_ref[...], cw_ref[...], wout_ref)

    @pl.when(i < n_tiles)
    def _main():
        is_p = i < n_p_tiles
        tin = jnp.where(is_p, i % tiles_p, (i - n_p_tiles) % tiles_s)
        last = jnp.where(is_p, tiles_p, tiles_s) - 1
        seq_len = jnp.where(is_p, t_p, t_s) + N_META
        n = tm + 2 * EXT

        @pl.when(tin == 0)
        def _():
            xe_ref[0:EXT, :] = meta_ref[...]

        @pl.when(tin != 0)
        def _():
            xe_ref[0:EXT, :] = prev_ref[...]

        xe_ref[EXT:EXT + tm, :] = tile_ref[...]

        @pl.when(tin == last)
        def _():
            xe_ref[EXT + tm:n, :] = jnp.zeros((EXT, xe_ref.shape[1]), F32)

        @pl.when(tin != last)
        def _():
            xe_ref[EXT + tm:n, :] = next_ref[...]

        pos = lax.broadcasted_iota(jnp.int32, (n, 1), 0) + tin * tm
        y = _even_mix(xe_ref[0:n, :], pos, seq_len, *weights)
        out_ref[...] = tile_ref[...] + y[EXT:EXT + tm]

    @pl.when(i == n_tiles)
    def _meta():
        grp = 3 * EXT
        n = grp * n_seq
        d = xe_ref.shape[1]
        for b in range(n_seq):
            xe_ref[grp * b:grp * b + EXT, :] = jnp.zeros((EXT, d), F32)
            xe_ref[grp * b + EXT:grp * b + 2 * EXT, :] = tile_ref[EXT * b:EXT * (b + 1), :]
            xe_ref[grp * b + 2 * EXT:grp * (b + 1), :] = first_ref[EXT * b:EXT * (b + 1), :]
        pos = lax.broadcasted_iota(jnp.int32, (n, 1), 0) % grp - EXT
        y = _even_mix(xe_ref[0:n, :], pos, jnp.int32(1 << 30), *weights)
        out_ref[...] = jnp.zeros(out_ref.shape, F32)
        for b in range(n_seq):
            out_ref[EXT * b:EXT * (b + 1), :] = (
                tile_ref[EXT * b:EXT * (b + 1), :] + y[grp * b + EXT:grp * b + 2 * EXT])


def _even_layer(h, meta, xfirst, g, win, pw, ps, cw, wout, plan):
    tm = plan["tm"]
    d = h.shape[1]
    n_tiles = plan["n_main"] // tm
    blk = tm // EXT
    last_blk = h.shape[0] // EXT - 1
    kern = functools.partial(
        _even_kernel, tm=tm, n_tiles=n_tiles, tiles_p=plan["t_p"] // tm,
        n_p_tiles=plan["n_p"] // tm, tiles_s=plan["t_s"] // tm, t_p=plan["t_p"], t_s=plan["t_s"],
        n_seq=plan["n_seq"])
    xe_rows = max(tm + 2 * EXT, 3 * EXT * plan["n_seq"])
    return pl.pallas_call(
        kern,
        grid=(n_tiles + 1,),
        in_specs=[
            pl.BlockSpec((tm, d), lambda i: (i, 0)),
            pl.BlockSpec((EXT, d), lambda i: (jnp.maximum(i * blk - 1, 0), 0)),
            pl.BlockSpec((EXT, d), lambda i: (jnp.minimum((i + 1) * blk, last_blk), 0)),
            _const_spec(meta.shape), _const_spec(xfirst.shape), _const_spec(g.shape),
            _const_spec(win.shape), _const_spec(pw.shape), _const_spec(ps.shape),
            _const_spec(cw.shape), _const_spec(wout.shape),
        ],
        out_specs=pl.BlockSpec((tm, d), lambda i: (i, 0)),
        out_shape=jax.ShapeDtypeStruct(h.shape, F32),
        scratch_shapes=[pltpu.VMEM((xe_rows, d), F32)],
        compiler_params=pltpu.CompilerParams(
            dimension_semantics=("arbitrary",), vmem_limit_bytes=VMEM_LIMIT),
        name="even_mixer",
    )(h, h, h, meta, xfirst, g, win, pw, ps, cw, wout)


def _ffn_kernel(x_ref, g_ref, wg_ref, wu_ref, wd_ref, out_ref):
    x = x_ref[...]
    xn = _rms(x, g_ref[...]).astype(BF16)
    act = (_silu(_dot(xn, wg_ref[...])) * _dot(xn, wu_ref[...])).astype(BF16)
    out_ref[...] = x + _dot(act, wd_ref[...])


def _ffn_layer(h, g, wg, wu, wd, plan):
    tm = plan["tm"]
    d = h.shape[1]
    return pl.pallas_call(
        _ffn_kernel,
        grid=(h.shape[0] // tm,),
        in_specs=[pl.BlockSpec((tm, d), lambda i: (i, 0)), _const_spec(g.shape),
                  _const_spec(wg.shape), _const_spec(wu.shape), _const_spec(wd.shape)],
        out_specs=pl.BlockSpec((tm, d), lambda i: (i, 0)),
        out_shape=jax.ShapeDtypeStruct(h.shape, F32),
        compiler_params=pltpu.CompilerParams(
            dimension_semantics=("arbitrary",), vmem_limit_bytes=VMEM_LIMIT),
        name="dense_swiglu",
    )(h, g, wg, wu, wd)


def _qkv_kernel(x_ref, g_ref, wqk_ref, wvt_ref, qk_ref, vt_ref):
    xn = _rms(x_ref[...], g_ref[...]).astype(BF16)
    qk_ref[...] = _dot(xn, wqk_ref[...]).astype(BF16)
    vt_ref[...] = _dot_nt(wvt_ref[...], xn).astype(BF16)


def _qkv_layer(h, g, wqk, wvt, plan):
    tm = plan["tm"]
    d = h.shape[1]
    return pl.pallas_call(
        _qkv_kernel,
        grid=(h.shape[0] // tm,),
        in_specs=[pl.BlockSpec((tm, d), lambda i: (i, 0)), _const_spec(g.shape),
                  _const_spec(wqk.shape), _const_spec(wvt.shape)],
        out_specs=[pl.BlockSpec((tm, wqk.shape[1]), lambda i: (i, 0)),
                   pl.BlockSpec((wvt.shape[0], tm), lambda i: (0, i))],
        out_shape=[jax.ShapeDtypeStruct((h.shape[0], wqk.shape[1]), BF16),
                   jax.ShapeDtypeStruct((wvt.shape[0], h.shape[0]), BF16)],
        compiler_params=pltpu.CompilerParams(
            dimension_semantics=("arbitrary",), vmem_limit_bytes=VMEM_LIMIT),
        name="qkv_proj",
    )(h, g, wqk, wvt)


def _attn_kernel(q_ref, kc_ref, kp_ref, kn_ref, vc_ref, vp_ref, vn_ref, qm_ref, km_ref, vm_ref,
                 vmt_ref, tab_ref, mb_ref, o_ref, om_ref, kx_ref, vx_ref, s_ref, ot_ref, *, cr,
                 chunks_p,
                 n_p_chunks, chunks_s, rows_p, rows_s):
    ci = pl.program_id(1)
    is_p = ci < n_p_chunks
    cin = jnp.where(is_p, ci % chunks_p, (ci - n_p_chunks) % chunks_s)
    rows = jnp.where(is_p, rows_p, rows_s)
    halo = HALO_ROWS * GRID_W
    ct = cr * GRID_W

    kx_ref[0:halo, :] = kp_ref[...]
    kx_ref[halo:halo + ct, :] = kc_ref[...]
    kx_ref[halo + ct:, :] = kn_ref[...]
    vx_ref[:, 0:halo] = vp_ref[...]
    vx_ref[:, halo:halo + ct] = vc_ref[...]
    vx_ref[:, halo + ct:] = vn_ref[...]

    lane = lax.broadcasted_iota(jnp.int32, (1, LANES), 1)
    head_masks = [(lane // HEAD_DIM) == hh for hh in range(HEADS_PER_BLOCK)]
    km = km_ref[...]
    vmt = vmt_ref[...]
    r0c = cin * cr

    n_blocks = cr // Q_ROWS

    def scores(bi, hh):
        off = pl.multiple_of(bi * NQ, NQ)
        q = q_ref[pl.ds(off, NQ), :]
        qh = jnp.where(head_masks[hh], q, jnp.zeros_like(q))
        s_ref[hh, 0:NK, :] = _dot_nt(kx_ref[pl.ds(off, NK), :], qh)
        s_ref[hh, NK:NK + N_META, :] = _dot_nt(km, qh)

    def softmax_pv(bi, hh):
        r0 = r0c + bi * Q_ROWS
        edge = jnp.where(r0 == 0, 0, jnp.where(r0 == rows - Q_ROWS, 2, 1))
        off = pl.multiple_of(bi * NQ, NQ)
        s = s_ref[hh] + tab_ref[hh, edge]
        p = jnp.exp(s - jnp.max(s, axis=0, keepdims=True))
        l = jnp.sum(p, axis=0, keepdims=True)
        pb = p.astype(BF16)
        hs = slice(hh * HEAD_DIM, (hh + 1) * HEAD_DIM)
        o = _dot(vx_ref[hs, pl.ds(off, NK)], pb[0:NK]) + _dot(vmt[hs, :], pb[NK:NK + N_META])
        ot_ref[hs, :] = o / l

    scores(0, 0)

    def block_body(bi, carry):
        scores(bi, 1)
        softmax_pv(bi, 0)
        scores(jnp.minimum(bi + 1, n_blocks - 1), 0)
        softmax_pv(bi, 1)
        off = pl.multiple_of(bi * NQ, NQ)
        o_ref[pl.ds(off, NQ), :] = ot_ref[...].T.astype(BF16)
        return carry

    lax.fori_loop(0, n_blocks, block_body, 0)

    qm = qm_ref[...]
    vm = vm_ref[...]
    out = None
    for hh in range(HEADS_PER_BLOCK):
        qh = jnp.where(head_masks[hh], qm, jnp.zeros_like(qm))
        sm = _dot_nt(qh, km) + mb_ref[hh]
        pm = jnp.exp(sm - jnp.max(sm, axis=-1, keepdims=True))
        o = _dot(pm.astype(BF16), vm) / jnp.sum(pm, axis=-1, keepdims=True)
        out = o if out is None else jnp.where(head_masks[hh], o, out)
    om_ref[...] = out.astype(BF16)


def _attn_layer(qk, vt, vm, vmt, bias_tab, meta_bias, plan):
    cr = plan["cr"]
    ct = cr * GRID_W
    halo = HALO_ROWS * GRID_W
    n_main = plan["n_main"]
    n_chunks = n_main // ct
    n_hb = NA_HEADS // HEADS_PER_BLOCK
    hb = ct // halo
    last_halo = n_main // halo - 1
    meta_blk0 = n_main // N_META
    n_p_chunks = plan["n_p"] // ct
    chunks_p = plan["t_p"] // ct
    chunks_s = plan["t_s"] // ct
    b_p = plan["n_p"] // plan["t_p"]

    def seq_of(ci):
        return jnp.where(ci < n_p_chunks, ci // chunks_p, b_p + (ci - n_p_chunks) // chunks_s)

    def cur(off):
        return pl.BlockSpec((ct, LANES), lambda h, c: (c, off + h))

    def prev(off):
        return pl.BlockSpec((halo, LANES), lambda h, c: (jnp.maximum(c * hb - 1, 0), off + h))

    def nxt(off):
        return pl.BlockSpec((halo, LANES),
                            lambda h, c: (jnp.minimum((c + 1) * hb, last_halo), off + h))

    def meta(off):
        return pl.BlockSpec((N_META, LANES), lambda h, c: (meta_blk0 + seq_of(c), off + h))

    def vt_cur():
        return pl.BlockSpec((LANES, ct), lambda h, c: (h, c))

    def vt_prev():
        return pl.BlockSpec((LANES, halo), lambda h, c: (h, jnp.maximum(c * hb - 1, 0)))

    def vt_next():
        return pl.BlockSpec((LANES, halo), lambda h, c: (h, jnp.minimum((c + 1) * hb, last_halo)))

    kern = functools.partial(
        _attn_kernel, cr=cr, chunks_p=chunks_p, n_p_chunks=n_p_chunks, chunks_s=chunks_s,
        rows_p=plan["t_p"] // GRID_W, rows_s=plan["t_s"] // GRID_W)
    width = NA_HEADS * HEAD_DIM
    return pl.pallas_call(
        kern,
        grid=(n_hb, n_chunks),
        in_specs=[
            cur(0), cur(n_hb), prev(n_hb), nxt(n_hb), vt_cur(), vt_prev(), vt_next(),
            meta(0), meta(n_hb),
            pl.BlockSpec((N_META, LANES), lambda h, c: (seq_of(c), h)),
            pl.BlockSpec((None, LANES, N_META), lambda h, c: (seq_of(c), h, 0)),
            pl.BlockSpec((HEADS_PER_BLOCK,) + bias_tab.shape[1:], lambda h, c: (h, 0, 0, 0)),
            pl.BlockSpec((None, HEADS_PER_BLOCK, 1, N_META), lambda h, c: (h, 0, 0, 0)),
        ],
        out_specs=[
            pl.BlockSpec((ct, LANES), lambda h, c: (c, h)),
            pl.BlockSpec((N_META, LANES), lambda h, c: (seq_of(c), h)),
        ],
        out_shape=[jax.ShapeDtypeStruct((n_main, width), BF16),
                   jax.ShapeDtypeStruct((plan["n_seq"] * N_META, width), BF16)],
        scratch_shapes=[pltpu.VMEM((ct + 2 * halo, LANES), BF16),
                        pltpu.VMEM((LANES, ct + 2 * halo), BF16),
                        pltpu.VMEM((HEADS_PER_BLOCK, NK + N_META, NQ), F32),
                        pltpu.VMEM((LANES, NQ), F32)],
        compiler_params=pltpu.CompilerParams(
            dimension_semantics=("arbitrary", "arbitrary"), vmem_limit_bytes=VMEM_LIMIT),
        name="neighbourhood_attention",
    )(qk, qk, qk, qk, vt, vt, vt, qk, qk, vm, vmt, bias_tab, meta_bias)


def _attn_bias_table(rel_bias, meta_bias):
    n_heads, n_row_off, n_col_off = rel_bias.shape
    half = n_col_off // 2
    ring = jnp.concatenate([
        rel_bias[:, :, half::-1],
        jnp.zeros((n_heads, n_row_off, 2 * GRID_W - n_col_off), F32),
        rel_bias[:, :, :half:-1]], axis=-1)
    toe = jnp.tile(ring, (1, 1, GRID_W))[:, :, :GRID_W * (2 * GRID_W - 1)]
    toe = toe.reshape(n_heads, n_row_off, GRID_W, 2 * GRID_W - 1)[..., :GRID_W]
    qc = np.arange(GRID_W)[None, :]
    kc = np.arange(GRID_W)[:, None]
    cstart = np.clip(qc - NA_WIN_COLS // 2, 0, GRID_W - NA_WIN_COLS)
    col_ok = (kc >= cstart) & (kc < cstart + NA_WIN_COLS)
    toe = jnp.where(col_ok[None, None], toe, -1e30)
    per_a = jnp.stack([toe[:, HALO_ROWS - 1 - a:HALO_ROWS - 1 - a + WIN_ROWS]
                       for a in range(Q_ROWS)], axis=1)
    a = np.arange(Q_ROWS)[:, None]
    b = np.arange(WIN_ROWS)[None, :]
    row_ok = np.stack([(b >= HALO_ROWS) & (a >= 0),
                       (b - a >= 0) & (b - a < NA_WIN_ROWS),
                       (b < NA_WIN_ROWS) & (a >= 0)])
    tab = jnp.where(row_ok[None, :, :, :, None, None], per_a[:, None], -1e30)
    tab = jnp.transpose(tab, (0, 1, 3, 4, 2, 5)).reshape(n_heads, 3, NK, NQ)
    meta = jnp.broadcast_to(meta_bias[:, None, :, None], (n_heads, 3, N_META, NQ))
    return jnp.concatenate([tab, meta], axis=2).astype(F32)


def _route_kernel(o_ref, om_ref, h_ref, wout_ref, g_ref, wr_ref, br_ref, tri_ref,
                  h_out, xn_out, route_out, cnt_out, cnt_ref, *, n_tiles):
    i = pl.program_id(0)

    @pl.when(i == 0)
    def _():
        cnt_ref[...] = jnp.zeros(cnt_ref.shape, F32)

    def body(o):
        h = h_ref[...] + _dot(o, wout_ref[...])
        h_out[...] = h
        xn = _rms(h, g_ref[...])
        xn_out[...] = xn
        logits = _dot(xn.astype(BF16), wr_ref[...]) + br_ref[...]
        tm = logits.shape[0]
        lane = lax.broadcasted_iota(jnp.int32, (tm, N_EXPERTS), 1)
        m1 = jnp.max(logits, axis=-1, keepdims=True)
        i1 = jnp.min(jnp.where(logits == m1, lane, N_EXPERTS), axis=-1, keepdims=True)
        oh1 = lane == i1
        rest = jnp.where(oh1, -jnp.inf, logits)
        m2 = jnp.max(rest, axis=-1, keepdims=True)
        i2 = jnp.min(jnp.where(rest == m2, lane, N_EXPERTS), axis=-1, keepdims=True)
        oh2 = lane == i2
        t = jnp.exp(m2 - m1)
        g1 = 1.0 / (1.0 + t)
        g2 = t * g1
        oh = jnp.where(oh1 | oh2, 1.0, 0.0)
        cum = _dot(tri_ref[...], oh.astype(BF16)) + cnt_ref[0:1, 0:N_EXPERTS]
        r1 = jnp.sum(jnp.where(oh1, cum, 0.0), axis=-1, keepdims=True)
        r2 = jnp.sum(jnp.where(oh2, cum, 0.0), axis=-1, keepdims=True)
        cnt_ref[0:1, 0:N_EXPERTS] = (cnt_ref[0:1, 0:N_EXPERTS]
                                     + jnp.sum(oh, axis=0, keepdims=True))
        route = jnp.zeros((tm, N_EXPERTS), F32)
        for col, val in enumerate([i1.astype(F32), i2.astype(F32), r1, r2, g1, g2]):
            route = jnp.where(lane == col, val, route)
        route_out[...] = route
        cnt_out[...] = cnt_ref[...]

    @pl.when(i < n_tiles)
    def _():
        body(o_ref[...])

    @pl.when(i == n_tiles)
    def _():
        body(om_ref[...])


def _route_layer(o_main, o_meta, h, wout, g, wr, br, plan):
    tm = plan["tm"]
    d = h.shape[1]
    n_tiles = plan["n_main"] // tm
    tri = (jnp.arange(tm)[:, None] > jnp.arange(tm)[None, :]).astype(BF16)
    kern = functools.partial(_route_kernel, n_tiles=n_tiles)
    return pl.pallas_call(
        kern,
        grid=(n_tiles + 1,),
        in_specs=[
            pl.BlockSpec((tm, d), lambda i: (jnp.minimum(i, n_tiles - 1), 0)),
            _const_spec(o_meta.shape),
            pl.BlockSpec((tm, d), lambda i: (i, 0)),
            _const_spec(wout.shape), _const_spec(g.shape), _const_spec(wr.shape),
            _const_spec(br.shape), _const_spec(tri.shape),
        ],
        out_specs=[
            pl.BlockSpec((tm, d), lambda i: (i, 0)),
            pl.BlockSpec((tm, d), lambda i: (i, 0)),
            pl.BlockSpec((tm, N_EXPERTS), lambda i: (i, 0)),
            pl.BlockSpec((8, LANES), lambda i: (0, 0)),
        ],
        out_shape=[jax.ShapeDtypeStruct(((n_tiles + 1) * tm, d), F32),
                   jax.ShapeDtypeStruct(((n_tiles + 1) * tm, d), F32),
                   jax.ShapeDtypeStruct(((n_tiles + 1) * tm, N_EXPERTS), F32),
                   jax.ShapeDtypeStruct((8, LANES), F32)],
        scratch_shapes=[pltpu.VMEM((8, LANES), F32)],
        compiler_params=pltpu.CompilerParams(
            dimension_semantics=("arbitrary",), vmem_limit_bytes=VMEM_LIMIT),
        name="outproj_router",
    )(o_main, o_meta, h, wout, g, wr, br, tri)


def _dispatch_kernel(pos_ref, tail_ref, x_ref, xs_ref, zero_ref, sem, zsem, *, tm, te):
    i = pl.program_id(0)

    @pl.when(i == 0)
    def _():
        zero_ref[...] = jnp.zeros(zero_ref.shape, F32)

        def zero_tile(start):
            cp = pltpu.make_async_copy(
                zero_ref, xs_ref.at[pl.ds(pl.multiple_of(start, te), te)], zsem)
            cp.start()
            cp.wait()

        for e in range(N_EXPERTS):
            @pl.when(tail_ref[e] >= 0)
            def _():
                zero_tile(tail_ref[e])

        def zero_unused(u, carry):
            zero_tile(u * te)
            return carry

        lax.fori_loop(tail_ref[N_EXPERTS], xs_ref.shape[0] // te, zero_unused, 0)

    def issue(t, carry):
        for k in range(2):
            pltpu.make_async_copy(
                x_ref.at[pl.ds(t, 1)], xs_ref.at[pl.ds(pos_ref[0, k, t], 1)], sem).start()
        return carry

    lax.fori_loop(0, tm, issue, 0)
    for k in range(2):
        pltpu.make_async_copy(x_ref, xs_ref.at[pl.ds(0, tm)], sem).wait()


def _dispatch(xn, pos, tails, plan):
    tm = plan["tm"]
    te = plan["te"]
    d = xn.shape[1]
    kern = functools.partial(_dispatch_kernel, tm=tm, te=te)
    return pl.pallas_call(
        kern,
        grid=(xn.shape[0] // tm,),
        in_specs=[
            pl.BlockSpec((1, 2, tm), lambda i: (i, 0, 0), memory_space=pltpu.SMEM),
            pl.BlockSpec(memory_space=pltpu.SMEM),
            pl.BlockSpec((tm, d), lambda i: (i, 0)),
        ],
        out_specs=pl.BlockSpec(memory_space=pl.ANY),
        out_shape=jax.ShapeDtypeStruct((plan["rows_sorted"], d), F32),
        scratch_shapes=[pltpu.VMEM((te, d), F32), pltpu.SemaphoreType.DMA(()),
                        pltpu.SemaphoreType.DMA(())],
        compiler_params=pltpu.CompilerParams(
            dimension_semantics=("arbitrary",), vmem_limit_bytes=VMEM_LIMIT),
        name="moe_dispatch",
    )(pos, tails, xn)


def _expert_kernel(te_ref, nu_ref, x_ref, wg_ref, wu_ref, wd_ref, out_ref, *, sub):
    j = pl.program_id(0)
    c = pl.program_id(1)

    @pl.when(c == 0)
    def _():
        out_ref[...] = jnp.zeros(out_ref.shape, F32)

    @pl.when(j < nu_ref[0])
    def _():
        xb = x_ref[...].astype(BF16)
        fc = wg_ref.shape[1]
        y = None
        for lo in range(0, fc, sub):
            hi = min(lo + sub, fc)
            act = (_silu(_dot(xb, wg_ref[:, lo:hi])) * _dot(xb, wu_ref[:, lo:hi])).astype(BF16)
            part = _dot(act, wd_ref[lo:hi, :])
            y = part if y is None else y + part
        out_ref[...] += y


def _experts(xs, tile_expert, n_used, wg, wu, wd, plan):
    te = plan["te"]
    fc = plan["fc"]
    d = xs.shape[1]
    n_tiles = xs.shape[0] // te
    n_c = wg.shape[2] // fc

    def row(j, c, te_ref, nu_ref):
        return (jnp.minimum(j, nu_ref[0] - 1), 0)

    def chunk(j, c, nu_ref):
        return jnp.where(j < nu_ref[0], c, n_c - 1)

    def w_in(j, c, te_ref, nu_ref):
        return (te_ref[jnp.minimum(j, nu_ref[0] - 1)], 0, chunk(j, c, nu_ref))

    def w_out(j, c, te_ref, nu_ref):
        return (te_ref[jnp.minimum(j, nu_ref[0] - 1)], chunk(j, c, nu_ref), 0)

    grid_spec = pltpu.PrefetchScalarGridSpec(
        num_scalar_prefetch=2,
        grid=(n_tiles, n_c),
        in_specs=[
            pl.BlockSpec((te, d), row),
            pl.BlockSpec((None, d, fc), w_in),
            pl.BlockSpec((None, d, fc), w_in),
            pl.BlockSpec((None, fc, d), w_out),
        ],
        out_specs=pl.BlockSpec((te, d), lambda j, c, te_ref, nu_ref: (j, 0)),
    )
    return pl.pallas_call(
        functools.partial(_expert_kernel, sub=plan["fsub"]),
        grid_spec=grid_spec,
        out_shape=jax.ShapeDtypeStruct(xs.shape, F32),
        compiler_params=pltpu.CompilerParams(
            dimension_semantics=("arbitrary", "arbitrary"), vmem_limit_bytes=VMEM_LIMIT),
        name="moe_experts",
    )(tile_expert, n_used, xs, wg, wu, wd)


def _combine_kernel(pos_ref, h_ref, route_ref, g_ref, ys_ref, outp_ref, outs_ref, y_ref, sem,
                    *, tm, n_p_tiles):
    i = pl.program_id(0)

    def issue(t, carry):
        for k in range(2):
            pltpu.make_async_copy(
                ys_ref.at[pl.ds(pos_ref[0, k, t], 1)], y_ref.at[k, pl.ds(t, 1)], sem).start()
        return carry

    lax.fori_loop(0, tm, issue, 0)
    for k in range(2):
        pltpu.make_async_copy(ys_ref.at[pl.ds(0, tm)], y_ref.at[k], sem).wait()

    g1 = route_ref[:, 4:5]
    g2 = route_ref[:, 5:6]
    h = h_ref[...] + g1 * y_ref[0] + g2 * y_ref[1]
    y = _rms(h, g_ref[...])

    @pl.when(i < n_p_tiles)
    def _():
        outp_ref[...] = y

    @pl.when(i >= n_p_tiles)
    def _():
        outs_ref[...] = y


def _combine(h, route, pos, ys, g, plan):
    tm = plan["tm"]
    d = h.shape[1]
    n_tiles = plan["n_main"] // tm
    n_p_tiles = plan["n_p"] // tm
    kern = functools.partial(_combine_kernel, tm=tm, n_p_tiles=n_p_tiles)
    return pl.pallas_call(
        kern,
        grid=(n_tiles,),
        in_specs=[
            pl.BlockSpec((1, 2, tm), lambda i: (i, 0, 0), memory_space=pltpu.SMEM),
            pl.BlockSpec((tm, d), lambda i: (i, 0)),
            pl.BlockSpec((tm, N_EXPERTS), lambda i: (i, 0)),
            _const_spec(g.shape),
            pl.BlockSpec(memory_space=pl.ANY),
        ],
        out_specs=[
            pl.BlockSpec((tm, d), lambda i: (jnp.minimum(i, n_p_tiles - 1), 0)),
            pl.BlockSpec((tm, d), lambda i: (jnp.maximum(i - n_p_tiles, 0), 0)),
        ],
        out_shape=[jax.ShapeDtypeStruct((plan["n_p"], d), F32),
                   jax.ShapeDtypeStruct((plan["n_main"] - plan["n_p"], d), F32)],
        scratch_shapes=[pltpu.VMEM((2, tm, d), F32), pltpu.SemaphoreType.DMA(())],
        compiler_params=pltpu.CompilerParams(
            dimension_semantics=("arbitrary",), vmem_limit_bytes=VMEM_LIMIT),
        name="moe_combine_norm",
    )(pos, h, route, g, ys)


def _make_plan(b_p, t_p, b_s, t_s, d_ff_expert):
    tm = 512 if t_p % 512 == 0 and t_s % 512 == 0 else 128
    te = 1024 if tm == 512 else 256
    n_p = b_p * t_p
    n_main = n_p + b_s * t_s
    n_seq = b_p + b_s
    assert n_seq * N_META <= tm and t_p % tm == 0 and t_s % tm == 0
    np_rows = n_main + tm
    rows_p, rows_s = t_p // GRID_W, t_s // GRID_W
    cr = 64 if rows_p % 64 == 0 and rows_s % 64 == 0 else 8
    assert rows_p % cr == 0 and rows_s % cr == 0 and cr >= NA_WIN_ROWS
    fc = 1792 if d_ff_expert % 1792 == 0 else d_ff_expert
    fsub = 2 * MXU_DIM
    return dict(tm=tm, te=te, n_p=n_p, n_main=n_main, n_seq=n_seq, np_rows=np_rows, t_p=t_p,
                t_s=t_s, cr=cr, fc=fc, fsub=fsub, rows_sorted=2 * np_rows + N_EXPERTS * te)


def kernel(x_prompt, x_sample, meta_tokens, norm_mix, norm_ffn, norm_final, even_w_in, pool_w,
           pool_scale, conv_w, even_w_out, ffn_w_gate, ffn_w_up, ffn_w_down, na_w_qkv,
           na_rel_bias, na_meta_bias, na_w_out, moe_w_router, moe_b_router, moe_w_gate,
           moe_w_up, moe_w_down):
    b_p, t_p, d = x_prompt.shape
    b_s, t_s, _ = x_sample.shape
    plan = _make_plan(b_p, t_p, b_s, t_s, moe_w_gate.shape[-1])
    tm, te = plan["tm"], plan["te"]
    n_main, n_seq, np_rows = plan["n_main"], plan["n_seq"], plan["np_rows"]
    n_meta_rows = n_seq * N_META

    h = jnp.concatenate([
        x_prompt.reshape(-1, d), x_sample.reshape(-1, d),
        jnp.tile(meta_tokens, (n_seq, 1)),
        jnp.zeros((np_rows - n_main - n_meta_rows, d), F32)], axis=0)
    xfirst = jnp.concatenate([x_prompt[:, :EXT].reshape(-1, d), x_sample[:, :EXT].reshape(-1, d)])

    row = lambda v: v.reshape(1, -1).astype(F32)
    h = _even_layer(h, meta_tokens, xfirst, row(norm_mix[0]), even_w_in[0].astype(BF16),
                    pool_w[0].astype(BF16), row(pool_scale[0]), conv_w[0],
                    even_w_out[0].astype(BF16), plan)
    h = _ffn_layer(h, row(norm_ffn[0]), ffn_w_gate[0].astype(BF16), ffn_w_up[0].astype(BF16),
                   ffn_w_down[0].astype(BF16), plan)

    width = NA_HEADS * HEAD_DIM
    qscale = jnp.concatenate([jnp.full((width,), HEAD_DIM ** -0.5, F32), jnp.ones((width,), F32)])
    qk, vt = _qkv_layer(h, row(norm_mix[1]), (na_w_qkv[0][:, :2 * width] * qscale).astype(BF16),
                        na_w_qkv[0][:, 2 * width:].T.astype(BF16), plan)
    vt_meta = vt[:, n_main:n_main + n_meta_rows]
    n_hb = NA_HEADS // HEADS_PER_BLOCK
    o_main, o_meta = _attn_layer(
        qk, vt, vt_meta.T, vt_meta.reshape(width, n_seq, N_META).transpose(1, 0, 2),
        _attn_bias_table(na_rel_bias[0], na_meta_bias[0]),
        na_meta_bias[0].reshape(n_hb, HEADS_PER_BLOCK, 1, N_META), plan)
    o_meta = jnp.concatenate([o_meta, jnp.zeros((tm - n_meta_rows, width), BF16)])

    h, xn, route, cnt = _route_layer(
        o_main, o_meta, h, na_w_out[0].astype(BF16), row(norm_ffn[1]),
        moe_w_router[0].astype(BF16), row(moe_b_router[0]), plan)
    counts = cnt[0, :N_EXPERTS].astype(jnp.int32)
    padded = (counts + te - 1) // te * te
    ends = jnp.cumsum(padded)
    starts = ends - padded
    e12 = route[:, 0:2].astype(jnp.int32)
    start_of = jnp.sum(jnp.where(e12[..., None] == jnp.arange(N_EXPERTS), starts, 0), axis=-1)
    pos = start_of + route[:, 2:4].astype(jnp.int32)
    pos = pos.reshape(np_rows // tm, tm, 2).transpose(0, 2, 1)
    tails = jnp.concatenate([jnp.where(padded > 0, ends - te, -1), ends[-1:] // te])
    tails = tails.astype(jnp.int32)
    n_row_tiles = plan["rows_sorted"] // te
    tile_expert = jnp.minimum(
        jnp.sum(ends[None, :] <= (jnp.arange(n_row_tiles) * te)[:, None], axis=1),
        N_EXPERTS - 1).astype(jnp.int32)
    n_used = (ends[-1:] // te).astype(jnp.int32)

    xs = _dispatch(xn, pos, tails, plan)
    ys = _experts(xs, tile_expert, n_used, moe_w_gate[0].astype(BF16), moe_w_up[0].astype(BF16),
                  moe_w_down[0].astype(BF16), plan)
    y_p, y_s = _combine(h, route, pos, ys, row(norm_final), plan)
    return (y_p.reshape(b_p, t_p, d), y_s.reshape(b_s, t_s, d))
```

```python
import functools

import jax
import jax.numpy as jnp
import numpy as np
from jax import lax
from jax.experimental import pallas as pl
from jax.experimental.pallas import tpu as pltpu

F32 = jnp.float32
BF16 = jnp.bfloat16

N_META = 16
GRID_W = 64
RMS_EPS = 1e-6
POOL_WINDOWS = (2, 4, 8, 16)
POOL_GROUP = 128
POOL_WIDTH = 512
CONV_WIDTH = 512
NA_HEADS = 16
HEAD_DIM = 64
NA_WIN_ROWS = 8
NA_WIN_COLS = 16
N_EXPERTS = 8
LANES = 128
MXU_DIM = 256
HEADS_PER_BLOCK = LANES // HEAD_DIM
HALO_ROWS = NA_WIN_ROWS // 2
Q_ROWS = 4
WIN_ROWS = Q_ROWS + NA_WIN_ROWS
NQ = Q_ROWS * GRID_W
NK = WIN_ROWS * GRID_W
V_ROWS = HEAD_DIM + 16
EXT = 16
VMEM_LIMIT = 56 * 1024 * 1024


def _const_spec(shape):
    nd = len(shape)
    return pl.BlockSpec(shape, lambda *_: (0,) * nd, pipeline_mode=pl.Buffered(1))


def _rms(x, g):
    ms = jnp.mean(x * x, axis=-1, keepdims=True)
    return x * lax.rsqrt(ms + RMS_EPS) * g


def _dot(a, b):
    return jnp.dot(a, b, preferred_element_type=F32)


def _dot_nt(a, b):
    return lax.dot_general(a, b, (((1,), (1,)), ((), ())), preferred_element_type=F32)


def _silu(x):
    return x * jax.nn.sigmoid(x)


def _even_mix(xe, pos, seq_len, g, win_ref, pw_ref, ps, cw, wout_ref):
    n = xe.shape[0]
    xn = _rms(xe, g).astype(BF16)
    z = _dot(xn, win_ref[...])
    a_parts = []
    for gi, k in enumerate(POOL_WINDOWS):
        p = z[:, gi * POOL_GROUP:(gi + 1) * POOL_GROUP]
        s = p
        w = 1
        while w < k:
            s = s + pltpu.roll(s, n - w, 0)
            w *= 2
        left = k // 2
        right = k - 1 - left
        s = pltpu.roll(s, left, 0)
        cnt = jnp.minimum(pos + right + 1, seq_len) - jnp.maximum(pos - left, 0)
        inv = 1.0 / jnp.maximum(cnt, 1).astype(F32)
        pooled = (s * inv - p).astype(BF16)
        a_parts.append(_dot(pooled, pw_ref[gi]))
    a = jnp.concatenate(a_parts, axis=-1) * ps
    gb = z[:, POOL_WIDTH:POOL_WIDTH + CONV_WIDTH]
    gc = z[:, POOL_WIDTH + CONV_WIDTH:POOL_WIDTH + 2 * CONV_WIDTH]
    hc = z[:, POOL_WIDTH + 2 * CONV_WIDTH:]
    u = gc * hc
    conv = pltpu.roll(u, 1, 0) * cw[0:1] + u * cw[1:2] + pltpu.roll(u, n - 1, 0) * cw[2:3]
    c = gb * conv
    return (_dot(a.astype(BF16), wout_ref[0:POOL_WIDTH, :])
            + _dot(c.astype(BF16), wout_ref[POOL_WIDTH:, :]))


def _even_kernel(tile_ref, prev_ref, next_ref, meta_ref, first_ref, g_ref, win_ref, pw_ref,
                 ps_ref, cw_ref, wout_ref, out_ref, xe_ref, *, tm, n_tiles, tiles_p, n_p_tiles,
                 tiles_s, t_p, t_s, n_seq):
    i = pl.program_id(0)
    weights = (g_ref[...], win_ref, pw_ref, ps_ref[...], cw_ref[...], wout_ref)

    @pl.when(i < n_tiles)
    def _main():
        is_p = i < n_p_tiles
        tin = jnp.where(is_p, i % tiles_p, (i - n_p_tiles) % tiles_s)
        last = jnp.where(is_p, tiles_p, tiles_s) - 1
        seq_len = jnp.where(is_p, t_p, t_s) + N_META
        n = tm + 2 * EXT

        @pl.when(tin == 0)
        def _():
            xe_ref[0:EXT, :] = meta_ref[...]

        @pl.when(tin != 0)
        def _():
            xe_ref[0:EXT, :] = prev_ref[...]

        xe_ref[EXT:EXT + tm, :] = tile_ref[...]

        @pl.when(tin == last)
        def _():
            xe_ref[EXT + tm:n, :] = jnp.zeros((EXT, xe_ref.shape[1]), F32)

        @pl.when(tin != last)
        def _():
            xe_ref[EXT + tm:n, :] = next_ref[...]

        pos = lax.broadcasted_iota(jnp.int32, (n, 1), 0) + tin * tm
        y = _even_mix(xe_ref[0:n, :], pos, seq_len, *weights)
        out_ref[...] = tile_ref[...] + y[EXT:EXT + tm]

    @pl.when(i == n_tiles)
    def _meta():
        grp = 3 * EXT
        n = grp * n_seq
        d = xe_ref.shape[1]
        for b in range(n_seq):
            xe_ref[grp * b:grp * b + EXT, :] = jnp.zeros((EXT, d), F32)
            xe_ref[grp * b + EXT:grp * b + 2 * EXT, :] = tile_ref[EXT * b:EXT * (b + 1), :]
            xe_ref[grp * b + 2 * EXT:grp * (b + 1), :] = first_ref[EXT * b:EXT * (b + 1), :]
        pos = lax.broadcasted_iota(jnp.int32, (n, 1), 0) % grp - EXT
        y = _even_mix(xe_ref[0:n, :], pos, jnp.int32(1 << 30), *weights)
        out_ref[...] = jnp.zeros(out_ref.shape, F32)
        for b in range(n_seq):
            out_ref[EXT * b:EXT * (b + 1), :] = (
                tile_ref[EXT * b:EXT * (b + 1), :] + y[grp * b + EXT:grp * b + 2 * EXT])


def _even_layer(h, meta, xfirst, g, win, pw, ps, cw, wout, plan):
    tm = plan["tm"]
    d = h.shape[1]
    n_tiles = plan["n_main"] // tm
    blk = tm // EXT
    last_blk = h.shape[0] // EXT - 1
    kern = functools.partial(
        _even_kernel, tm=tm, n_tiles=n_tiles, tiles_p=plan["t_p"] // tm,
        n_p_tiles=plan["n_p"] // tm, tiles_s=plan["t_s"] // tm, t_p=plan["t_p"], t_s=plan["t_s"],
        n_seq=plan["n_seq"])
    xe_rows = max(tm + 2 * EXT, 3 * EXT * plan["n_seq"])
    return pl.pallas_call(
        kern,
        grid=(n_tiles + 1,),
        in_specs=[
            pl.BlockSpec((tm, d), lambda i: (i, 0)),
            pl.BlockSpec((EXT, d), lambda i: (jnp.maximum(i * blk - 1, 0), 0)),
            pl.BlockSpec((EXT, d), lambda i: (jnp.minimum((i + 1) * blk, last_blk), 0)),
            _const_spec(meta.shape), _const_spec(xfirst.shape), _const_spec(g.shape),
            _const_spec(win.shape), _const_spec(pw.shape), _const_spec(ps.shape),
            _const_spec(cw.shape), _const_spec(wout.shape),
        ],
        out_specs=pl.BlockSpec((tm, d), lambda i: (i, 0)),
        out_shape=jax.ShapeDtypeStruct(h.shape, F32),
        scratch_shapes=[pltpu.VMEM((xe_rows, d), F32)],
        compiler_params=pltpu.CompilerParams(
            dimension_semantics=("arbitrary",), vmem_limit_bytes=VMEM_LIMIT),
        name="even_mixer",
    )(h, h, h, meta, xfirst, g, win, pw, ps, cw, wout)


def _ffn_kernel(x_ref, g_ref, wg_ref, wu_ref, wd_ref, out_ref):
    x = x_ref[...]
    xn = _rms(x, g_ref[...]).astype(BF16)
    act = (_silu(_dot(xn, wg_ref[...])) * _dot(xn, wu_ref[...])).astype(BF16)
    out_ref[...] = x + _dot(act, wd_ref[...])


def _ffn_layer(h, g, wg, wu, wd, plan):
    tm = plan["tm"]
    d = h.shape[1]
    return pl.pallas_call(
        _ffn_kernel,
        grid=(h.shape[0] // tm,),
        in_specs=[pl.BlockSpec((tm, d), lambda i: (i, 0)), _const_spec(g.shape),
                  _const_spec(wg.shape), _const_spec(wu.shape), _const_spec(wd.shape)],
        out_specs=pl.BlockSpec((tm, d), lambda i: (i, 0)),
        out_shape=jax.ShapeDtypeStruct(h.shape, F32),
        compiler_params=pltpu.CompilerParams(
            dimension_semantics=("arbitrary",), vmem_limit_bytes=VMEM_LIMIT),
        name="dense_swiglu",
    )(h, g, wg, wu, wd)


def _qkv_kernel(x_ref, g_ref, wqk_ref, wvt_ref, qk_ref, vt_ref):
    xn = _rms(x_ref[...], g_ref[...]).astype(BF16)
    qk_ref[...] = _dot(xn, wqk_ref[...]).astype(BF16)
    vt_ref[...] = _dot_nt(wvt_ref[...], xn).astype(BF16)


def _qkv_layer(h, g, wqk, wvt, plan):
    tm = plan["tm"]
    d = h.shape[1]
    return pl.pallas_call(
        _qkv_kernel,
        grid=(h.shape[0] // tm,),
        in_specs=[pl.BlockSpec((tm, d), lambda i: (i, 0)), _const_spec(g.shape),
                  _const_spec(wqk.shape), _const_spec(wvt.shape)],
        out_specs=[pl.BlockSpec((tm, wqk.shape[1]), lambda i: (i, 0)),
                   pl.BlockSpec((wvt.shape[0], tm), lambda i: (0, i))],
        out_shape=[jax.ShapeDtypeStruct((h.shape[0], wqk.shape[1]), BF16),
                   jax.ShapeDtypeStruct((wvt.shape[0], h.shape[0]), BF16)],
        compiler_params=pltpu.CompilerParams(
            dimension_semantics=("arbitrary",), vmem_limit_bytes=VMEM_LIMIT),
        name="qkv_proj",
    )(h, g, wqk, wvt)


def _attn_kernel(q_ref, kc_ref, kp_ref, kn_ref, vc_ref, vp_ref, vn_ref, qm_ref, km_ref, vm_ref,
                 vmt_ref, tab_ref, mb_ref, o_ref, om_ref, kx_ref, vx_ref, s_ref, *, cr, chunks_p,
                 n_p_chunks, chunks_s, rows_p, rows_s):
    ci = pl.program_id(1)
    is_p = ci < n_p_chunks
    cin = jnp.where(is_p, ci % chunks_p, (ci - n_p_chunks) % chunks_s)
    rows = jnp.where(is_p, rows_p, rows_s)
    halo = HALO_ROWS * GRID_W
    ct = cr * GRID_W

    kx_ref[0:halo, :] = kp_ref[...]
    kx_ref[halo:halo + ct, :] = kc_ref[...]
    kx_ref[halo + ct:, :] = kn_ref[...]
    ones_rows = V_ROWS - HEAD_DIM
    vmt_aug = []
    for hh in range(HEADS_PER_BLOCK):
        hs = slice(hh * HEAD_DIM, (hh + 1) * HEAD_DIM)
        vx_ref[hh, 0:HEAD_DIM, 0:halo] = vp_ref[hs, :]
        vx_ref[hh, 0:HEAD_DIM, halo:halo + ct] = vc_ref[hs, :]
        vx_ref[hh, 0:HEAD_DIM, halo + ct:] = vn_ref[hs, :]
        vx_ref[hh, HEAD_DIM:, :] = jnp.ones((ones_rows, ct + 2 * halo), BF16)
        vmt_aug.append(jnp.concatenate(
            [vmt_ref[hs, :], jnp.ones((ones_rows, N_META), BF16)], axis=0))

    lane = lax.broadcasted_iota(jnp.int32, (1, LANES), 1)
    head_masks = [(lane // HEAD_DIM) == hh for hh in range(HEADS_PER_BLOCK)]
    km = km_ref[...]
    r0c = cin * cr

    n_blocks = cr // Q_ROWS

    def scores(bi, slot):
        r0 = r0c + bi * Q_ROWS
        edge = jnp.where(r0 == 0, 0, jnp.where(r0 == rows - Q_ROWS, 2, 1))
        off = pl.multiple_of(bi * NQ, NQ)
        q = q_ref[pl.ds(off, NQ), :]
        qq = jnp.concatenate(
            [jnp.where(head_masks[hh], q, jnp.zeros_like(q)) for hh in range(HEADS_PER_BLOCK)],
            axis=0)
        s_ref[slot, 0:NK, :] = _dot_nt(kx_ref[pl.ds(off, NK), :], qq) + tab_ref[edge, 0:NK, :]
        s_ref[slot, NK:NK + N_META, :] = _dot_nt(km, qq) + tab_ref[edge, NK:NK + N_META, :]

    def softmax_pv(bi, slot):
        off = pl.multiple_of(bi * NQ, NQ)
        s = s_ref[slot]
        pb = jnp.exp2(s - jnp.max(s, axis=0, keepdims=True)).astype(BF16)
        outs = []
        for hh in range(HEADS_PER_BLOCK):
            qs = slice(hh * NQ, (hh + 1) * NQ)
            oa = (_dot(vx_ref[hh, :, pl.ds(off, NK)], pb[0:NK, qs])
                  + _dot(vmt_aug[hh], pb[NK:NK + N_META, qs]))
            outs.append(oa[0:HEAD_DIM] / oa[HEAD_DIM:HEAD_DIM + 1])
        o_ref[pl.ds(off, NQ), :] = jnp.concatenate(outs, axis=0).T.astype(BF16)

    scores(0, 0)
    scores(1, 1)

    def quad_body(b4, carry):
        b = 4 * b4
        for k in range(4):
            scores(jnp.minimum(b + k + 2, n_blocks - 1), (k + 2) % 4)
            softmax_pv(b + k, k)
        return carry

    lax.fori_loop(0, n_blocks // 4, quad_body, 0)

    qm = qm_ref[...]
    vm = vm_ref[...]
    out = None
    for hh in range(HEADS_PER_BLOCK):
        qh = jnp.where(head_masks[hh], qm, jnp.zeros_like(qm))
        sm = _dot_nt(qh, km) + mb_ref[hh]
        pm = jnp.exp2(sm - jnp.max(sm, axis=-1, keepdims=True))
        o = _dot(pm.astype(BF16), vm) / jnp.sum(pm, axis=-1, keepdims=True)
        out = o if out is None else jnp.where(head_masks[hh], o, out)
    om_ref[...] = out.astype(BF16)


def _attn_layer(qk, vt, vm, vmt, bias_tab, meta_bias, plan):
    cr = plan["cr"]
    ct = cr * GRID_W
    halo = HALO_ROWS * GRID_W
    n_main = plan["n_main"]
    n_chunks = n_main // ct
    n_hb = NA_HEADS // HEADS_PER_BLOCK
    hb = ct // halo
    last_halo = n_main // halo - 1
    meta_blk0 = n_main // N_META
    n_p_chunks = plan["n_p"] // ct
    chunks_p = plan["t_p"] // ct
    chunks_s = plan["t_s"] // ct
    b_p = plan["n_p"] // plan["t_p"]

    def seq_of(ci):
        return jnp.where(ci < n_p_chunks, ci // chunks_p, b_p + (ci - n_p_chunks) // chunks_s)

    def cur(off):
        return pl.BlockSpec((ct, LANES), lambda h, c: (c, off + h))

    def prev(off):
        return pl.BlockSpec((halo, LANES), lambda h, c: (jnp.maximum(c * hb - 1, 0), off + h))

    def nxt(off):
        return pl.BlockSpec((halo, LANES),
                            lambda h, c: (jnp.minimum((c + 1) * hb, last_halo), off + h))

    def meta(off):
        return pl.BlockSpec((N_META, LANES), lambda h, c: (meta_blk0 + seq_of(c), off + h))

    def vt_cur():
        return pl.BlockSpec((LANES, ct), lambda h, c: (h, c))

    def vt_prev():
        return pl.BlockSpec((LANES, halo), lambda h, c: (h, jnp.maximum(c * hb - 1, 0)))

    def vt_next():
        return pl.BlockSpec((LANES, halo), lambda h, c: (h, jnp.minimum((c + 1) * hb, last_halo)))

    kern = functools.partial(
        _attn_kernel, cr=cr, chunks_p=chunks_p, n_p_chunks=n_p_chunks, chunks_s=chunks_s,
        rows_p=plan["t_p"] // GRID_W, rows_s=plan["t_s"] // GRID_W)
    width = NA_HEADS * HEAD_DIM
    return pl.pallas_call(
        kern,
        grid=(n_hb, n_chunks),
        in_specs=[
            cur(0), cur(n_hb), prev(n_hb), nxt(n_hb), vt_cur(), vt_prev(), vt_next(),
            meta(0), meta(n_hb),
            pl.BlockSpec((N_META, LANES), lambda h, c: (seq_of(c), h)),
            pl.BlockSpec((None, LANES, N_META), lambda h, c: (seq_of(c), h, 0)),
            pl.BlockSpec((None,) + bias_tab.shape[1:], lambda h, c: (h, 0, 0, 0)),
            pl.BlockSpec((None, HEADS_PER_BLOCK, 1, N_META), lambda h, c: (h, 0, 0, 0)),
        ],
        out_specs=[
            pl.BlockSpec((ct, LANES), lambda h, c: (c, h)),
            pl.BlockSpec((N_META, LANES), lambda h, c: (seq_of(c), h)),
        ],
        out_shape=[jax.ShapeDtypeStruct((n_main, width), BF16),
                   jax.ShapeDtypeStruct((plan["n_seq"] * N_META, width), BF16)],
        scratch_shapes=[pltpu.VMEM((ct + 2 * halo, LANES), BF16),
                        pltpu.VMEM((HEADS_PER_BLOCK, V_ROWS, ct + 2 * halo), BF16),
                        pltpu.VMEM((4, NK + N_META, HEADS_PER_BLOCK * NQ), F32)],
        compiler_params=pltpu.CompilerParams(
            dimension_semantics=("arbitrary", "arbitrary"), vmem_limit_bytes=VMEM_LIMIT),
        name="neighbourhood_attention",
    )(qk, qk, qk, qk, vt, vt, vt, qk, qk, vm, vmt, bias_tab, meta_bias)


def _attn_bias_table(rel_bias, meta_bias):
    n_heads, n_row_off, n_col_off = rel_bias.shape
    half = n_col_off // 2
    ring = jnp.concatenate([
        rel_bias[:, :, half::-1],
        jnp.zeros((n_heads, n_row_off, 2 * GRID_W - n_col_off), F32),
        rel_bias[:, :, :half:-1]], axis=-1)
    toe = jnp.tile(ring, (1, 1, GRID_W))[:, :, :GRID_W * (2 * GRID_W - 1)]
    toe = toe.reshape(n_heads, n_row_off, GRID_W, 2 * GRID_W - 1)[..., :GRID_W]
    qc = np.arange(GRID_W)[None, :]
    kc = np.arange(GRID_W)[:, None]
    cstart = np.clip(qc - NA_WIN_COLS // 2, 0, GRID_W - NA_WIN_COLS)
    col_ok = (kc >= cstart) & (kc < cstart + NA_WIN_COLS)
    toe = jnp.where(col_ok[None, None], toe, -1e30)
    per_a = jnp.stack([toe[:, HALO_ROWS - 1 - a:HALO_ROWS - 1 - a + WIN_ROWS]
                       for a in range(Q_ROWS)], axis=1)
    a = np.arange(Q_ROWS)[:, None]
    b = np.arange(WIN_ROWS)[None, :]
    row_ok = np.stack([(b >= HALO_ROWS) & (a >= 0),
                       (b - a >= 0) & (b - a < NA_WIN_ROWS),
                       (b < NA_WIN_ROWS) & (a >= 0)])
    tab = jnp.where(row_ok[None, :, :, :, None, None], per_a[:, None], -1e30)
    tab = jnp.transpose(tab, (0, 1, 3, 4, 2, 5)).reshape(n_heads, 3, NK, NQ)
    meta = jnp.broadcast_to(meta_bias[:, None, :, None], (n_heads, 3, N_META, NQ))
    tab = jnp.concatenate([tab, meta], axis=2).astype(F32)
    tab = tab.reshape(n_heads // HEADS_PER_BLOCK, HEADS_PER_BLOCK, 3, NK + N_META, NQ)
    return jnp.transpose(tab, (0, 2, 3, 1, 4)).reshape(
        n_heads // HEADS_PER_BLOCK, 3, NK + N_META, HEADS_PER_BLOCK * NQ)


def _route_kernel(o_ref, om_ref, h_ref, wout_ref, g_ref, wr_ref, br_ref, tri_ref,
                  h_out, xn_out, route_out, cnt_out, cnt_ref, *, n_tiles):
    i = pl.program_id(0)

    @pl.when(i == 0)
    def _():
        cnt_ref[...] = jnp.zeros(cnt_ref.shape, F32)

    def body(o):
        h = h_ref[...] + _dot(o, wout_ref[...])
        h_out[...] = h
        xn = _rms(h, g_ref[...])
        xn_out[...] = xn
        logits = _dot(xn.astype(BF16), wr_ref[...]) + br_ref[...]
        tm = logits.shape[0]
        lane = lax.broadcasted_iota(jnp.int32, (tm, N_EXPERTS), 1)
        m1 = jnp.max(logits, axis=-1, keepdims=True)
        i1 = jnp.min(jnp.where(logits == m1, lane, N_EXPERTS), axis=-1, keepdims=True)
        oh1 = lane == i1
        rest = jnp.where(oh1, -jnp.inf, logits)
        m2 = jnp.max(rest, axis=-1, keepdims=True)
        i2 = jnp.min(jnp.where(rest == m2, lane, N_EXPERTS), axis=-1, keepdims=True)
        oh2 = lane == i2
        t = jnp.exp(m2 - m1)
        g1 = 1.0 / (1.0 + t)
        g2 = t * g1
        oh = jnp.where(oh1 | oh2, 1.0, 0.0)
        cum = _dot(tri_ref[...], oh.astype(BF16)) + cnt_ref[0:1, 0:N_EXPERTS]
        r1 = jnp.sum(jnp.where(oh1, cum, 0.0), axis=-1, keepdims=True)
        r2 = jnp.sum(jnp.where(oh2, cum, 0.0), axis=-1, keepdims=True)
        cnt_ref[0:1, 0:N_EXPERTS] = (cnt_ref[0:1, 0:N_EXPERTS]
                                     + jnp.sum(oh, axis=0, keepdims=True))
        route = jnp.zeros((tm, N_EXPERTS), F32)
        for col, val in enumerate([i1.astype(F32), i2.astype(F32), r1, r2, g1, g2]):
            route = jnp.where(lane == col, val, route)
        route_out[...] = route
        cnt_out[...] = cnt_ref[...]

    @pl.when(i < n_tiles)
    def _():
        body(o_ref[...])

    @pl.when(i == n_tiles)
    def _():
        body(om_ref[...])


def _route_layer(o_main, o_meta, h, wout, g, wr, br, plan):
    tm = plan["tm"]
    d = h.shape[1]
    n_tiles = plan["n_main"] // tm
    tri = (jnp.arange(tm)[:, None] > jnp.arange(tm)[None, :]).astype(BF16)
    kern = functools.partial(_route_kernel, n_tiles=n_tiles)
    return pl.pallas_call(
        kern,
        grid=(n_tiles + 1,),
        in_specs=[
            pl.BlockSpec((tm, d), lambda i: (jnp.minimum(i, n_tiles - 1), 0)),
            _const_spec(o_meta.shape),
            pl.BlockSpec((tm, d), lambda i: (i, 0)),
            _const_spec(wout.shape), _const_spec(g.shape), _const_spec(wr.shape),
            _const_spec(br.shape), _const_spec(tri.shape),
        ],
        out_specs=[
            pl.BlockSpec((tm, d), lambda i: (i, 0)),
            pl.BlockSpec((tm, d), lambda i: (i, 0)),
            pl.BlockSpec((tm, N_EXPERTS), lambda i: (i, 0)),
            pl.BlockSpec((8, LANES), lambda i: (0, 0)),
        ],
        out_shape=[jax.ShapeDtypeStruct(((n_tiles + 1) * tm, d), F32),
                   jax.ShapeDtypeStruct(((n_tiles + 1) * tm, d), F32),
                   jax.ShapeDtypeStruct(((n_tiles + 1) * tm, N_EXPERTS), F32),
                   jax.ShapeDtypeStruct((8, LANES), F32)],
        scratch_shapes=[pltpu.VMEM((8, LANES), F32)],
        compiler_params=pltpu.CompilerParams(
            dimension_semantics=("arbitrary",), vmem_limit_bytes=VMEM_LIMIT),
        name="outproj_router",
    )(o_main, o_meta, h, wout, g, wr, br, tri)


def _dispatch_kernel(pos_ref, tail_ref, x_ref, xs_ref, zero_ref, sem, zsem, *, tm, te):
    i = pl.program_id(0)

    @pl.when(i == 0)
    def _():
        zero_ref[...] = jnp.zeros(zero_ref.shape, F32)

        def zero_tile(start):
            cp = pltpu.make_async_copy(
                zero_ref, xs_ref.at[pl.ds(pl.multiple_of(start, te), te)], zsem)
            cp.start()
            cp.wait()

        for e in range(N_EXPERTS):
            @pl.when(tail_ref[e] >= 0)
            def _():
                zero_tile(tail_ref[e])

        def zero_unused(u, carry):
            zero_tile(u * te)
            return carry

        lax.fori_loop(tail_ref[N_EXPERTS], xs_ref.shape[0] // te, zero_unused, 0)

    def issue(g, carry):
        for j in range(LANES):
            for k in range(2):
                pltpu.make_async_copy(
                    x_ref.at[g, pl.ds(j, 1)], xs_ref.at[pl.ds(pos_ref[0, g, k, j], 1)], sem
                ).start()
        return carry

    lax.fori_loop(0, tm // LANES, issue, 0)
    for k in range(2):
        for g in range(tm // LANES):
            pltpu.make_async_copy(x_ref.at[g], xs_ref.at[pl.ds(0, LANES)], sem).wait()


def _dispatch(xn, pos, tails, plan):
    tm = plan["tm"]
    te = plan["te"]
    d = xn.shape[1]
    kern = functools.partial(_dispatch_kernel, tm=tm, te=te)
    groups = tm // LANES
    return pl.pallas_call(
        kern,
        grid=(xn.shape[0] // tm,),
        in_specs=[
            pl.BlockSpec((1, groups, 2, LANES), lambda i: (i, 0, 0, 0), memory_space=pltpu.SMEM),
            pl.BlockSpec(memory_space=pltpu.SMEM),
            pl.BlockSpec((groups, LANES, d), lambda i: (i, 0, 0)),
        ],
        out_specs=pl.BlockSpec(memory_space=pl.ANY),
        out_shape=jax.ShapeDtypeStruct((plan["rows_sorted"], d), F32),
        scratch_shapes=[pltpu.VMEM((te, d), F32), pltpu.SemaphoreType.DMA(()),
                        pltpu.SemaphoreType.DMA(())],
        compiler_params=pltpu.CompilerParams(
            dimension_semantics=("arbitrary",), vmem_limit_bytes=VMEM_LIMIT),
        name="moe_dispatch",
    )(pos, tails, xn.reshape(-1, LANES, d))


def _expert_kernel(te_ref, nu_ref, x_ref, wg_ref, wu_ref, wd_ref, out_ref, *, sub):
    j = pl.program_id(0)
    c = pl.program_id(1)

    @pl.when(c == 0)
    def _():
        out_ref[...] = jnp.zeros(out_ref.shape, F32)

    @pl.when(j < nu_ref[0])
    def _():
        xb = x_ref[...].astype(BF16)
        fc = wg_ref.shape[1]
        y = None
        for lo in range(0, fc, sub):
            hi = min(lo + sub, fc)
            act = (_silu(_dot(xb, wg_ref[:, lo:hi])) * _dot(xb, wu_ref[:, lo:hi])).astype(BF16)
            part = _dot(act, wd_ref[lo:hi, :])
            y = part if y is None else y + part
        out_ref[...] += y


def _experts(xs, tile_expert, n_used, wg, wu, wd, plan):
    te = plan["te"]
    fc = plan["fc"]
    d = xs.shape[1]
    n_tiles = xs.shape[0] // te
    n_c = wg.shape[2] // fc

    def row(j, c, te_ref, nu_ref):
        return (jnp.minimum(j, nu_ref[0] - 1), 0)

    def chunk(j, c, nu_ref):
        return jnp.where(j < nu_ref[0], c, n_c - 1)

    def w_in(j, c, te_ref, nu_ref):
        return (te_ref[jnp.minimum(j, nu_ref[0] - 1)], 0, chunk(j, c, nu_ref))

    def w_out(j, c, te_ref, nu_ref):
        return (te_ref[jnp.minimum(j, nu_ref[0] - 1)], chunk(j, c, nu_ref), 0)

    grid_spec = pltpu.PrefetchScalarGridSpec(
        num_scalar_prefetch=2,
        grid=(n_tiles, n_c),
        in_specs=[
            pl.BlockSpec((te, d), row),
            pl.BlockSpec((None, d, fc), w_in),
            pl.BlockSpec((None, d, fc), w_in),
            pl.BlockSpec((None, fc, d), w_out),
        ],
        out_specs=pl.BlockSpec((te, d), lambda j, c, te_ref, nu_ref: (j, 0)),
    )
    return pl.pallas_call(
        functools.partial(_expert_kernel, sub=plan["fsub"]),
        grid_spec=grid_spec,
        out_shape=jax.ShapeDtypeStruct(xs.shape, F32),
        compiler_params=pltpu.CompilerParams(
            dimension_semantics=("arbitrary", "arbitrary"), vmem_limit_bytes=VMEM_LIMIT),
        name="moe_experts",
    )(tile_expert, n_used, xs, wg, wu, wd)


def _combine_kernel(posc_ref, posn_ref, h_ref, route_ref, g_ref, ys_ref, outp_ref, outs_ref,
                    y_ref, sems, *, tm, n_tiles, n_p_tiles):
    i = pl.program_id(0)
    groups = tm // LANES

    def gather(pos_ref, slot):
        def issue(g, carry):
            for j in range(LANES):
                for k in range(2):
                    pltpu.make_async_copy(
                        ys_ref.at[pl.ds(pos_ref[0, g, k, j], 1)],
                        y_ref.at[slot, k, g, pl.ds(j, 1)], sems.at[slot]).start()
            return carry

        lax.fori_loop(0, groups, issue, 0)

    @pl.when(i == 0)
    def _():
        gather(posc_ref, 0)

    @pl.when(i + 1 < n_tiles)
    def _():
        gather(posn_ref, (i + 1) % 2)

    slot = i % 2
    for k in range(2):
        for g in range(groups):
            pltpu.make_async_copy(
                ys_ref.at[pl.ds(0, LANES)], y_ref.at[slot, k, g], sems.at[slot]).wait()

    for g in range(groups):
        rows = slice(g * LANES, (g + 1) * LANES)
        h = (h_ref[rows, :] + route_ref[rows, 4:5] * y_ref[slot, 0, g]
             + route_ref[rows, 5:6] * y_ref[slot, 1, g])
        y = _rms(h, g_ref[...])

        @pl.when(i < n_p_tiles)
        def _():
            outp_ref[rows, :] = y

        @pl.when(i >= n_p_tiles)
        def _():
            outs_ref[rows, :] = y


def _combine(h, route, pos, ys, g, plan):
    tm = plan["tm"]
    d = h.shape[1]
    n_tiles = plan["n_main"] // tm
    n_p_tiles = plan["n_p"] // tm
    groups = tm // LANES
    kern = functools.partial(_combine_kernel, tm=tm, n_tiles=n_tiles, n_p_tiles=n_p_tiles)
    return pl.pallas_call(
        kern,
        grid=(n_tiles,),
        in_specs=[
            pl.BlockSpec((1, groups, 2, LANES), lambda i: (i, 0, 0, 0), memory_space=pltpu.SMEM),
            pl.BlockSpec((1, groups, 2, LANES),
                         lambda i: (jnp.minimum(i + 1, n_tiles - 1), 0, 0, 0),
                         memory_space=pltpu.SMEM),
            pl.BlockSpec((tm, d), lambda i: (i, 0)),
            pl.BlockSpec((tm, N_EXPERTS), lambda i: (i, 0)),
            _const_spec(g.shape),
            pl.BlockSpec(memory_space=pl.ANY),
        ],
        out_specs=[
            pl.BlockSpec((tm, d), lambda i: (jnp.minimum(i, n_p_tiles - 1), 0)),
            pl.BlockSpec((tm, d), lambda i: (jnp.maximum(i - n_p_tiles, 0), 0)),
        ],
        out_shape=[jax.ShapeDtypeStruct((plan["n_p"], d), F32),
                   jax.ShapeDtypeStruct((plan["n_main"] - plan["n_p"], d), F32)],
        scratch_shapes=[pltpu.VMEM((2, 2, groups, LANES, d), F32),
                        pltpu.SemaphoreType.DMA((2,))],
        compiler_params=pltpu.CompilerParams(
            dimension_semantics=("arbitrary",), vmem_limit_bytes=VMEM_LIMIT),
        name="moe_combine_norm",
    )(pos, pos, h, route, g, ys)


def _make_plan(b_p, t_p, b_s, t_s, d_ff_expert):
    tm = 512 if t_p % 512 == 0 and t_s % 512 == 0 else 128
    te = 1024 if tm == 512 else 256
    n_p = b_p * t_p
    n_main = n_p + b_s * t_s
    n_seq = b_p + b_s
    assert n_seq * N_META <= tm and t_p % tm == 0 and t_s % tm == 0
    np_rows = n_main + tm
    rows_p, rows_s = t_p // GRID_W, t_s // GRID_W
    cr = 64 if rows_p % 64 == 0 and rows_s % 64 == 0 else 4 * Q_ROWS
    assert rows_p % cr == 0 and rows_s % cr == 0 and cr % (4 * Q_ROWS) == 0
    fc = 1792 if d_ff_expert % 1792 == 0 else d_ff_expert
    fsub = 2 * MXU_DIM
    return dict(tm=tm, te=te, n_p=n_p, n_main=n_main, n_seq=n_seq, np_rows=np_rows, t_p=t_p,
                t_s=t_s, cr=cr, fc=fc, fsub=fsub, rows_sorted=2 * np_rows + N_EXPERTS * te)


def kernel(x_prompt, x_sample, meta_tokens, norm_mix, norm_ffn, norm_final, even_w_in, pool_w,
           pool_scale, conv_w, even_w_out, ffn_w_gate, ffn_w_up, ffn_w_down, na_w_qkv,
           na_rel_bias, na_meta_bias, na_w_out, moe_w_router, moe_b_router, moe_w_gate,
           moe_w_up, moe_w_down):
    b_p, t_p, d = x_prompt.shape
    b_s, t_s, _ = x_sample.shape
    plan = _make_plan(b_p, t_p, b_s, t_s, moe_w_gate.shape[-1])
    tm, te = plan["tm"], plan["te"]
    n_main, n_seq, np_rows = plan["n_main"], plan["n_seq"], plan["np_rows"]
    n_meta_rows = n_seq * N_META

    h = jnp.concatenate([
        x_prompt.reshape(-1, d), x_sample.reshape(-1, d),
        jnp.tile(meta_tokens, (n_seq, 1)),
        jnp.zeros((np_rows - n_main - n_meta_rows, d), F32)], axis=0)
    xfirst = jnp.concatenate([x_prompt[:, :EXT].reshape(-1, d), x_sample[:, :EXT].reshape(-1, d)])

    row = lambda v: v.reshape(1, -1).astype(F32)
    h = _even_layer(h, meta_tokens, xfirst, row(norm_mix[0]), even_w_in[0].astype(BF16),
                    pool_w[0].astype(BF16), row(pool_scale[0]), conv_w[0],
                    even_w_out[0].astype(BF16), plan)
    h = _ffn_layer(h, row(norm_ffn[0]), ffn_w_gate[0].astype(BF16), ffn_w_up[0].astype(BF16),
                   ffn_w_down[0].astype(BF16), plan)

    width = NA_HEADS * HEAD_DIM
    log2e = float(np.log2(np.e))
    qscale = jnp.concatenate([jnp.full((width,), HEAD_DIM ** -0.5 * log2e, F32),
                              jnp.ones((width,), F32)])
    qk, vt = _qkv_layer(h, row(norm_mix[1]), (na_w_qkv[0][:, :2 * width] * qscale).astype(BF16),
                        na_w_qkv[0][:, 2 * width:].T.astype(BF16), plan)
    vt_meta = vt[:, n_main:n_main + n_meta_rows]
    n_hb = NA_HEADS // HEADS_PER_BLOCK
    o_main, o_meta = _attn_layer(
        qk, vt, vt_meta.T, vt_meta.reshape(width, n_seq, N_META).transpose(1, 0, 2),
        _attn_bias_table(na_rel_bias[0] * log2e, na_meta_bias[0] * log2e),
        (na_meta_bias[0] * log2e).reshape(n_hb, HEADS_PER_BLOCK, 1, N_META), plan)
    o_meta = jnp.concatenate([o_meta, jnp.zeros((tm - n_meta_rows, width), BF16)])

    h, xn, route, cnt = _route_layer(
        o_main, o_meta, h, na_w_out[0].astype(BF16), row(norm_ffn[1]),
        moe_w_router[0].astype(BF16), row(moe_b_router[0]), plan)
    counts = cnt[0, :N_EXPERTS].astype(jnp.int32)
    padded = (counts + te - 1) // te * te
    ends = jnp.cumsum(padded)
    starts = ends - padded
    e12 = route[:, 0:2].astype(jnp.int32)
    start_of = jnp.sum(jnp.where(e12[..., None] == jnp.arange(N_EXPERTS), starts, 0), axis=-1)
    pos = start_of + route[:, 2:4].astype(jnp.int32)
    pos = pos.reshape(np_rows // tm, tm // LANES, LANES, 2).transpose(0, 1, 3, 2)
    tails = jnp.concatenate([jnp.where(padded > 0, ends - te, -1), ends[-1:] // te])
    tails = tails.astype(jnp.int32)
    n_row_tiles = plan["rows_sorted"] // te
    tile_expert = jnp.minimum(
        jnp.sum(ends[None, :] <= (jnp.arange(n_row_tiles) * te)[:, None], axis=1),
        N_EXPERTS - 1).astype(jnp.int32)
    n_used = (ends[-1:] // te).astype(jnp.int32)

    xs = _dispatch(xn, pos, tails, plan)
    ys = _experts(xs, tile_expert, n_used, moe_w_gate[0].astype(BF16), moe_w_up[0].astype(BF16),
                  moe_w_down[0].astype(BF16), plan)
    y_p, y_s = _combine(h, route, pos, ys, row(norm_final), plan)
    return (y_p.reshape(b_p, t_p, d), y_s.reshape(b_s, t_s, d))
```

```python
import functools

import jax
import jax.numpy as jnp
import numpy as np
from jax import lax
from jax.experimental import pallas as pl
from jax.experimental.pallas import tpu as pltpu

F32 = jnp.float32
BF16 = jnp.bfloat16

N_META = 16
GRID_W = 64
RMS_EPS = 1e-6
POOL_WINDOWS = (2, 4, 8, 16)
POOL_GROUP = 128
POOL_WIDTH = 512
CONV_WIDTH = 512
NA_HEADS = 16
HEAD_DIM = 64
NA_WIN_ROWS = 8
NA_WIN_COLS = 16
N_EXPERTS = 8
LANES = 128
MXU_DIM = 256
HEADS_PER_BLOCK = LANES // HEAD_DIM
HALO_ROWS = NA_WIN_ROWS // 2
Q_ROWS = 4
WIN_ROWS = Q_ROWS + NA_WIN_ROWS
NQ = Q_ROWS * GRID_W
NK = WIN_ROWS * GRID_W
V_ROWS = HEAD_DIM + 16
EXT = 16
VMEM_LIMIT = 56 * 1024 * 1024


def _const_spec(shape):
    nd = len(shape)
    return pl.BlockSpec(shape, lambda *_: (0,) * nd, pipeline_mode=pl.Buffered(1))


def _rms(x, g):
    ms = jnp.mean(x * x, axis=-1, keepdims=True)
    return x * lax.rsqrt(ms + RMS_EPS) * g


def _dot(a, b):
    return jnp.dot(a, b, preferred_element_type=F32)


def _dot_nt(a, b):
    return lax.dot_general(a, b, (((1,), (1,)), ((), ())), preferred_element_type=F32)


def _silu(x):
    return x * jax.nn.sigmoid(x)


def _even_mix(xe, pos, seq_len, g, win_ref, pw_ref, ps, cw, wout_ref):
    n = xe.shape[0]
    xn = _rms(xe, g).astype(BF16)
    z = _dot(xn, win_ref[...])
    a_parts = []
    for gi, k in enumerate(POOL_WINDOWS):
        p = z[:, gi * POOL_GROUP:(gi + 1) * POOL_GROUP]
        s = p
        w = 1
        while w < k:
            s = s + pltpu.roll(s, n - w, 0)
            w *= 2
        left = k // 2
        right = k - 1 - left
        s = pltpu.roll(s, left, 0)
        cnt = jnp.minimum(pos + right + 1, seq_len) - jnp.maximum(pos - left, 0)
        inv = 1.0 / jnp.maximum(cnt, 1).astype(F32)
        pooled = (s * inv - p).astype(BF16)
        a_parts.append(_dot(pooled, pw_ref[gi]))
    a = jnp.concatenate(a_parts, axis=-1) * ps
    gb = z[:, POOL_WIDTH:POOL_WIDTH + CONV_WIDTH]
    gc = z[:, POOL_WIDTH + CONV_WIDTH:POOL_WIDTH + 2 * CONV_WIDTH]
    hc = z[:, POOL_WIDTH + 2 * CONV_WIDTH:]
    u = gc * hc
    conv = pltpu.roll(u, 1, 0) * cw[0:1] + u * cw[1:2] + pltpu.roll(u, n - 1, 0) * cw[2:3]
    c = gb * conv
    return (_dot(a.astype(BF16), wout_ref[0:POOL_WIDTH, :])
            + _dot(c.astype(BF16), wout_ref[POOL_WIDTH:, :]))


def _even_kernel(xp_ref, xpp_ref, xpn_ref, xs_ref, xsp_ref, xsn_ref, meta_ref, first_ref, g_ref,
                 win_ref, pw_ref, ps_ref, cw_ref, wout_ref, out_ref, xe_ref, *, tm, n_tiles,
                 tiles_p, n_p_tiles, tiles_s, t_p, t_s, n_seq):
    i = pl.program_id(0)
    weights = (g_ref[...], win_ref, pw_ref, ps_ref[...], cw_ref[...], wout_ref)
    d = xe_ref.shape[1]

    def token_tile(tile_ref, prev_ref, next_ref, tin, last, t_seq):
        n = tm + 2 * EXT

        @pl.when(tin == 0)
        def _():
            xe_ref[0:EXT, :] = meta_ref[...]

        @pl.when(tin != 0)
        def _():
            xe_ref[0:EXT, :] = prev_ref[...]

        xe_ref[EXT:EXT + tm, :] = tile_ref[...]

        @pl.when(tin == last)
        def _():
            xe_ref[EXT + tm:n, :] = jnp.zeros((EXT, d), F32)

        @pl.when(tin != last)
        def _():
            xe_ref[EXT + tm:n, :] = next_ref[...]

        pos = lax.broadcasted_iota(jnp.int32, (n, 1), 0) + tin * tm
        y = _even_mix(xe_ref[0:n, :], pos, t_seq + N_META, *weights)
        out_ref[...] = xe_ref[EXT:EXT + tm, :] + y[EXT:EXT + tm]

    @pl.when(i < n_p_tiles)
    def _prompt():
        token_tile(xp_ref, xpp_ref, xpn_ref, i % tiles_p, tiles_p - 1, t_p)

    @pl.when((i >= n_p_tiles) & (i < n_tiles))
    def _sample():
        token_tile(xs_ref, xsp_ref, xsn_ref, (i - n_p_tiles) % tiles_s, tiles_s - 1, t_s)

    @pl.when(i == n_tiles)
    def _meta():
        grp = 3 * EXT
        n = grp * n_seq
        for b in range(n_seq):
            xe_ref[grp * b:grp * b + EXT, :] = jnp.zeros((EXT, d), F32)
            xe_ref[grp * b + EXT:grp * b + 2 * EXT, :] = meta_ref[...]
            xe_ref[grp * b + 2 * EXT:grp * (b + 1), :] = first_ref[EXT * b:EXT * (b + 1), :]
        pos = lax.broadcasted_iota(jnp.int32, (n, 1), 0) % grp - EXT
        y = _even_mix(xe_ref[0:n, :], pos, jnp.int32(1 << 30), *weights)
        out_ref[...] = jnp.zeros(out_ref.shape, F32)
        for b in range(n_seq):
            out_ref[EXT * b:EXT * (b + 1), :] = meta_ref[...] + y[grp * b + EXT:grp * b + 2 * EXT]


def _even_layer(xp, xs, meta, xfirst, g, win, pw, ps, cw, wout, plan):
    tm = plan["tm"]
    d = xp.shape[1]
    n_tiles = plan["n_main"] // tm
    n_p_tiles = plan["n_p"] // tm
    n_s_tiles = n_tiles - n_p_tiles
    blk = tm // EXT
    kern = functools.partial(
        _even_kernel, tm=tm, n_tiles=n_tiles, tiles_p=plan["t_p"] // tm, n_p_tiles=n_p_tiles,
        tiles_s=plan["t_s"] // tm, t_p=plan["t_p"], t_s=plan["t_s"], n_seq=plan["n_seq"])
    xe_rows = max(tm + 2 * EXT, 3 * EXT * plan["n_seq"])

    def specs(first_tile, count):
        last = count * blk - 1
        tile = lambda i: jnp.clip(i - first_tile, 0, count - 1)
        return [
            pl.BlockSpec((tm, d), lambda i: (tile(i), 0)),
            pl.BlockSpec((EXT, d), lambda i: (jnp.clip(tile(i) * blk - 1, 0, last), 0)),
            pl.BlockSpec((EXT, d), lambda i: (jnp.clip((tile(i) + 1) * blk, 0, last), 0)),
        ]

    return pl.pallas_call(
        kern,
        grid=(n_tiles + 1,),
        in_specs=specs(0, n_p_tiles) + specs(n_p_tiles, n_s_tiles) + [
            _const_spec(meta.shape), _const_spec(xfirst.shape), _const_spec(g.shape),
            _const_spec(win.shape), _const_spec(pw.shape), _const_spec(ps.shape),
            _const_spec(cw.shape), _const_spec(wout.shape),
        ],
        out_specs=pl.BlockSpec((tm, d), lambda i: (i, 0)),
        out_shape=jax.ShapeDtypeStruct((plan["np_rows"], d), F32),
        scratch_shapes=[pltpu.VMEM((xe_rows, d), F32)],
        compiler_params=pltpu.CompilerParams(
            dimension_semantics=("arbitrary",), vmem_limit_bytes=VMEM_LIMIT),
        name="even_mixer",
    )(xp, xp, xp, xs, xs, xs, meta, xfirst, g, win, pw, ps, cw, wout)


def _ffn_kernel(x_ref, g_ref, wg_ref, wu_ref, wd_ref, out_ref):
    x = x_ref[...]
    xn = _rms(x, g_ref[...]).astype(BF16)
    act = (_silu(_dot(xn, wg_ref[...])) * _dot(xn, wu_ref[...])).astype(BF16)
    out_ref[...] = x + _dot(act, wd_ref[...])


def _ffn_layer(h, g, wg, wu, wd, plan):
    tm = plan["tm"]
    d = h.shape[1]
    return pl.pallas_call(
        _ffn_kernel,
        grid=(h.shape[0] // tm,),
        in_specs=[pl.BlockSpec((tm, d), lambda i: (i, 0)), _const_spec(g.shape),
                  _const_spec(wg.shape), _const_spec(wu.shape), _const_spec(wd.shape)],
        out_specs=pl.BlockSpec((tm, d), lambda i: (i, 0)),
        out_shape=jax.ShapeDtypeStruct(h.shape, F32),
        compiler_params=pltpu.CompilerParams(
            dimension_semantics=("arbitrary",), vmem_limit_bytes=VMEM_LIMIT),
        name="dense_swiglu",
    )(h, g, wg, wu, wd)


def _qkv_kernel(x_ref, g_ref, wqk_ref, wvt_ref, qk_ref, vt_ref):
    xn = _rms(x_ref[...], g_ref[...]).astype(BF16)
    qk_ref[...] = _dot(xn, wqk_ref[...]).astype(BF16)
    vt_ref[...] = _dot_nt(wvt_ref[...], xn).astype(BF16)


def _qkv_layer(h, g, wqk, wvt, plan):
    tm = plan["tm"]
    d = h.shape[1]
    return pl.pallas_call(
        _qkv_kernel,
        grid=(h.shape[0] // tm,),
        in_specs=[pl.BlockSpec((tm, d), lambda i: (i, 0)), _const_spec(g.shape),
                  _const_spec(wqk.shape), _const_spec(wvt.shape)],
        out_specs=[pl.BlockSpec((tm, wqk.shape[1]), lambda i: (i, 0)),
                   pl.BlockSpec((wvt.shape[0], tm), lambda i: (0, i))],
        out_shape=[jax.ShapeDtypeStruct((h.shape[0], wqk.shape[1]), BF16),
                   jax.ShapeDtypeStruct((wvt.shape[0], h.shape[0]), BF16)],
        compiler_params=pltpu.CompilerParams(
            dimension_semantics=("arbitrary",), vmem_limit_bytes=VMEM_LIMIT),
        name="qkv_proj",
    )(h, g, wqk, wvt)


def _attn_kernel(q_ref, kc_ref, kp_ref, kn_ref, vc_ref, vp_ref, vn_ref, qm_ref, km_ref, vm_ref,
                 vmt_ref, tab_ref, mb_ref, o_ref, om_ref, kx_ref, vx_ref, s_ref, *, cr, chunks_p,
                 n_p_chunks, chunks_s, rows_p, rows_s):
    ci = pl.program_id(1)
    is_p = ci < n_p_chunks
    cin = jnp.where(is_p, ci % chunks_p, (ci - n_p_chunks) % chunks_s)
    rows = jnp.where(is_p, rows_p, rows_s)
    halo = HALO_ROWS * GRID_W
    ct = cr * GRID_W

    kx_ref[0:halo, :] = kp_ref[...]
    kx_ref[halo:halo + ct, :] = kc_ref[...]
    kx_ref[halo + ct:, :] = kn_ref[...]
    ones_rows = V_ROWS - HEAD_DIM
    vmt_aug = []
    for hh in range(HEADS_PER_BLOCK):
        hs = slice(hh * HEAD_DIM, (hh + 1) * HEAD_DIM)
        vx_ref[hh, 0:HEAD_DIM, 0:halo] = vp_ref[hs, :]
        vx_ref[hh, 0:HEAD_DIM, halo:halo + ct] = vc_ref[hs, :]
        vx_ref[hh, 0:HEAD_DIM, halo + ct:] = vn_ref[hs, :]
        vx_ref[hh, HEAD_DIM:, :] = jnp.ones((ones_rows, ct + 2 * halo), BF16)
        vmt_aug.append(jnp.concatenate(
            [vmt_ref[hs, :], jnp.ones((ones_rows, N_META), BF16)], axis=0))

    lane = lax.broadcasted_iota(jnp.int32, (1, LANES), 1)
    head_masks = [(lane // HEAD_DIM) == hh for hh in range(HEADS_PER_BLOCK)]
    km = km_ref[...]
    r0c = cin * cr

    n_blocks = cr // Q_ROWS

    def scores(bi, slot):
        r0 = r0c + bi * Q_ROWS
        edge = jnp.where(r0 == 0, 0, jnp.where(r0 == rows - Q_ROWS, 2, 1))
        off = pl.multiple_of(bi * NQ, NQ)
        q = q_ref[pl.ds(off, NQ), :]
        qq = jnp.concatenate(
            [jnp.where(head_masks[hh], q, jnp.zeros_like(q)) for hh in range(HEADS_PER_BLOCK)],
            axis=0)
        s_ref[slot, 0:NK, :] = _dot_nt(kx_ref[pl.ds(off, NK), :], qq) + tab_ref[edge, 0:NK, :]
        s_ref[slot, NK:NK + N_META, :] = _dot_nt(km, qq) + tab_ref[edge, NK:NK + N_META, :]

    def softmax_pv(bi, slot):
        off = pl.multiple_of(bi * NQ, NQ)
        s = s_ref[slot]
        pb = jnp.exp2(s - jnp.max(s, axis=0, keepdims=True)).astype(BF16)
        outs = []
        for hh in range(HEADS_PER_BLOCK):
            qs = slice(hh * NQ, (hh + 1) * NQ)
            oa = (_dot(vx_ref[hh, :, pl.ds(off, NK)], pb[0:NK, qs])
                  + _dot(vmt_aug[hh], pb[NK:NK + N_META, qs]))
            outs.append(oa[0:HEAD_DIM] / oa[HEAD_DIM:HEAD_DIM + 1])
        o_ref[pl.ds(off, NQ), :] = jnp.concatenate(outs, axis=0).T.astype(BF16)

    scores(0, 0)
    scores(1, 1)

    def quad_body(b4, carry):
        b = 4 * b4
        for k in range(4):
            scores(jnp.minimum(b + k + 2, n_blocks - 1), (k + 2) % 4)
            softmax_pv(b + k, k)
        return carry

    lax.fori_loop(0, n_blocks // 4, quad_body, 0)

    qm = qm_ref[...]
    vm = vm_ref[...]
    out = None
    for hh in range(HEADS_PER_BLOCK):
        qh = jnp.where(head_masks[hh], qm, jnp.zeros_like(qm))
        sm = _dot_nt(qh, km) + mb_ref[hh]
        pm = jnp.exp2(sm - jnp.max(sm, axis=-1, keepdims=True))
        o = _dot(pm.astype(BF16), vm) / jnp.sum(pm, axis=-1, keepdims=True)
        out = o if out is None else jnp.where(head_masks[hh], o, out)
    om_ref[...] = out.astype(BF16)


def _attn_layer(qk, vt, vm, vmt, bias_tab, meta_bias, plan):
    cr = plan["cr"]
    ct = cr * GRID_W
    halo = HALO_ROWS * GRID_W
    n_main = plan["n_main"]
    n_chunks = n_main // ct
    n_hb = NA_HEADS // HEADS_PER_BLOCK
    hb = ct // halo
    last_halo = n_main // halo - 1
    meta_blk0 = n_main // N_META
    n_p_chunks = plan["n_p"] // ct
    chunks_p = plan["t_p"] // ct
    chunks_s = plan["t_s"] // ct
    b_p = plan["n_p"] // plan["t_p"]

    def seq_of(ci):
        return jnp.where(ci < n_p_chunks, ci // chunks_p, b_p + (ci - n_p_chunks) // chunks_s)

    def cur(off):
        return pl.BlockSpec((ct, LANES), lambda h, c: (c, off + h))

    def prev(off):
        return pl.BlockSpec((halo, LANES), lambda h, c: (jnp.maximum(c * hb - 1, 0), off + h))

    def nxt(off):
        return pl.BlockSpec((halo, LANES),
                            lambda h, c: (jnp.minimum((c + 1) * hb, last_halo), off + h))

    def meta(off):
        return pl.BlockSpec((N_META, LANES), lambda h, c: (meta_blk0 + seq_of(c), off + h))

    def vt_cur():
        return pl.BlockSpec((LANES, ct), lambda h, c: (h, c))

    def vt_prev():
        return pl.BlockSpec((LANES, halo), lambda h, c: (h, jnp.maximum(c * hb - 1, 0)))

    def vt_next():
        return pl.BlockSpec((LANES, halo), lambda h, c: (h, jnp.minimum((c + 1) * hb, last_halo)))

    kern = functools.partial(
        _attn_kernel, cr=cr, chunks_p=chunks_p, n_p_chunks=n_p_chunks, chunks_s=chunks_s,
        rows_p=plan["t_p"] // GRID_W, rows_s=plan["t_s"] // GRID_W)
    width = NA_HEADS * HEAD_DIM
    return pl.pallas_call(
        kern,
        grid=(n_hb, n_chunks),
        in_specs=[
            cur(0), cur(n_hb), prev(n_hb), nxt(n_hb), vt_cur(), vt_prev(), vt_next(),
            meta(0), meta(n_hb),
            pl.BlockSpec((N_META, LANES), lambda h, c: (seq_of(c), h)),
            pl.BlockSpec((None, LANES, N_META), lambda h, c: (seq_of(c), h, 0)),
            pl.BlockSpec((None,) + bias_tab.shape[1:], lambda h, c: (h, 0, 0, 0)),
            pl.BlockSpec((None, HEADS_PER_BLOCK, 1, N_META), lambda h, c: (h, 0, 0, 0)),
        ],
        out_specs=[
            pl.BlockSpec((ct, LANES), lambda h, c: (c, h)),
            pl.BlockSpec((N_META, LANES), lambda h, c: (seq_of(c), h)),
        ],
        out_shape=[jax.ShapeDtypeStruct((n_main, width), BF16),
                   jax.ShapeDtypeStruct((plan["n_seq"] * N_META, width), BF16)],
        scratch_shapes=[pltpu.VMEM((ct + 2 * halo, LANES), BF16),
                        pltpu.VMEM((HEADS_PER_BLOCK, V_ROWS, ct + 2 * halo), BF16),
                        pltpu.VMEM((4, NK + N_META, HEADS_PER_BLOCK * NQ), F32)],
        compiler_params=pltpu.CompilerParams(
            dimension_semantics=("arbitrary", "arbitrary"), vmem_limit_bytes=VMEM_LIMIT),
        name="neighbourhood_attention",
    )(qk, qk, qk, qk, vt, vt, vt, qk, qk, vm, vmt, bias_tab, meta_bias)


def _attn_bias_table(rel_bias, meta_bias):
    n_heads, n_row_off, n_col_off = rel_bias.shape
    half = n_col_off // 2
    ring = jnp.concatenate([
        rel_bias[:, :, half::-1],
        jnp.zeros((n_heads, n_row_off, 2 * GRID_W - n_col_off), F32),
        rel_bias[:, :, :half:-1]], axis=-1)
    toe = jnp.tile(ring, (1, 1, GRID_W))[:, :, :GRID_W * (2 * GRID_W - 1)]
    toe = toe.reshape(n_heads, n_row_off, GRID_W, 2 * GRID_W - 1)[..., :GRID_W]
    qc = np.arange(GRID_W)[None, :]
    kc = np.arange(GRID_W)[:, None]
    cstart = np.clip(qc - NA_WIN_COLS // 2, 0, GRID_W - NA_WIN_COLS)
    col_ok = (kc >= cstart) & (kc < cstart + NA_WIN_COLS)
    toe = jnp.where(col_ok[None, None], toe, -1e30)
    n_hb = n_heads // HEADS_PER_BLOCK
    toe = toe.reshape(n_hb, HEADS_PER_BLOCK, n_row_off, GRID_W, GRID_W)
    per_a = jnp.stack([toe[:, :, HALO_ROWS - 1 - a:HALO_ROWS - 1 - a + WIN_ROWS]
                       for a in range(Q_ROWS)], axis=2)
    per_a = jnp.transpose(per_a, (0, 3, 4, 1, 2, 5))
    a = np.arange(Q_ROWS)[None, :]
    b = np.arange(WIN_ROWS)[:, None]
    row_ok = np.stack([(b >= HALO_ROWS) & (a >= 0),
                       (b - a >= 0) & (b - a < NA_WIN_ROWS),
                       (b < NA_WIN_ROWS) & (a >= 0)])
    tab = jnp.where(row_ok[None, :, :, None, None, :, None], per_a[:, None], -1e30)
    tab = tab.reshape(n_hb, 3, NK, HEADS_PER_BLOCK * NQ)
    meta = jnp.transpose(meta_bias.reshape(n_hb, HEADS_PER_BLOCK, N_META), (0, 2, 1))
    meta = jnp.broadcast_to(meta[:, None, :, :, None], (n_hb, 3, N_META, HEADS_PER_BLOCK, NQ))
    meta = meta.reshape(n_hb, 3, N_META, HEADS_PER_BLOCK * NQ)
    return jnp.concatenate([tab, meta], axis=2).astype(F32)


def _route_kernel(o_ref, om_ref, h_ref, wout_ref, g_ref, wr_ref, br_ref, tri_ref,
                  h_out, xn_out, route_out, cnt_out, cnt_ref, *, n_tiles):
    i = pl.program_id(0)

    @pl.when(i == 0)
    def _():
        cnt_ref[...] = jnp.zeros(cnt_ref.shape, F32)

    def body(o):
        h = h_ref[...] + _dot(o, wout_ref[...])
        h_out[...] = h
        xn = _rms(h, g_ref[...])
        xn_out[...] = xn
        logits = _dot(xn.astype(BF16), wr_ref[...]) + br_ref[...]
        tm = logits.shape[0]
        lane = lax.broadcasted_iota(jnp.int32, (tm, N_EXPERTS), 1)
        m1 = jnp.max(logits, axis=-1, keepdims=True)
        i1 = jnp.min(jnp.where(logits == m1, lane, N_EXPERTS), axis=-1, keepdims=True)
        oh1 = lane == i1
        rest = jnp.where(oh1, -jnp.inf, logits)
        m2 = jnp.max(rest, axis=-1, keepdims=True)
        i2 = jnp.min(jnp.where(rest == m2, lane, N_EXPERTS), axis=-1, keepdims=True)
        oh2 = lane == i2
        t = jnp.exp(m2 - m1)
        g1 = 1.0 / (1.0 + t)
        g2 = t * g1
        oh = jnp.where(oh1 | oh2, 1.0, 0.0)
        cum = _dot(tri_ref[...], oh.astype(BF16)) + cnt_ref[0:1, 0:N_EXPERTS]
        r1 = jnp.sum(jnp.where(oh1, cum, 0.0), axis=-1, keepdims=True)
        r2 = jnp.sum(jnp.where(oh2, cum, 0.0), axis=-1, keepdims=True)
        cnt_ref[0:1, 0:N_EXPERTS] = (cnt_ref[0:1, 0:N_EXPERTS]
                                     + jnp.sum(oh, axis=0, keepdims=True))
        route = jnp.zeros((tm, N_EXPERTS), F32)
        for col, val in enumerate([i1.astype(F32), i2.astype(F32), r1, r2, g1, g2]):
            route = jnp.where(lane == col, val, route)
        route_out[...] = route
        cnt_out[...] = cnt_ref[...]

    @pl.when(i < n_tiles)
    def _():
        body(o_ref[...])

    @pl.when(i == n_tiles)
    def _():
        body(om_ref[...])


def _route_layer(o_main, o_meta, h, wout, g, wr, br, plan):
    tm = plan["tm"]
    d = h.shape[1]
    n_tiles = plan["n_main"] // tm
    tri = (jnp.arange(tm)[:, None] > jnp.arange(tm)[None, :]).astype(BF16)
    kern = functools.partial(_route_kernel, n_tiles=n_tiles)
    return pl.pallas_call(
        kern,
        grid=(n_tiles + 1,),
        in_specs=[
            pl.BlockSpec((tm, d), lambda i: (jnp.minimum(i, n_tiles - 1), 0)),
            _const_spec(o_meta.shape),
            pl.BlockSpec((tm, d), lambda i: (i, 0)),
            _const_spec(wout.shape), _const_spec(g.shape), _const_spec(wr.shape),
            _const_spec(br.shape), _const_spec(tri.shape),
        ],
        out_specs=[
            pl.BlockSpec((tm, d), lambda i: (i, 0)),
            pl.BlockSpec((tm, d), lambda i: (i, 0)),
            pl.BlockSpec((tm, N_EXPERTS), lambda i: (i, 0)),
            pl.BlockSpec((8, LANES), lambda i: (0, 0)),
        ],
        out_shape=[jax.ShapeDtypeStruct(((n_tiles + 1) * tm, d), F32),
                   jax.ShapeDtypeStruct(((n_tiles + 1) * tm, d), F32),
                   jax.ShapeDtypeStruct(((n_tiles + 1) * tm, N_EXPERTS), F32),
                   jax.ShapeDtypeStruct((8, LANES), F32)],
        scratch_shapes=[pltpu.VMEM((8, LANES), F32)],
        compiler_params=pltpu.CompilerParams(
            dimension_semantics=("arbitrary",), vmem_limit_bytes=VMEM_LIMIT),
        name="outproj_router",
    )(o_main, o_meta, h, wout, g, wr, br, tri)


def _dispatch_kernel(pos_ref, tail_ref, x_ref, xs_ref, zero_ref, sem, zsem, *, tm, te):
    i = pl.program_id(0)

    @pl.when(i == 0)
    def _():
        zero_ref[...] = jnp.zeros(zero_ref.shape, F32)

        def zero_tile(start):
            cp = pltpu.make_async_copy(
                zero_ref, xs_ref.at[pl.ds(pl.multiple_of(start, te), te)], zsem)
            cp.start()
            cp.wait()

        for e in range(N_EXPERTS):
            @pl.when(tail_ref[e] >= 0)
            def _():
                zero_tile(tail_ref[e])

        def zero_unused(u, carry):
            zero_tile(u * te)
            return carry

        lax.fori_loop(tail_ref[N_EXPERTS], xs_ref.shape[0] // te, zero_unused, 0)

    def issue(g, carry):
        for j in range(LANES):
            for k in range(2):
                pltpu.make_async_copy(
                    x_ref.at[g, pl.ds(j, 1)], xs_ref.at[pl.ds(pos_ref[0, g, k, j], 1)], sem
                ).start(priority=k)
        return carry

    lax.fori_loop(0, tm // LANES, issue, 0)
    for k in range(2):
        for g in range(tm // LANES):
            pltpu.make_async_copy(x_ref.at[g], xs_ref.at[pl.ds(0, LANES)], sem).wait()


def _dispatch(xn, pos, tails, plan):
    tm = plan["tm"]
    te = plan["te"]
    d = xn.shape[1]
    kern = functools.partial(_dispatch_kernel, tm=tm, te=te)
    groups = tm // LANES
    return pl.pallas_call(
        kern,
        grid=(xn.shape[0] // tm,),
        in_specs=[
            pl.BlockSpec((1, groups, 2, LANES), lambda i: (i, 0, 0, 0), memory_space=pltpu.SMEM),
            pl.BlockSpec(memory_space=pltpu.SMEM),
            pl.BlockSpec((groups, LANES, d), lambda i: (i, 0, 0)),
        ],
        out_specs=pl.BlockSpec(memory_space=pl.ANY),
        out_shape=jax.ShapeDtypeStruct((plan["rows_sorted"], d), F32),
        scratch_shapes=[pltpu.VMEM((te, d), F32), pltpu.SemaphoreType.DMA(()),
                        pltpu.SemaphoreType.DMA(())],
        compiler_params=pltpu.CompilerParams(
            dimension_semantics=("arbitrary",), vmem_limit_bytes=VMEM_LIMIT),
        name="moe_dispatch",
    )(pos, tails, xn.reshape(-1, LANES, d))


def _expert_kernel(te_ref, nu_ref, x_ref, wg_ref, wu_ref, wd_ref, out_ref, *, sub):
    j = pl.program_id(0)
    c = pl.program_id(1)

    @pl.when(c == 0)
    def _():
        out_ref[...] = jnp.zeros(out_ref.shape, F32)

    @pl.when(j < nu_ref[0])
    def _():
        xb = x_ref[...].astype(BF16)
        fc = wg_ref.shape[1]
        y = None
        for lo in range(0, fc, sub):
            hi = min(lo + sub, fc)
            act = (_silu(_dot(xb, wg_ref[:, lo:hi])) * _dot(xb, wu_ref[:, lo:hi])).astype(BF16)
            part = _dot(act, wd_ref[lo:hi, :])
            y = part if y is None else y + part
        out_ref[...] += y


def _experts(xs, tile_expert, n_used, wg, wu, wd, plan):
    te = plan["te"]
    fc = plan["fc"]
    d = xs.shape[1]
    n_tiles = xs.shape[0] // te
    n_c = wg.shape[2] // fc

    def row(j, c, te_ref, nu_ref):
        return (jnp.minimum(j, nu_ref[0] - 1), 0)

    def chunk(j, c, nu_ref):
        return jnp.where(j < nu_ref[0], c, n_c - 1)

    def w_in(j, c, te_ref, nu_ref):
        return (te_ref[jnp.minimum(j, nu_ref[0] - 1)], 0, chunk(j, c, nu_ref))

    def w_out(j, c, te_ref, nu_ref):
        return (te_ref[jnp.minimum(j, nu_ref[0] - 1)], chunk(j, c, nu_ref), 0)

    grid_spec = pltpu.PrefetchScalarGridSpec(
        num_scalar_prefetch=2,
        grid=(n_tiles, n_c),
        in_specs=[
            pl.BlockSpec((te, d), row),
            pl.BlockSpec((None, d, fc), w_in),
            pl.BlockSpec((None, d, fc), w_in),
            pl.BlockSpec((None, fc, d), w_out),
        ],
        out_specs=pl.BlockSpec((te, d), lambda j, c, te_ref, nu_ref: (j, 0)),
    )
    return pl.pallas_call(
        functools.partial(_expert_kernel, sub=plan["fsub"]),
        grid_spec=grid_spec,
        out_shape=jax.ShapeDtypeStruct(xs.shape, F32),
        compiler_params=pltpu.CompilerParams(
            dimension_semantics=("arbitrary", "arbitrary"), vmem_limit_bytes=VMEM_LIMIT),
        name="moe_experts",
    )(tile_expert, n_used, xs, wg, wu, wd)


def _combine_kernel(posc_ref, posn_ref, h_ref, route_ref, g_ref, ys_ref, outp_ref, outs_ref,
                    y_ref, sems, *, tm, n_tiles, n_p_tiles):
    i = pl.program_id(0)
    groups = tm // LANES

    def gather(pos_ref, slot):
        def issue(g, carry):
            for j in range(LANES):
                for k in range(2):
                    pltpu.make_async_copy(
                        ys_ref.at[pl.ds(pos_ref[0, g, k, j], 1)],
                        y_ref.at[slot, k, g, pl.ds(j, 1)], sems.at[slot]).start(priority=k)
            return carry

        lax.fori_loop(0, groups, issue, 0)

    @pl.when(i == 0)
    def _():
        gather(posc_ref, 0)

    @pl.when(i + 1 < n_tiles)
    def _():
        gather(posn_ref, (i + 1) % 2)

    slot = i % 2
    for k in range(2):
        for g in range(groups):
            pltpu.make_async_copy(
                ys_ref.at[pl.ds(0, LANES)], y_ref.at[slot, k, g], sems.at[slot]).wait()

    for g in range(groups):
        rows = slice(g * LANES, (g + 1) * LANES)
        h = (h_ref[rows, :] + route_ref[rows, 4:5] * y_ref[slot, 0, g]
             + route_ref[rows, 5:6] * y_ref[slot, 1, g])
        y = _rms(h, g_ref[...])

        @pl.when(i < n_p_tiles)
        def _():
            outp_ref[rows, :] = y

        @pl.when(i >= n_p_tiles)
        def _():
            outs_ref[rows, :] = y


def _combine(h, route, pos, ys, g, plan):
    tm = plan["tm"]
    d = h.shape[1]
    n_tiles = plan["n_main"] // tm
    n_p_tiles = plan["n_p"] // tm
    groups = tm // LANES
    kern = functools.partial(_combine_kernel, tm=tm, n_tiles=n_tiles, n_p_tiles=n_p_tiles)
    return pl.pallas_call(
        kern,
        grid=(n_tiles,),
        in_specs=[
            pl.BlockSpec((1, groups, 2, LANES), lambda i: (i, 0, 0, 0), memory_space=pltpu.SMEM),
            pl.BlockSpec((1, groups, 2, LANES),
                         lambda i: (jnp.minimum(i + 1, n_tiles - 1), 0, 0, 0),
                         memory_space=pltpu.SMEM),
            pl.BlockSpec((tm, d), lambda i: (i, 0)),
            pl.BlockSpec((tm, N_EXPERTS), lambda i: (i, 0)),
            _const_spec(g.shape),
            pl.BlockSpec(memory_space=pl.ANY),
        ],
        out_specs=[
            pl.BlockSpec((tm, d), lambda i: (jnp.minimum(i, n_p_tiles - 1), 0)),
            pl.BlockSpec((tm, d), lambda i: (jnp.maximum(i - n_p_tiles, 0), 0)),
        ],
        out_shape=[jax.ShapeDtypeStruct((plan["n_p"], d), F32),
                   jax.ShapeDtypeStruct((plan["n_main"] - plan["n_p"], d), F32)],
        scratch_shapes=[pltpu.VMEM((2, 2, groups, LANES, d), F32),
                        pltpu.SemaphoreType.DMA((2,))],
        compiler_params=pltpu.CompilerParams(
            dimension_semantics=("arbitrary",), vmem_limit_bytes=VMEM_LIMIT),
        name="moe_combine_norm",
    )(pos, pos, h, route, g, ys)


def _make_plan(b_p, t_p, b_s, t_s, d_ff_expert):
    tm = 512 if t_p % 512 == 0 and t_s % 512 == 0 else 128
    te = 1024 if tm == 512 else 256
    n_p = b_p * t_p
    n_main = n_p + b_s * t_s
    n_seq = b_p + b_s
    assert n_seq * N_META <= tm and t_p % tm == 0 and t_s % tm == 0
    np_rows = n_main + tm
    rows_p, rows_s = t_p // GRID_W, t_s // GRID_W
    cr = 64 if rows_p % 64 == 0 and rows_s % 64 == 0 else 4 * Q_ROWS
    assert rows_p % cr == 0 and rows_s % cr == 0 and cr % (4 * Q_ROWS) == 0
    fc = 1792 if d_ff_expert % 1792 == 0 else d_ff_expert
    fsub = 2 * MXU_DIM
    return dict(tm=tm, te=te, n_p=n_p, n_main=n_main, n_seq=n_seq, np_rows=np_rows, t_p=t_p,
                t_s=t_s, cr=cr, fc=fc, fsub=fsub, rows_sorted=2 * np_rows + N_EXPERTS * te)


def kernel(x_prompt, x_sample, meta_tokens, norm_mix, norm_ffn, norm_final, even_w_in, pool_w,
           pool_scale, conv_w, even_w_out, ffn_w_gate, ffn_w_up, ffn_w_down, na_w_qkv,
           na_rel_bias, na_meta_bias, na_w_out, moe_w_router, moe_b_router, moe_w_gate,
           moe_w_up, moe_w_down):
    b_p, t_p, d = x_prompt.shape
    b_s, t_s, _ = x_sample.shape
    plan = _make_plan(b_p, t_p, b_s, t_s, moe_w_gate.shape[-1])
    tm, te = plan["tm"], plan["te"]
    n_main, n_seq, np_rows = plan["n_main"], plan["n_seq"], plan["np_rows"]
    n_meta_rows = n_seq * N_META

    xfirst = jnp.concatenate([x_prompt[:, :EXT].reshape(-1, d), x_sample[:, :EXT].reshape(-1, d)])

    row = lambda v: v.reshape(1, -1).astype(F32)
    h = _even_layer(x_prompt.reshape(-1, d), x_sample.reshape(-1, d), meta_tokens, xfirst,
                    row(norm_mix[0]), even_w_in[0].astype(BF16), pool_w[0].astype(BF16),
                    row(pool_scale[0]), conv_w[0], even_w_out[0].astype(BF16), plan)
    h = _ffn_layer(h, row(norm_ffn[0]), ffn_w_gate[0].astype(BF16), ffn_w_up[0].astype(BF16),
                   ffn_w_down[0].astype(BF16), plan)

    width = NA_HEADS * HEAD_DIM
    log2e = float(np.log2(np.e))
    qscale = jnp.concatenate([jnp.full((width,), HEAD_DIM ** -0.5 * log2e, F32),
                              jnp.ones((width,), F32)])
    qk, vt = _qkv_layer(h, row(norm_mix[1]), (na_w_qkv[0][:, :2 * width] * qscale).astype(BF16),
                        na_w_qkv[0][:, 2 * width:].T.astype(BF16), plan)
    vt_meta = vt[:, n_main:n_main + n_meta_rows]
    n_hb = NA_HEADS // HEADS_PER_BLOCK
    o_main, o_meta = _attn_layer(
        qk, vt, vt_meta.T, vt_meta.reshape(width, n_seq, N_META).transpose(1, 0, 2),
        _attn_bias_table(na_rel_bias[0] * log2e, na_meta_bias[0] * log2e),
        (na_meta_bias[0] * log2e).reshape(n_hb, HEADS_PER_BLOCK, 1, N_META), plan)
    o_meta = jnp.concatenate([o_meta, jnp.zeros((tm - n_meta_rows, width), BF16)])

    h, xn, route, cnt = _route_layer(
        o_main, o_meta, h, na_w_out[0].astype(BF16), row(norm_ffn[1]),
        moe_w_router[0].astype(BF16), row(moe_b_router[0]), plan)
    counts = cnt[0, :N_EXPERTS].astype(jnp.int32)
    padded = (counts + te - 1) // te * te
    ends = jnp.cumsum(padded)
    starts = ends - padded
    e12 = route[:, 0:2].astype(jnp.int32)
    start_of = jnp.sum(jnp.where(e12[..., None] == jnp.arange(N_EXPERTS), starts, 0), axis=-1)
    pos = start_of + route[:, 2:4].astype(jnp.int32)
    pos = pos.reshape(np_rows // tm, tm // LANES, LANES, 2).transpose(0, 1, 3, 2)
    tails = jnp.concatenate([jnp.where(padded > 0, ends - te, -1), ends[-1:] // te])
    tails = tails.astype(jnp.int32)
    n_row_tiles = plan["rows_sorted"] // te
    tile_expert = jnp.minimum(
        jnp.sum(ends[None, :] <= (jnp.arange(n_row_tiles) * te)[:, None], axis=1),
        N_EXPERTS - 1).astype(jnp.int32)
    n_used = (ends[-1:] // te).astype(jnp.int32)

    xs = _dispatch(xn, pos, tails, plan)
    ys = _experts(xs, tile_expert, n_used, moe_w_gate[0].astype(BF16), moe_w_up[0].astype(BF16),
                  moe_w_down[0].astype(BF16), plan)
    y_p, y_s = _combine(h, route, pos, ys, row(norm_final), plan)
    return (y_p.reshape(b_p, t_p, d), y_s.reshape(b_s, t_s, d))
```

```python
import functools

import jax
import jax.numpy as jnp
import numpy as np
from jax import lax
from jax.experimental import pallas as pl
from jax.experimental.pallas import tpu as pltpu

F32 = jnp.float32
BF16 = jnp.bfloat16

N_META = 16
GRID_W = 64
RMS_EPS = 1e-6
POOL_WINDOWS = (2, 4, 8, 16)
POOL_GROUP = 128
POOL_WIDTH = 512
CONV_WIDTH = 512
NA_HEADS = 16
HEAD_DIM = 64
NA_WIN_ROWS = 8
NA_WIN_COLS = 16
N_EXPERTS = 8
LANES = 128
MXU_DIM = 256
SEG_ALIGN = 8
HEADS_PER_BLOCK = LANES // HEAD_DIM
HALO_ROWS = NA_WIN_ROWS // 2
Q_ROWS = 4
WIN_ROWS = Q_ROWS + NA_WIN_ROWS
NQ = Q_ROWS * GRID_W
NK = WIN_ROWS * GRID_W
V_ROWS = HEAD_DIM + 16
EXT = 16
VMEM_LIMIT = 56 * 1024 * 1024


def _const_spec(shape):
    nd = len(shape)
    return pl.BlockSpec(shape, lambda *_: (0,) * nd, pipeline_mode=pl.Buffered(1))


def _rms(x, g):
    ms = jnp.mean(x * x, axis=-1, keepdims=True)
    return x * lax.rsqrt(ms + RMS_EPS) * g


def _dot(a, b):
    return jnp.dot(a, b, preferred_element_type=F32)


def _dot_nt(a, b):
    return lax.dot_general(a, b, (((1,), (1,)), ((), ())), preferred_element_type=F32)


def _silu(x):
    return x * jax.nn.sigmoid(x)


def _even_mix(xe, pos, seq_len, g, win_ref, pw_ref, ps, cw, wout_ref):
    n = xe.shape[0]
    xn = _rms(xe, g).astype(BF16)
    z = _dot(xn, win_ref[...])
    a_parts = []
    for gi, k in enumerate(POOL_WINDOWS):
        p = z[:, gi * POOL_GROUP:(gi + 1) * POOL_GROUP]
        s = p
        w = 1
        while w < k:
            s = s + pltpu.roll(s, n - w, 0)
            w *= 2
        left = k // 2
        right = k - 1 - left
        s = pltpu.roll(s, left, 0)
        cnt = jnp.minimum(pos + right + 1, seq_len) - jnp.maximum(pos - left, 0)
        inv = 1.0 / jnp.maximum(cnt, 1).astype(F32)
        pooled = (s * inv - p).astype(BF16)
        a_parts.append(_dot(pooled, pw_ref[gi]))
    a = jnp.concatenate(a_parts, axis=-1) * ps
    gb = z[:, POOL_WIDTH:POOL_WIDTH + CONV_WIDTH]
    gc = z[:, POOL_WIDTH + CONV_WIDTH:POOL_WIDTH + 2 * CONV_WIDTH]
    hc = z[:, POOL_WIDTH + 2 * CONV_WIDTH:]
    u = gc * hc
    conv = pltpu.roll(u, 1, 0) * cw[0:1] + u * cw[1:2] + pltpu.roll(u, n - 1, 0) * cw[2:3]
    c = gb * conv
    return (_dot(a.astype(BF16), wout_ref[0:POOL_WIDTH, :])
            + _dot(c.astype(BF16), wout_ref[POOL_WIDTH:, :]))


def _even_kernel(xp_ref, xpp_ref, xpn_ref, xs_ref, xsp_ref, xsn_ref, meta_ref, first_ref, g_ref,
                 win_ref, pw_ref, ps_ref, cw_ref, wout_ref, out_ref, xe_ref, *, tm, n_tiles,
                 tiles_p, n_p_tiles, tiles_s, t_p, t_s, n_seq):
    i = pl.program_id(0)
    weights = (g_ref[...], win_ref, pw_ref, ps_ref[...], cw_ref[...], wout_ref)
    d = xe_ref.shape[1]

    def token_tile(tile_ref, prev_ref, next_ref, tin, last, t_seq):
        n = tm + 2 * EXT

        @pl.when(tin == 0)
        def _():
            xe_ref[0:EXT, :] = meta_ref[...]

        @pl.when(tin != 0)
        def _():
            xe_ref[0:EXT, :] = prev_ref[...]

        xe_ref[EXT:EXT + tm, :] = tile_ref[...]

        @pl.when(tin == last)
        def _():
            xe_ref[EXT + tm:n, :] = jnp.zeros((EXT, d), F32)

        @pl.when(tin != last)
        def _():
            xe_ref[EXT + tm:n, :] = next_ref[...]

        pos = lax.broadcasted_iota(jnp.int32, (n, 1), 0) + tin * tm
        y = _even_mix(xe_ref[0:n, :], pos, t_seq + N_META, *weights)
        out_ref[...] = xe_ref[EXT:EXT + tm, :] + y[EXT:EXT + tm]

    @pl.when(i < n_p_tiles)
    def _prompt():
        token_tile(xp_ref, xpp_ref, xpn_ref, i % tiles_p, tiles_p - 1, t_p)

    @pl.when((i >= n_p_tiles) & (i < n_tiles))
    def _sample():
        token_tile(xs_ref, xsp_ref, xsn_ref, (i - n_p_tiles) % tiles_s, tiles_s - 1, t_s)

    @pl.when(i == n_tiles)
    def _meta():
        grp = 3 * EXT
        n = grp * n_seq
        for b in range(n_seq):
            xe_ref[grp * b:grp * b + EXT, :] = jnp.zeros((EXT, d), F32)
            xe_ref[grp * b + EXT:grp * b + 2 * EXT, :] = meta_ref[...]
            xe_ref[grp * b + 2 * EXT:grp * (b + 1), :] = first_ref[EXT * b:EXT * (b + 1), :]
        pos = lax.broadcasted_iota(jnp.int32, (n, 1), 0) % grp - EXT
        y = _even_mix(xe_ref[0:n, :], pos, jnp.int32(1 << 30), *weights)
        out_ref[...] = jnp.zeros(out_ref.shape, F32)
        for b in range(n_seq):
            out_ref[EXT * b:EXT * (b + 1), :] = meta_ref[...] + y[grp * b + EXT:grp * b + 2 * EXT]


def _even_layer(xp, xs, meta, xfirst, g, win, pw, ps, cw, wout, plan):
    tm = plan["tm"]
    d = xp.shape[1]
    n_tiles = plan["n_main"] // tm
    n_p_tiles = plan["n_p"] // tm
    n_s_tiles = n_tiles - n_p_tiles
    blk = tm // EXT
    kern = functools.partial(
        _even_kernel, tm=tm, n_tiles=n_tiles, tiles_p=plan["t_p"] // tm, n_p_tiles=n_p_tiles,
        tiles_s=plan["t_s"] // tm, t_p=plan["t_p"], t_s=plan["t_s"], n_seq=plan["n_seq"])
    xe_rows = max(tm + 2 * EXT, 3 * EXT * plan["n_seq"])

    def specs(first_tile, count):
        last = count * blk - 1
        tile = lambda i: jnp.clip(i - first_tile, 0, count - 1)
        return [
            pl.BlockSpec((tm, d), lambda i: (tile(i), 0)),
            pl.BlockSpec((EXT, d), lambda i: (jnp.clip(tile(i) * blk - 1, 0, last), 0)),
            pl.BlockSpec((EXT, d), lambda i: (jnp.clip((tile(i) + 1) * blk, 0, last), 0)),
        ]

    return pl.pallas_call(
        kern,
        grid=(n_tiles + 1,),
        in_specs=specs(0, n_p_tiles) + specs(n_p_tiles, n_s_tiles) + [
            _const_spec(meta.shape), _const_spec(xfirst.shape), _const_spec(g.shape),
            _const_spec(win.shape), _const_spec(pw.shape), _const_spec(ps.shape),
            _const_spec(cw.shape), _const_spec(wout.shape),
        ],
        out_specs=pl.BlockSpec((tm, d), lambda i: (i, 0)),
        out_shape=jax.ShapeDtypeStruct((plan["np_rows"], d), F32),
        scratch_shapes=[pltpu.VMEM((xe_rows, d), F32)],
        compiler_params=pltpu.CompilerParams(
            dimension_semantics=("arbitrary",), vmem_limit_bytes=VMEM_LIMIT),
        name="even_mixer",
    )(xp, xp, xp, xs, xs, xs, meta, xfirst, g, win, pw, ps, cw, wout)


def _ffn_kernel(x_ref, g_ref, wg_ref, wu_ref, wd_ref, out_ref):
    x = x_ref[...]
    xn = _rms(x, g_ref[...]).astype(BF16)
    act = (_silu(_dot(xn, wg_ref[...])) * _dot(xn, wu_ref[...])).astype(BF16)
    out_ref[...] = x + _dot(act, wd_ref[...])


def _ffn_layer(h, g, wg, wu, wd, plan):
    tm = plan["tm"]
    d = h.shape[1]
    return pl.pallas_call(
        _ffn_kernel,
        grid=(h.shape[0] // tm,),
        in_specs=[pl.BlockSpec((tm, d), lambda i: (i, 0)), _const_spec(g.shape),
                  _const_spec(wg.shape), _const_spec(wu.shape), _const_spec(wd.shape)],
        out_specs=pl.BlockSpec((tm, d), lambda i: (i, 0)),
        out_shape=jax.ShapeDtypeStruct(h.shape, F32),
        compiler_params=pltpu.CompilerParams(
            dimension_semantics=("arbitrary",), vmem_limit_bytes=VMEM_LIMIT),
        name="dense_swiglu",
    )(h, g, wg, wu, wd)


def _qkv_kernel(x_ref, g_ref, wqk_ref, wvt_ref, qk_ref, vt_ref):
    xn = _rms(x_ref[...], g_ref[...]).astype(BF16)
    qk_ref[...] = _dot(xn, wqk_ref[...]).astype(BF16)
    vt_ref[...] = _dot_nt(wvt_ref[...], xn).astype(BF16)


def _qkv_layer(h, g, wqk, wvt, plan):
    tm = plan["tm"]
    d = h.shape[1]
    return pl.pallas_call(
        _qkv_kernel,
        grid=(h.shape[0] // tm,),
        in_specs=[pl.BlockSpec((tm, d), lambda i: (i, 0)), _const_spec(g.shape),
                  _const_spec(wqk.shape), _const_spec(wvt.shape)],
        out_specs=[pl.BlockSpec((tm, wqk.shape[1]), lambda i: (i, 0)),
                   pl.BlockSpec((wvt.shape[0], tm), lambda i: (0, i))],
        out_shape=[jax.ShapeDtypeStruct((h.shape[0], wqk.shape[1]), BF16),
                   jax.ShapeDtypeStruct((wvt.shape[0], h.shape[0]), BF16)],
        compiler_params=pltpu.CompilerParams(
            dimension_semantics=("arbitrary",), vmem_limit_bytes=VMEM_LIMIT),
        name="qkv_proj",
    )(h, g, wqk, wvt)


def _attn_kernel(q_ref, kc_ref, kp_ref, kn_ref, vc_ref, vp_ref, vn_ref, qm_ref, km_ref, vm_ref,
                 vmt_ref, tab_ref, mb_ref, o_ref, om_ref, kx_ref, vx_ref, s_ref, *, cr, chunks_p,
                 n_p_chunks, chunks_s, rows_p, rows_s):
    ci = pl.program_id(1)
    is_p = ci < n_p_chunks
    cin = jnp.where(is_p, ci % chunks_p, (ci - n_p_chunks) % chunks_s)
    rows = jnp.where(is_p, rows_p, rows_s)
    halo = HALO_ROWS * GRID_W
    ct = cr * GRID_W

    kx_ref[0:halo, :] = kp_ref[...]
    kx_ref[halo:halo + ct, :] = kc_ref[...]
    kx_ref[halo + ct:, :] = kn_ref[...]
    ones_rows = V_ROWS - HEAD_DIM
    vmt_aug = []
    for hh in range(HEADS_PER_BLOCK):
        hs = slice(hh * HEAD_DIM, (hh + 1) * HEAD_DIM)
        vx_ref[hh, 0:HEAD_DIM, 0:halo] = vp_ref[hs, :]
        vx_ref[hh, 0:HEAD_DIM, halo:halo + ct] = vc_ref[hs, :]
        vx_ref[hh, 0:HEAD_DIM, halo + ct:] = vn_ref[hs, :]
        vx_ref[hh, HEAD_DIM:, :] = jnp.ones((ones_rows, ct + 2 * halo), BF16)
        vmt_aug.append(jnp.concatenate(
            [vmt_ref[hs, :], jnp.ones((ones_rows, N_META), BF16)], axis=0))

    lane = lax.broadcasted_iota(jnp.int32, (1, LANES), 1)
    head_masks = [(lane // HEAD_DIM) == hh for hh in range(HEADS_PER_BLOCK)]
    km = km_ref[...]
    r0c = cin * cr

    n_blocks = cr // Q_ROWS

    def scores(bi, slot):
        r0 = r0c + bi * Q_ROWS
        edge = jnp.where(r0 == 0, 0, jnp.where(r0 == rows - Q_ROWS, 2, 1))
        off = pl.multiple_of(bi * NQ, NQ)
        q = q_ref[pl.ds(off, NQ), :]
        qq = jnp.concatenate(
            [jnp.where(head_masks[hh], q, jnp.zeros_like(q)) for hh in range(HEADS_PER_BLOCK)],
            axis=0)
        s_ref[slot, 0:NK, :] = _dot_nt(kx_ref[pl.ds(off, NK), :], qq) + tab_ref[edge, 0:NK, :]
        s_ref[slot, NK:NK + N_META, :] = _dot_nt(km, qq) + tab_ref[edge, NK:NK + N_META, :]

    def softmax_pv(bi, slot):
        off = pl.multiple_of(bi * NQ, NQ)
        s = s_ref[slot]
        pb = jnp.exp2(s - jnp.max(s, axis=0, keepdims=True)).astype(BF16)
        outs = []
        for hh in range(HEADS_PER_BLOCK):
            qs = slice(hh * NQ, (hh + 1) * NQ)
            oa = (_dot(vx_ref[hh, :, pl.ds(off, NK)], pb[0:NK, qs])
                  + _dot(vmt_aug[hh], pb[NK:NK + N_META, qs]))
            outs.append(oa[0:HEAD_DIM] / oa[HEAD_DIM:HEAD_DIM + 1])
        o_ref[pl.ds(off, NQ), :] = jnp.concatenate(outs, axis=0).T.astype(BF16)

    scores(0, 0)
    scores(1, 1)

    def quad_body(b4, carry):
        b = 4 * b4
        for k in range(4):
            scores(jnp.minimum(b + k + 2, n_blocks - 1), (k + 2) % 4)
            softmax_pv(b + k, k)
        return carry

    lax.fori_loop(0, n_blocks // 4, quad_body, 0)

    qm = qm_ref[...]
    vm = vm_ref[...]
    out = None
    for hh in range(HEADS_PER_BLOCK):
        qh = jnp.where(head_masks[hh], qm, jnp.zeros_like(qm))
        sm = _dot_nt(qh, km) + mb_ref[hh]
        pm = jnp.exp2(sm - jnp.max(sm, axis=-1, keepdims=True))
        o = _dot(pm.astype(BF16), vm) / jnp.sum(pm, axis=-1, keepdims=True)
        out = o if out is None else jnp.where(head_masks[hh], o, out)
    om_ref[...] = out.astype(BF16)


def _attn_layer(qk, vt, vm, vmt, bias_tab, meta_bias, plan):
    cr = plan["cr"]
    ct = cr * GRID_W
    halo = HALO_ROWS * GRID_W
    n_main = plan["n_main"]
    n_chunks = n_main // ct
    n_hb = NA_HEADS // HEADS_PER_BLOCK
    hb = ct // halo
    last_halo = n_main // halo - 1
    meta_blk0 = n_main // N_META
    n_p_chunks = plan["n_p"] // ct
    chunks_p = plan["t_p"] // ct
    chunks_s = plan["t_s"] // ct
    b_p = plan["n_p"] // plan["t_p"]

    def seq_of(ci):
        return jnp.where(ci < n_p_chunks, ci // chunks_p, b_p + (ci - n_p_chunks) // chunks_s)

    def cur(off):
        return pl.BlockSpec((ct, LANES), lambda h, c: (c, off + h))

    def prev(off):
        return pl.BlockSpec((halo, LANES), lambda h, c: (jnp.maximum(c * hb - 1, 0), off + h))

    def nxt(off):
        return pl.BlockSpec((halo, LANES),
                            lambda h, c: (jnp.minimum((c + 1) * hb, last_halo), off + h))

    def meta(off):
        return pl.BlockSpec((N_META, LANES), lambda h, c: (meta_blk0 + seq_of(c), off + h))

    def vt_cur():
        return pl.BlockSpec((LANES, ct), lambda h, c: (h, c))

    def vt_prev():
        return pl.BlockSpec((LANES, halo), lambda h, c: (h, jnp.maximum(c * hb - 1, 0)))

    def vt_next():
        return pl.BlockSpec((LANES, halo), lambda h, c: (h, jnp.minimum((c + 1) * hb, last_halo)))

    kern = functools.partial(
        _attn_kernel, cr=cr, chunks_p=chunks_p, n_p_chunks=n_p_chunks, chunks_s=chunks_s,
        rows_p=plan["t_p"] // GRID_W, rows_s=plan["t_s"] // GRID_W)
    width = NA_HEADS * HEAD_DIM
    return pl.pallas_call(
        kern,
        grid=(n_hb, n_chunks),
        in_specs=[
            cur(0), cur(n_hb), prev(n_hb), nxt(n_hb), vt_cur(), vt_prev(), vt_next(),
            meta(0), meta(n_hb),
            pl.BlockSpec((N_META, LANES), lambda h, c: (seq_of(c), h)),
            pl.BlockSpec((None, LANES, N_META), lambda h, c: (seq_of(c), h, 0)),
            pl.BlockSpec((None,) + bias_tab.shape[1:], lambda h, c: (h, 0, 0, 0)),
            pl.BlockSpec((None, HEADS_PER_BLOCK, 1, N_META), lambda h, c: (h, 0, 0, 0)),
        ],
        out_specs=[
            pl.BlockSpec((ct, LANES), lambda h, c: (c, h)),
            pl.BlockSpec((N_META, LANES), lambda h, c: (seq_of(c), h)),
        ],
        out_shape=[jax.ShapeDtypeStruct((n_main, width), BF16),
                   jax.ShapeDtypeStruct((plan["n_seq"] * N_META, width), BF16)],
        scratch_shapes=[pltpu.VMEM((ct + 2 * halo, LANES), BF16),
                        pltpu.VMEM((HEADS_PER_BLOCK, V_ROWS, ct + 2 * halo), BF16),
                        pltpu.VMEM((4, NK + N_META, HEADS_PER_BLOCK * NQ), F32)],
        compiler_params=pltpu.CompilerParams(
            dimension_semantics=("arbitrary", "arbitrary"), vmem_limit_bytes=VMEM_LIMIT),
        name="neighbourhood_attention",
    )(qk, qk, qk, qk, vt, vt, vt, qk, qk, vm, vmt, bias_tab, meta_bias)


def _attn_bias_table(rel_bias, meta_bias):
    n_heads, n_row_off, n_col_off = rel_bias.shape
    half = n_col_off // 2
    ring = jnp.concatenate([
        rel_bias[:, :, half::-1],
        jnp.zeros((n_heads, n_row_off, 2 * GRID_W - n_col_off), F32),
        rel_bias[:, :, :half:-1]], axis=-1)
    toe = jnp.tile(ring, (1, 1, GRID_W))[:, :, :GRID_W * (2 * GRID_W - 1)]
    toe = toe.reshape(n_heads, n_row_off, GRID_W, 2 * GRID_W - 1)[..., :GRID_W]
    qc = np.arange(GRID_W)[None, :]
    kc = np.arange(GRID_W)[:, None]
    cstart = np.clip(qc - NA_WIN_COLS // 2, 0, GRID_W - NA_WIN_COLS)
    col_ok = (kc >= cstart) & (kc < cstart + NA_WIN_COLS)
    toe = jnp.where(col_ok[None, None], toe, -1e30)
    n_hb = n_heads // HEADS_PER_BLOCK
    toe = toe.reshape(n_hb, HEADS_PER_BLOCK, n_row_off, GRID_W, GRID_W)
    per_a = jnp.stack([toe[:, :, HALO_ROWS - 1 - a:HALO_ROWS - 1 - a + WIN_ROWS]
                       for a in range(Q_ROWS)], axis=2)
    per_a = jnp.transpose(per_a, (0, 3, 4, 1, 2, 5))
    a = np.arange(Q_ROWS)[None, :]
    b = np.arange(WIN_ROWS)[:, None]
    row_ok = np.stack([(b >= HALO_ROWS) & (a >= 0),
                       (b - a >= 0) & (b - a < NA_WIN_ROWS),
                       (b < NA_WIN_ROWS) & (a >= 0)])
    tab = jnp.where(row_ok[None, :, :, None, None, :, None], per_a[:, None], -1e30)
    tab = tab.reshape(n_hb, 3, NK, HEADS_PER_BLOCK * NQ)
    meta = jnp.transpose(meta_bias.reshape(n_hb, HEADS_PER_BLOCK, N_META), (0, 2, 1))
    meta = jnp.broadcast_to(meta[:, None, :, :, None], (n_hb, 3, N_META, HEADS_PER_BLOCK, NQ))
    meta = meta.reshape(n_hb, 3, N_META, HEADS_PER_BLOCK * NQ)
    return jnp.concatenate([tab, meta], axis=2).astype(F32)


def _route_kernel(o_ref, om_ref, h_ref, wout_ref, g_ref, wrt_ref, brc_ref, triu_ref,
                  h_out, xn_out, route_out, cnt_out, *, n_tiles):
    i = pl.program_id(0)

    def body(o):
        h = h_ref[...] + _dot(o, wout_ref[...])
        h_out[...] = h
        xn = _rms(h, g_ref[...]).astype(BF16)
        xn_out[...] = xn
        logits = _dot_nt(wrt_ref[...], xn) + brc_ref[...]
        tm = logits.shape[1]
        sub = lax.broadcasted_iota(jnp.int32, (N_EXPERTS, tm), 0)
        m1 = jnp.max(logits, axis=0, keepdims=True)
        i1 = jnp.min(jnp.where(logits == m1, sub, N_EXPERTS), axis=0, keepdims=True)
        oh1 = sub == i1
        rest = jnp.where(oh1, -jnp.inf, logits)
        m2 = jnp.max(rest, axis=0, keepdims=True)
        i2 = jnp.min(jnp.where(rest == m2, sub, N_EXPERTS), axis=0, keepdims=True)
        oh2 = sub == i2
        t = jnp.exp(m2 - m1)
        g1 = 1.0 / (1.0 + t)
        g2 = t * g1
        oh = jnp.where(oh1 | oh2, 1.0, 0.0)
        cum = _dot(oh.astype(BF16), triu_ref[...])
        lr1 = jnp.sum(jnp.where(oh1, cum, 0.0), axis=0, keepdims=True)
        lr2 = jnp.sum(jnp.where(oh2, cum, 0.0), axis=0, keepdims=True)
        route = jnp.zeros((N_EXPERTS, tm), F32)
        for k, val in enumerate([i1.astype(F32), i2.astype(F32), lr1, lr2, g1, g2]):
            route = jnp.where(sub == k, val, route)
        route_out[...] = route
        cnt_out[0] = jnp.broadcast_to(jnp.sum(oh, axis=1, keepdims=True), (N_EXPERTS, LANES))

    @pl.when(i < n_tiles)
    def _():
        body(o_ref[...])

    @pl.when(i == n_tiles)
    def _():
        body(om_ref[...])


def _route_layer(o_main, o_meta, h, wout, g, wrt, brc, plan):
    tm = plan["tm"]
    d = h.shape[1]
    n_tiles = plan["n_main"] // tm
    triu = (jnp.arange(tm)[:, None] < jnp.arange(tm)[None, :]).astype(BF16)
    kern = functools.partial(_route_kernel, n_tiles=n_tiles)
    rows = (n_tiles + 1) * tm
    return pl.pallas_call(
        kern,
        grid=(n_tiles + 1,),
        in_specs=[
            pl.BlockSpec((tm, d), lambda i: (jnp.minimum(i, n_tiles - 1), 0)),
            _const_spec(o_meta.shape),
            pl.BlockSpec((tm, d), lambda i: (i, 0)),
            _const_spec(wout.shape), _const_spec(g.shape), _const_spec(wrt.shape),
            _const_spec(brc.shape), _const_spec(triu.shape),
        ],
        out_specs=[
            pl.BlockSpec((tm, d), lambda i: (i, 0)),
            pl.BlockSpec((tm, d), lambda i: (i, 0)),
            pl.BlockSpec((N_EXPERTS, tm), lambda i: (0, i)),
            pl.BlockSpec((1, N_EXPERTS, LANES), lambda i: (i, 0, 0)),
        ],
        out_shape=[jax.ShapeDtypeStruct((rows, d), F32),
                   jax.ShapeDtypeStruct((rows, d), BF16),
                   jax.ShapeDtypeStruct((N_EXPERTS, rows), F32),
                   jax.ShapeDtypeStruct((n_tiles + 1, N_EXPERTS, LANES), F32)],
        compiler_params=pltpu.CompilerParams(
            dimension_semantics=("arbitrary",), vmem_limit_bytes=VMEM_LIMIT),
        name="outproj_router",
    )(o_main, o_meta, h, wout, g, wrt, brc, triu)


def _for_each_run(tab_ref, tm, fn):
    for e in range(N_EXPERTS):
        local, length, sorted_row = tab_ref[0, 0, e], tab_ref[0, 1, e], tab_ref[0, 2, e]
        n = tm
        while n >= SEG_ALIGN:
            @pl.when((length & n) != 0)
            def _(n=n):
                done = (length // (2 * n)) * (2 * n)
                fn(pl.multiple_of(local + done, SEG_ALIGN),
                   pl.multiple_of(sorted_row + done, SEG_ALIGN), n)
            n //= 2


def _dispatch_kernel(tab_ref, tail_ref, slot_ref, route_ref, x_ref, xs_ref, stage_ref, zero_ref,
                     sem, zsem, *, tm, te):
    i = pl.program_id(0)
    d = x_ref.shape[1]

    @pl.when(i == 0)
    def _():
        zero_ref[...] = jnp.zeros(zero_ref.shape, F32)

        def zero_tile(start):
            cp = pltpu.make_async_copy(
                zero_ref, xs_ref.at[pl.ds(pl.multiple_of(start, te), te)], zsem)
            cp.start()
            cp.wait()

        for e in range(N_EXPERTS):
            @pl.when(tail_ref[e] >= 0)
            def _():
                zero_tile(tail_ref[e])

        def zero_unused(u, carry):
            zero_tile(u * te)
            return carry

        lax.fori_loop(tail_ref[N_EXPERTS], xs_ref.shape[0] // te, zero_unused, 0)

    s_rows = stage_ref.shape[0]
    slot_iota = lax.broadcasted_iota(jnp.int32, (s_rows, tm), 0)
    first = slot_iota == slot_ref[0:1, :]
    second = slot_iota == slot_ref[1:2, :]
    perm = jnp.where(first | second, 1.0, 0.0).astype(BF16)
    stage_ref[:, 0:d] = _dot(perm, x_ref[...])
    gate = jnp.sum(jnp.where(first, route_ref[4:5, :], 0.0)
                   + jnp.where(second, route_ref[5:6, :], 0.0), axis=1, keepdims=True)
    stage_ref[:, d:] = jnp.broadcast_to(gate, (s_rows, LANES))

    def piece(local_row, sorted_row, n):
        return pltpu.make_async_copy(
            stage_ref.at[pl.ds(local_row, n)], xs_ref.at[pl.ds(sorted_row, n)], sem)

    _for_each_run(tab_ref, tm, lambda lr, sr, n: piece(lr, sr, n).start())
    _for_each_run(tab_ref, tm, lambda lr, sr, n: piece(lr, sr, n).wait())


def _dispatch(xn, slot_t, route_t, tab, tails, plan):
    tm = plan["tm"]
    te = plan["te"]
    d = xn.shape[1]
    kern = functools.partial(_dispatch_kernel, tm=tm, te=te)
    return pl.pallas_call(
        kern,
        grid=(xn.shape[0] // tm,),
        in_specs=[
            pl.BlockSpec((1, 3, N_EXPERTS), lambda i: (i, 0, 0), memory_space=pltpu.SMEM),
            pl.BlockSpec(memory_space=pltpu.SMEM),
            pl.BlockSpec((2, tm), lambda i: (0, i)),
            pl.BlockSpec((N_EXPERTS, tm), lambda i: (0, i)),
            pl.BlockSpec((tm, d), lambda i: (i, 0)),
        ],
        out_specs=pl.BlockSpec(memory_space=pl.ANY),
        out_shape=jax.ShapeDtypeStruct((plan["rows_sorted"], d + LANES), F32),
        scratch_shapes=[pltpu.VMEM((plan["s_rows"], d + LANES), F32),
                        pltpu.VMEM((te, d + LANES), F32),
                        pltpu.SemaphoreType.DMA(()), pltpu.SemaphoreType.DMA(())],
        compiler_params=pltpu.CompilerParams(
            dimension_semantics=("arbitrary",), vmem_limit_bytes=VMEM_LIMIT),
        name="moe_dispatch",
    )(tab, tails, slot_t, route_t, xn)


def _expert_kernel(te_ref, nu_ref, x_ref, wg_ref, wu_ref, wd_ref, out_ref, *, sub):
    j = pl.program_id(0)
    c = pl.program_id(1)

    @pl.when(c == 0)
    def _():
        out_ref[...] = jnp.zeros(out_ref.shape, F32)

    @pl.when(j < nu_ref[0])
    def _():
        d = out_ref.shape[1]
        xb = x_ref[:, 0:d].astype(BF16)
        fc = wg_ref.shape[1]
        y = None
        for lo in range(0, fc, sub):
            hi = min(lo + sub, fc)
            act = (_silu(_dot(xb, wg_ref[:, lo:hi])) * _dot(xb, wu_ref[:, lo:hi])).astype(BF16)
            part = _dot(act, wd_ref[lo:hi, :])
            y = part if y is None else y + part
        out_ref[...] += x_ref[:, d:d + 1] * y


def _experts(xs, tile_expert, n_used, wg, wu, wd, plan):
    te = plan["te"]
    fc = plan["fc"]
    d = xs.shape[1] - LANES
    n_tiles = xs.shape[0] // te
    n_c = wg.shape[2] // fc

    def row(j, c, te_ref, nu_ref):
        return (jnp.minimum(j, nu_ref[0] - 1), 0)

    def chunk(j, c, nu_ref):
        return jnp.where(j < nu_ref[0], c, n_c - 1)

    def w_in(j, c, te_ref, nu_ref):
        return (te_ref[jnp.minimum(j, nu_ref[0] - 1)], 0, chunk(j, c, nu_ref))

    def w_out(j, c, te_ref, nu_ref):
        return (te_ref[jnp.minimum(j, nu_ref[0] - 1)], chunk(j, c, nu_ref), 0)

    grid_spec = pltpu.PrefetchScalarGridSpec(
        num_scalar_prefetch=2,
        grid=(n_tiles, n_c),
        in_specs=[
            pl.BlockSpec((te, d + LANES), row),
            pl.BlockSpec((None, d, fc), w_in),
            pl.BlockSpec((None, d, fc), w_in),
            pl.BlockSpec((None, fc, d), w_out),
        ],
        out_specs=pl.BlockSpec((te, d), lambda j, c, te_ref, nu_ref: (j, 0)),
    )
    return pl.pallas_call(
        functools.partial(_expert_kernel, sub=plan["fsub"]),
        grid_spec=grid_spec,
        out_shape=jax.ShapeDtypeStruct((xs.shape[0], d), F32),
        compiler_params=pltpu.CompilerParams(
            dimension_semantics=("arbitrary", "arbitrary"), vmem_limit_bytes=VMEM_LIMIT),
        name="moe_experts",
    )(tile_expert, n_used, xs, wg, wu, wd)


def _combine_kernel(tabc_ref, tabn_ref, slot_ref, h_ref, g_ref, ys_ref, outp_ref, outs_ref,
                    y_ref, sems, *, tm, n_tiles, n_p_tiles):
    i = pl.program_id(0)

    def piece(slot, local_row, sorted_row, n):
        return pltpu.make_async_copy(
            ys_ref.at[pl.ds(sorted_row, n)], y_ref.at[slot, pl.ds(local_row, n)], sems.at[slot])

    @pl.when(i == 0)
    def _():
        y_ref[...] = jnp.zeros(y_ref.shape, F32)
        _for_each_run(tabc_ref, tm, lambda lr, sr, n: piece(0, lr, sr, n).start())

    @pl.when(i + 1 < n_tiles)
    def _():
        _for_each_run(tabn_ref, tm, lambda lr, sr, n: piece((i + 1) % 2, lr, sr, n).start())

    slot = i % 2
    _for_each_run(tabc_ref, tm, lambda lr, sr, n: piece(slot, lr, sr, n).wait())

    y = y_ref[slot]
    y_hi = y.astype(BF16)
    y_lo = (y - y_hi.astype(F32)).astype(BF16)
    s_rows = y.shape[0]
    slot_iota = lax.broadcasted_iota(jnp.int32, (tm, s_rows), 1)
    pick = jnp.where((slot_iota == slot_ref[:, 0:1]) | (slot_iota == slot_ref[:, 1:2]),
                     1.0, 0.0).astype(BF16)
    h = h_ref[...] + _dot(pick, y_hi) + _dot(pick, y_lo)
    out = _rms(h, g_ref[...])

    @pl.when(i < n_p_tiles)
    def _():
        outp_ref[...] = out

    @pl.when(i >= n_p_tiles)
    def _():
        outs_ref[...] = out


def _combine(h, slot_c, tab, ys, g, plan):
    tm = plan["tm"]
    d = h.shape[1]
    n_tiles = plan["n_main"] // tm
    n_p_tiles = plan["n_p"] // tm
    kern = functools.partial(_combine_kernel, tm=tm, n_tiles=n_tiles, n_p_tiles=n_p_tiles)
    return pl.pallas_call(
        kern,
        grid=(n_tiles,),
        in_specs=[
            pl.BlockSpec((1, 3, N_EXPERTS), lambda i: (i, 0, 0), memory_space=pltpu.SMEM),
            pl.BlockSpec((1, 3, N_EXPERTS), lambda i: (jnp.minimum(i + 1, n_tiles - 1), 0, 0),
                         memory_space=pltpu.SMEM),
            pl.BlockSpec((tm, 2), lambda i: (i, 0)),
            pl.BlockSpec((tm, d), lambda i: (i, 0)),
            _const_spec(g.shape),
            pl.BlockSpec(memory_space=pl.ANY),
        ],
        out_specs=[
            pl.BlockSpec((tm, d), lambda i: (jnp.minimum(i, n_p_tiles - 1), 0)),
            pl.BlockSpec((tm, d), lambda i: (jnp.maximum(i - n_p_tiles, 0), 0)),
        ],
        out_shape=[jax.ShapeDtypeStruct((plan["n_p"], d), F32),
                   jax.ShapeDtypeStruct((plan["n_main"] - plan["n_p"], d), F32)],
        scratch_shapes=[pltpu.VMEM((2, plan["s_rows"], d), F32),
                        pltpu.SemaphoreType.DMA((2,))],
        compiler_params=pltpu.CompilerParams(
            dimension_semantics=("arbitrary",), vmem_limit_bytes=VMEM_LIMIT),
        name="moe_combine_norm",
    )(tab, tab, slot_c, h, g, ys)


def _make_plan(b_p, t_p, b_s, t_s, d_ff_expert):
    tm = 512 if t_p % 512 == 0 and t_s % 512 == 0 else 128
    te = 1024 if tm == 512 else 256
    n_p = b_p * t_p
    n_main = n_p + b_s * t_s
    n_seq = b_p + b_s
    assert n_seq * N_META <= tm and t_p % tm == 0 and t_s % tm == 0
    np_rows = n_main + tm
    rows_p, rows_s = t_p // GRID_W, t_s // GRID_W
    cr = 64 if rows_p % 64 == 0 and rows_s % 64 == 0 else 4 * Q_ROWS
    assert rows_p % cr == 0 and rows_s % cr == 0 and cr % (4 * Q_ROWS) == 0
    fc = 1792 if d_ff_expert % 1792 == 0 else d_ff_expert
    fsub = 2 * MXU_DIM
    run_pad = N_EXPERTS * (SEG_ALIGN - 1)
    s_rows = -(-(2 * tm + run_pad) // SEG_ALIGN) * SEG_ALIGN
    rows_sorted = -(-(2 * np_rows + run_pad * (np_rows // tm)) // te) * te + N_EXPERTS * te
    return dict(tm=tm, te=te, n_p=n_p, n_main=n_main, n_seq=n_seq, np_rows=np_rows, t_p=t_p,
                t_s=t_s, cr=cr, fc=fc, fsub=fsub, s_rows=s_rows, rows_sorted=rows_sorted)


def kernel(x_prompt, x_sample, meta_tokens, norm_mix, norm_ffn, norm_final, even_w_in, pool_w,
           pool_scale, conv_w, even_w_out, ffn_w_gate, ffn_w_up, ffn_w_down, na_w_qkv,
           na_rel_bias, na_meta_bias, na_w_out, moe_w_router, moe_b_router, moe_w_gate,
           moe_w_up, moe_w_down):
    b_p, t_p, d = x_prompt.shape
    b_s, t_s, _ = x_sample.shape
    plan = _make_plan(b_p, t_p, b_s, t_s, moe_w_gate.shape[-1])
    tm, te = plan["tm"], plan["te"]
    n_main, n_seq, np_rows = plan["n_main"], plan["n_seq"], plan["np_rows"]
    n_meta_rows = n_seq * N_META

    xfirst = jnp.concatenate([x_prompt[:, :EXT].reshape(-1, d), x_sample[:, :EXT].reshape(-1, d)])

    row = lambda v: v.reshape(1, -1).astype(F32)
    h = _even_layer(x_prompt.reshape(-1, d), x_sample.reshape(-1, d), meta_tokens, xfirst,
                    row(norm_mix[0]), even_w_in[0].astype(BF16), pool_w[0].astype(BF16),
                    row(pool_scale[0]), conv_w[0], even_w_out[0].astype(BF16), plan)
    h = _ffn_layer(h, row(norm_ffn[0]), ffn_w_gate[0].astype(BF16), ffn_w_up[0].astype(BF16),
                   ffn_w_down[0].astype(BF16), plan)

    width = NA_HEADS * HEAD_DIM
    log2e = float(np.log2(np.e))
    qscale = jnp.concatenate([jnp.full((width,), HEAD_DIM ** -0.5 * log2e, F32),
                              jnp.ones((width,), F32)])
    qk, vt = _qkv_layer(h, row(norm_mix[1]), (na_w_qkv[0][:, :2 * width] * qscale).astype(BF16),
                        na_w_qkv[0][:, 2 * width:].T.astype(BF16), plan)
    vt_meta = vt[:, n_main:n_main + n_meta_rows]
    n_hb = NA_HEADS // HEADS_PER_BLOCK
    o_main, o_meta = _attn_layer(
        qk, vt, vt_meta.T, vt_meta.reshape(width, n_seq, N_META).transpose(1, 0, 2),
        _attn_bias_table(na_rel_bias[0] * log2e, na_meta_bias[0] * log2e),
        (na_meta_bias[0] * log2e).reshape(n_hb, HEADS_PER_BLOCK, 1, N_META), plan)
    o_meta = jnp.concatenate([o_meta, jnp.zeros((tm - n_meta_rows, width), BF16)])

    h, xn, route_t, tile_cnt = _route_layer(
        o_main, o_meta, h, na_w_out[0].astype(BF16), row(norm_ffn[1]),
        moe_w_router[0].T.astype(BF16), moe_b_router[0].reshape(-1, 1).astype(F32), plan)

    cnt = tile_cnt[:, :, 0].astype(jnp.int32)
    seg = (cnt + SEG_ALIGN - 1) // SEG_ALIGN * SEG_ALIGN
    local = jnp.cumsum(seg, axis=1) - seg
    before = jnp.cumsum(seg, axis=0) - seg
    padded = (jnp.sum(seg, axis=0) + te - 1) // te * te
    ends = jnp.cumsum(padded)
    tab = jnp.stack([local, seg, (ends - padded)[None, :] + before], axis=1).astype(jnp.int32)
    experts_t = route_t[0:2].astype(jnp.int32)
    local_tok = jnp.repeat(local, tm, axis=0).T
    pick = experts_t[:, None, :] == jnp.arange(N_EXPERTS)[None, :, None]
    slot_t = jnp.sum(jnp.where(pick, local_tok[None], 0), axis=1) + route_t[2:4].astype(jnp.int32)
    tails = jnp.concatenate([jnp.where(padded > 0, ends - te, -1), ends[-1:] // te])
    tails = tails.astype(jnp.int32)
    n_row_tiles = plan["rows_sorted"] // te
    tile_expert = jnp.minimum(
        jnp.sum(ends[None, :] <= (jnp.arange(n_row_tiles) * te)[:, None], axis=1),
        N_EXPERTS - 1).astype(jnp.int32)
    n_used = (ends[-1:] // te).astype(jnp.int32)

    xs = _dispatch(xn, slot_t, route_t, tab, tails, plan)
    ys = _experts(xs, tile_expert, n_used, moe_w_gate[0].astype(BF16), moe_w_up[0].astype(BF16),
                  moe_w_down[0].astype(BF16), plan)
    y_p, y_s = _combine(h, slot_t.T, tab, ys, row(norm_final), plan)
    return (y_p.reshape(b_p, t_p, d), y_s.reshape(b_s, t_s, d))
```

```python
import functools

import jax
import jax.numpy as jnp
import numpy as np
from jax import lax
from jax.experimental import pallas as pl
from jax.experimental.pallas import tpu as pltpu

F32 = jnp.float32
BF16 = jnp.bfloat16

N_META = 16
GRID_W = 64
RMS_EPS = 1e-6
POOL_WINDOWS = (2, 4, 8, 16)
POOL_GROUP = 128
POOL_WIDTH = 512
CONV_WIDTH = 512
NA_HEADS = 16
HEAD_DIM = 64
NA_WIN_ROWS = 8
NA_WIN_COLS = 16
N_EXPERTS = 8
LANES = 128
MXU_DIM = 256
SEG_ALIGN = 8
HEADS_PER_BLOCK = LANES // HEAD_DIM
HALO_ROWS = NA_WIN_ROWS // 2
Q_ROWS = 4
WIN_ROWS = Q_ROWS + NA_WIN_ROWS
NQ = Q_ROWS * GRID_W
NK = WIN_ROWS * GRID_W
V_ROWS = HEAD_DIM + 16
EXT = 16
VMEM_LIMIT = 56 * 1024 * 1024


def _const_spec(shape):
    nd = len(shape)
    return pl.BlockSpec(shape, lambda *_: (0,) * nd, pipeline_mode=pl.Buffered(1))


def _rms(x, g):
    ms = jnp.mean(x * x, axis=-1, keepdims=True)
    return x * lax.rsqrt(ms + RMS_EPS) * g


def _dot(a, b):
    return jnp.dot(a, b, preferred_element_type=F32)


def _dot_nt(a, b):
    return lax.dot_general(a, b, (((1,), (1,)), ((), ())), preferred_element_type=F32)


def _silu(x):
    return x * jax.nn.sigmoid(x)


def _even_mix(xe, pos, seq_len, g, win_ref, pw_ref, ps, cw, wout_ref):
    n = xe.shape[0]
    xn = _rms(xe, g).astype(BF16)
    z = _dot(xn, win_ref[...])
    a_parts = []
    for gi, k in enumerate(POOL_WINDOWS):
        p = z[:, gi * POOL_GROUP:(gi + 1) * POOL_GROUP]
        s = p
        w = 1
        while w < k:
            s = s + pltpu.roll(s, n - w, 0)
            w *= 2
        left = k // 2
        right = k - 1 - left
        s = pltpu.roll(s, left, 0)
        cnt = jnp.minimum(pos + right + 1, seq_len) - jnp.maximum(pos - left, 0)
        inv = 1.0 / jnp.maximum(cnt, 1).astype(F32)
        pooled = (s * inv - p).astype(BF16)
        a_parts.append(_dot(pooled, pw_ref[gi]))
    a = jnp.concatenate(a_parts, axis=-1) * ps
    gb = z[:, POOL_WIDTH:POOL_WIDTH + CONV_WIDTH]
    gc = z[:, POOL_WIDTH + CONV_WIDTH:POOL_WIDTH + 2 * CONV_WIDTH]
    hc = z[:, POOL_WIDTH + 2 * CONV_WIDTH:]
    u = gc * hc
    conv = pltpu.roll(u, 1, 0) * cw[0:1] + u * cw[1:2] + pltpu.roll(u, n - 1, 0) * cw[2:3]
    c = gb * conv
    return (_dot(a.astype(BF16), wout_ref[0:POOL_WIDTH, :])
            + _dot(c.astype(BF16), wout_ref[POOL_WIDTH:, :]))


def _even_kernel(xp_ref, xpp_ref, xpn_ref, xs_ref, xsp_ref, xsn_ref, meta_ref, first_ref, g_ref,
                 win_ref, pw_ref, ps_ref, cw_ref, wout_ref, out_ref, xe_ref, *, tm, n_tiles,
                 tiles_p, n_p_tiles, tiles_s, t_p, t_s, n_seq):
    i = pl.program_id(0)
    weights = (g_ref[...], win_ref, pw_ref, ps_ref[...], cw_ref[...], wout_ref)
    d = xe_ref.shape[1]

    def token_tile(tile_ref, prev_ref, next_ref, tin, last, t_seq):
        n = tm + 2 * EXT

        @pl.when(tin == 0)
        def _():
            xe_ref[0:EXT, :] = meta_ref[...]

        @pl.when(tin != 0)
        def _():
            xe_ref[0:EXT, :] = prev_ref[...]

        xe_ref[EXT:EXT + tm, :] = tile_ref[...]

        @pl.when(tin == last)
        def _():
            xe_ref[EXT + tm:n, :] = jnp.zeros((EXT, d), F32)

        @pl.when(tin != last)
        def _():
            xe_ref[EXT + tm:n, :] = next_ref[...]

        pos = lax.broadcasted_iota(jnp.int32, (n, 1), 0) + tin * tm
        y = _even_mix(xe_ref[0:n, :], pos, t_seq + N_META, *weights)
        out_ref[...] = xe_ref[EXT:EXT + tm, :] + y[EXT:EXT + tm]

    @pl.when(i < n_p_tiles)
    def _prompt():
        token_tile(xp_ref, xpp_ref, xpn_ref, i % tiles_p, tiles_p - 1, t_p)

    @pl.when((i >= n_p_tiles) & (i < n_tiles))
    def _sample():
        token_tile(xs_ref, xsp_ref, xsn_ref, (i - n_p_tiles) % tiles_s, tiles_s - 1, t_s)

    @pl.when(i == n_tiles)
    def _meta():
        grp = 3 * EXT
        n = grp * n_seq
        for b in range(n_seq):
            xe_ref[grp * b:grp * b + EXT, :] = jnp.zeros((EXT, d), F32)
            xe_ref[grp * b + EXT:grp * b + 2 * EXT, :] = meta_ref[...]
            xe_ref[grp * b + 2 * EXT:grp * (b + 1), :] = first_ref[EXT * b:EXT * (b + 1), :]
        pos = lax.broadcasted_iota(jnp.int32, (n, 1), 0) % grp - EXT
        y = _even_mix(xe_ref[0:n, :], pos, jnp.int32(1 << 30), *weights)
        out_ref[...] = jnp.zeros(out_ref.shape, F32)
        for b in range(n_seq):
            out_ref[EXT * b:EXT * (b + 1), :] = meta_ref[...] + y[grp * b + EXT:grp * b + 2 * EXT]


def _even_layer(xp, xs, meta, xfirst, g, win, pw, ps, cw, wout, plan):
    tm = plan["tm"]
    d = xp.shape[1]
    n_tiles = plan["n_main"] // tm
    n_p_tiles = plan["n_p"] // tm
    n_s_tiles = n_tiles - n_p_tiles
    blk = tm // EXT
    kern = functools.partial(
        _even_kernel, tm=tm, n_tiles=n_tiles, tiles_p=plan["t_p"] // tm, n_p_tiles=n_p_tiles,
        tiles_s=plan["t_s"] // tm, t_p=plan["t_p"], t_s=plan["t_s"], n_seq=plan["n_seq"])
    xe_rows = max(tm + 2 * EXT, 3 * EXT * plan["n_seq"])

    def specs(first_tile, count):
        last = count * blk - 1
        tile = lambda i: jnp.clip(i - first_tile, 0, count - 1)
        return [
            pl.BlockSpec((tm, d), lambda i: (tile(i), 0)),
            pl.BlockSpec((EXT, d), lambda i: (jnp.clip(tile(i) * blk - 1, 0, last), 0)),
            pl.BlockSpec((EXT, d), lambda i: (jnp.clip((tile(i) + 1) * blk, 0, last), 0)),
        ]

    return pl.pallas_call(
        kern,
        grid=(n_tiles + 1,),
        in_specs=specs(0, n_p_tiles) + specs(n_p_tiles, n_s_tiles) + [
            _const_spec(meta.shape), _const_spec(xfirst.shape), _const_spec(g.shape),
            _const_spec(win.shape), _const_spec(pw.shape), _const_spec(ps.shape),
            _const_spec(cw.shape), _const_spec(wout.shape),
        ],
        out_specs=pl.BlockSpec((tm, d), lambda i: (i, 0)),
        out_shape=jax.ShapeDtypeStruct((plan["np_rows"], d), F32),
        scratch_shapes=[pltpu.VMEM((xe_rows, d), F32)],
        compiler_params=pltpu.CompilerParams(
            dimension_semantics=("arbitrary",), vmem_limit_bytes=VMEM_LIMIT),
        name="even_mixer",
    )(xp, xp, xp, xs, xs, xs, meta, xfirst, g, win, pw, ps, cw, wout)


def _ffn_kernel(x_ref, g_ref, wg_ref, wu_ref, wd_ref, out_ref):
    x = x_ref[...]
    xn = _rms(x, g_ref[...]).astype(BF16)
    act = (_silu(_dot(xn, wg_ref[...])) * _dot(xn, wu_ref[...])).astype(BF16)
    out_ref[...] = x + _dot(act, wd_ref[...])


def _ffn_layer(h, g, wg, wu, wd, plan):
    tm = plan["tm"]
    d = h.shape[1]
    return pl.pallas_call(
        _ffn_kernel,
        grid=(h.shape[0] // tm,),
        in_specs=[pl.BlockSpec((tm, d), lambda i: (i, 0)), _const_spec(g.shape),
                  _const_spec(wg.shape), _const_spec(wu.shape), _const_spec(wd.shape)],
        out_specs=pl.BlockSpec((tm, d), lambda i: (i, 0)),
        out_shape=jax.ShapeDtypeStruct(h.shape, F32),
        compiler_params=pltpu.CompilerParams(
            dimension_semantics=("arbitrary",), vmem_limit_bytes=VMEM_LIMIT),
        name="dense_swiglu",
    )(h, g, wg, wu, wd)


def _qkv_kernel(x_ref, g_ref, wqk_ref, wvt_ref, qk_ref, vt_ref):
    xn = _rms(x_ref[...], g_ref[...]).astype(BF16)
    qk_ref[...] = _dot(xn, wqk_ref[...]).astype(BF16)
    vt_ref[...] = _dot_nt(wvt_ref[...], xn).astype(BF16)


def _qkv_layer(h, g, wqk, wvt, plan):
    tm = plan["tm"]
    d = h.shape[1]
    return pl.pallas_call(
        _qkv_kernel,
        grid=(h.shape[0] // tm,),
        in_specs=[pl.BlockSpec((tm, d), lambda i: (i, 0)), _const_spec(g.shape),
                  _const_spec(wqk.shape), _const_spec(wvt.shape)],
        out_specs=[pl.BlockSpec((tm, wqk.shape[1]), lambda i: (i, 0)),
                   pl.BlockSpec((wvt.shape[0], tm), lambda i: (0, i))],
        out_shape=[jax.ShapeDtypeStruct((h.shape[0], wqk.shape[1]), BF16),
                   jax.ShapeDtypeStruct((wvt.shape[0], h.shape[0]), BF16)],
        compiler_params=pltpu.CompilerParams(
            dimension_semantics=("arbitrary",), vmem_limit_bytes=VMEM_LIMIT),
        name="qkv_proj",
    )(h, g, wqk, wvt)


def _attn_kernel(q_ref, kc_ref, kp_ref, kn_ref, vc_ref, vp_ref, vn_ref, qm_ref, km_ref, vm_ref,
                 vmt_ref, tab_ref, mb_ref, o_ref, om_ref, kx_ref, vx_ref, s_ref, *, cr, chunks_p,
                 n_p_chunks, chunks_s, rows_p, rows_s):
    ci = pl.program_id(1)
    is_p = ci < n_p_chunks
    cin = jnp.where(is_p, ci % chunks_p, (ci - n_p_chunks) % chunks_s)
    rows = jnp.where(is_p, rows_p, rows_s)
    halo = HALO_ROWS * GRID_W
    ct = cr * GRID_W

    kx_ref[0:halo, :] = kp_ref[...]
    kx_ref[halo:halo + ct, :] = kc_ref[...]
    kx_ref[halo + ct:, :] = kn_ref[...]
    ones_rows = V_ROWS - HEAD_DIM
    vmt_aug = []
    for hh in range(HEADS_PER_BLOCK):
        hs = slice(hh * HEAD_DIM, (hh + 1) * HEAD_DIM)
        vx_ref[hh, 0:HEAD_DIM, 0:halo] = vp_ref[hs, :]
        vx_ref[hh, 0:HEAD_DIM, halo:halo + ct] = vc_ref[hs, :]
        vx_ref[hh, 0:HEAD_DIM, halo + ct:] = vn_ref[hs, :]
        vx_ref[hh, HEAD_DIM:, :] = jnp.ones((ones_rows, ct + 2 * halo), BF16)
        vmt_aug.append(jnp.concatenate(
            [vmt_ref[hs, :], jnp.ones((ones_rows, N_META), BF16)], axis=0))

    lane = lax.broadcasted_iota(jnp.int32, (1, LANES), 1)
    head_masks = [(lane // HEAD_DIM) == hh for hh in range(HEADS_PER_BLOCK)]
    km = km_ref[...]
    r0c = cin * cr

    n_blocks = cr // Q_ROWS

    def scores(bi, slot):
        r0 = r0c + bi * Q_ROWS
        edge = jnp.where(r0 == 0, 0, jnp.where(r0 == rows - Q_ROWS, 2, 1))
        off = pl.multiple_of(bi * NQ, NQ)
        q = q_ref[pl.ds(off, NQ), :]
        qq = jnp.concatenate(
            [jnp.where(head_masks[hh], q, jnp.zeros_like(q)) for hh in range(HEADS_PER_BLOCK)],
            axis=0)
        s_ref[slot, 0:NK, :] = _dot_nt(kx_ref[pl.ds(off, NK), :], qq) + tab_ref[edge, 0:NK, :]
        s_ref[slot, NK:NK + N_META, :] = _dot_nt(km, qq) + tab_ref[edge, NK:NK + N_META, :]

    def softmax_pv(bi, slot):
        off = pl.multiple_of(bi * NQ, NQ)
        s = s_ref[slot]
        pb = jnp.exp2(s - jnp.max(s, axis=0, keepdims=True)).astype(BF16)
        outs = []
        for hh in range(HEADS_PER_BLOCK):
            qs = slice(hh * NQ, (hh + 1) * NQ)
            oa = (_dot(vx_ref[hh, :, pl.ds(off, NK)], pb[0:NK, qs])
                  + _dot(vmt_aug[hh], pb[NK:NK + N_META, qs]))
            outs.append(oa[0:HEAD_DIM] / oa[HEAD_DIM:HEAD_DIM + 1])
        o_ref[pl.ds(off, NQ), :] = jnp.concatenate(outs, axis=0).T.astype(BF16)

    scores(0, 0)
    scores(1, 1)

    def quad_body(b4, carry):
        b = 4 * b4
        for k in range(4):
            scores(jnp.minimum(b + k + 2, n_blocks - 1), (k + 2) % 4)
            softmax_pv(b + k, k)
        return carry

    lax.fori_loop(0, n_blocks // 4, quad_body, 0)

    qm = qm_ref[...]
    vm = vm_ref[...]
    out = None
    for hh in range(HEADS_PER_BLOCK):
        qh = jnp.where(head_masks[hh], qm, jnp.zeros_like(qm))
        sm = _dot_nt(qh, km) + mb_ref[hh]
        pm = jnp.exp2(sm - jnp.max(sm, axis=-1, keepdims=True))
        o = _dot(pm.astype(BF16), vm) / jnp.sum(pm, axis=-1, keepdims=True)
        out = o if out is None else jnp.where(head_masks[hh], o, out)
    om_ref[...] = out.astype(BF16)


def _attn_layer(qk, vt, vm, vmt, bias_tab, meta_bias, plan):
    cr = plan["cr"]
    ct = cr * GRID_W
    halo = HALO_ROWS * GRID_W
    n_main = plan["n_main"]
    n_chunks = n_main // ct
    n_hb = NA_HEADS // HEADS_PER_BLOCK
    hb = ct // halo
    last_halo = n_main // halo - 1
    meta_blk0 = n_main // N_META
    n_p_chunks = plan["n_p"] // ct
    chunks_p = plan["t_p"] // ct
    chunks_s = plan["t_s"] // ct
    b_p = plan["n_p"] // plan["t_p"]

    def seq_of(ci):
        return jnp.where(ci < n_p_chunks, ci // chunks_p, b_p + (ci - n_p_chunks) // chunks_s)

    def cur(off):
        return pl.BlockSpec((ct, LANES), lambda h, c: (c, off + h))

    def prev(off):
        return pl.BlockSpec((halo, LANES), lambda h, c: (jnp.maximum(c * hb - 1, 0), off + h))

    def nxt(off):
        return pl.BlockSpec((halo, LANES),
                            lambda h, c: (jnp.minimum((c + 1) * hb, last_halo), off + h))

    def meta(off):
        return pl.BlockSpec((N_META, LANES), lambda h, c: (meta_blk0 + seq_of(c), off + h))

    def vt_cur():
        return pl.BlockSpec((LANES, ct), lambda h, c: (h, c))

    def vt_prev():
        return pl.BlockSpec((LANES, halo), lambda h, c: (h, jnp.maximum(c * hb - 1, 0)))

    def vt_next():
        return pl.BlockSpec((LANES, halo), lambda h, c: (h, jnp.minimum((c + 1) * hb, last_halo)))

    kern = functools.partial(
        _attn_kernel, cr=cr, chunks_p=chunks_p, n_p_chunks=n_p_chunks, chunks_s=chunks_s,
        rows_p=plan["t_p"] // GRID_W, rows_s=plan["t_s"] // GRID_W)
    width = NA_HEADS * HEAD_DIM
    return pl.pallas_call(
        kern,
        grid=(n_hb, n_chunks),
        in_specs=[
            cur(0), cur(n_hb), prev(n_hb), nxt(n_hb), vt_cur(), vt_prev(), vt_next(),
            meta(0), meta(n_hb),
            pl.BlockSpec((N_META, LANES), lambda h, c: (seq_of(c), h)),
            pl.BlockSpec((None, LANES, N_META), lambda h, c: (seq_of(c), h, 0)),
            pl.BlockSpec((None,) + bias_tab.shape[1:], lambda h, c: (h, 0, 0, 0)),
            pl.BlockSpec((None, HEADS_PER_BLOCK, 1, N_META), lambda h, c: (h, 0, 0, 0)),
        ],
        out_specs=[
            pl.BlockSpec((ct, LANES), lambda h, c: (c, h)),
            pl.BlockSpec((N_META, LANES), lambda h, c: (seq_of(c), h)),
        ],
        out_shape=[jax.ShapeDtypeStruct((n_main, width), BF16),
                   jax.ShapeDtypeStruct((plan["n_seq"] * N_META, width), BF16)],
        scratch_shapes=[pltpu.VMEM((ct + 2 * halo, LANES), BF16),
                        pltpu.VMEM((HEADS_PER_BLOCK, V_ROWS, ct + 2 * halo), BF16),
                        pltpu.VMEM((4, NK + N_META, HEADS_PER_BLOCK * NQ), F32)],
        compiler_params=pltpu.CompilerParams(
            dimension_semantics=("arbitrary", "arbitrary"), vmem_limit_bytes=VMEM_LIMIT),
        name="neighbourhood_attention",
    )(qk, qk, qk, qk, vt, vt, vt, qk, qk, vm, vmt, bias_tab, meta_bias)


def _attn_bias_table(rel_bias, meta_bias):
    n_heads, n_row_off, n_col_off = rel_bias.shape
    half = n_col_off // 2
    ring = jnp.concatenate([
        rel_bias[:, :, half::-1],
        jnp.zeros((n_heads, n_row_off, 2 * GRID_W - n_col_off), F32),
        rel_bias[:, :, :half:-1]], axis=-1)
    toe = jnp.tile(ring, (1, 1, GRID_W))[:, :, :GRID_W * (2 * GRID_W - 1)]
    toe = toe.reshape(n_heads, n_row_off, GRID_W, 2 * GRID_W - 1)[..., :GRID_W]
    qc = np.arange(GRID_W)[None, :]
    kc = np.arange(GRID_W)[:, None]
    cstart = np.clip(qc - NA_WIN_COLS // 2, 0, GRID_W - NA_WIN_COLS)
    col_ok = (kc >= cstart) & (kc < cstart + NA_WIN_COLS)
    toe = jnp.where(col_ok[None, None], toe, -1e30)
    n_hb = n_heads // HEADS_PER_BLOCK
    toe = toe.reshape(n_hb, HEADS_PER_BLOCK, n_row_off, GRID_W, GRID_W)
    per_a = jnp.stack([toe[:, :, HALO_ROWS - 1 - a:HALO_ROWS - 1 - a + WIN_ROWS]
                       for a in range(Q_ROWS)], axis=2)
    per_a = jnp.transpose(per_a, (0, 3, 4, 1, 2, 5))
    a = np.arange(Q_ROWS)[None, :]
    b = np.arange(WIN_ROWS)[:, None]
    row_ok = np.stack([(b >= HALO_ROWS) & (a >= 0),
                       (b - a >= 0) & (b - a < NA_WIN_ROWS),
                       (b < NA_WIN_ROWS) & (a >= 0)])
    tab = jnp.where(row_ok[None, :, :, None, None, :, None], per_a[:, None], -1e30)
    tab = tab.reshape(n_hb, 3, NK, HEADS_PER_BLOCK * NQ)
    meta = jnp.transpose(meta_bias.reshape(n_hb, HEADS_PER_BLOCK, N_META), (0, 2, 1))
    meta = jnp.broadcast_to(meta[:, None, :, :, None], (n_hb, 3, N_META, HEADS_PER_BLOCK, NQ))
    meta = meta.reshape(n_hb, 3, N_META, HEADS_PER_BLOCK * NQ)
    return jnp.concatenate([tab, meta], axis=2).astype(F32)


def _route_kernel(o_ref, om_ref, h_ref, wout_ref, g_ref, wrt_ref, brc_ref, triu_ref,
                  h_out, xn_out, route_out, cnt_out, *, n_tiles):
    i = pl.program_id(0)

    def body(o):
        h = h_ref[...] + _dot(o, wout_ref[...])
        h_out[...] = h
        xn = _rms(h, g_ref[...]).astype(BF16)
        xn_out[...] = xn
        logits = _dot_nt(wrt_ref[...], xn) + brc_ref[...]
        tm = logits.shape[1]
        sub = lax.broadcasted_iota(jnp.int32, (N_EXPERTS, tm), 0)
        m1 = jnp.max(logits, axis=0, keepdims=True)
        i1 = jnp.min(jnp.where(logits == m1, sub, N_EXPERTS), axis=0, keepdims=True)
        oh1 = sub == i1
        rest = jnp.where(oh1, -jnp.inf, logits)
        m2 = jnp.max(rest, axis=0, keepdims=True)
        i2 = jnp.min(jnp.where(rest == m2, sub, N_EXPERTS), axis=0, keepdims=True)
        oh2 = sub == i2
        t = jnp.exp(m2 - m1)
        g1 = 1.0 / (1.0 + t)
        g2 = t * g1
        oh = jnp.where(oh1 | oh2, 1.0, 0.0)
        cum = _dot(oh.astype(BF16), triu_ref[...])
        lr1 = jnp.sum(jnp.where(oh1, cum, 0.0), axis=0, keepdims=True)
        lr2 = jnp.sum(jnp.where(oh2, cum, 0.0), axis=0, keepdims=True)
        route = jnp.zeros((N_EXPERTS, tm), F32)
        for k, val in enumerate([i1.astype(F32), i2.astype(F32), lr1, lr2, g1, g2]):
            route = jnp.where(sub == k, val, route)
        route_out[...] = route
        cnt_out[0] = jnp.broadcast_to(jnp.sum(oh, axis=1, keepdims=True), (N_EXPERTS, LANES))

    @pl.when(i < n_tiles)
    def _():
        body(o_ref[...])

    @pl.when(i == n_tiles)
    def _():
        body(om_ref[...])


def _route_layer(o_main, o_meta, h, wout, g, wrt, brc, plan):
    tm = plan["tm"]
    d = h.shape[1]
    n_tiles = plan["n_main"] // tm
    triu = (jnp.arange(tm)[:, None] < jnp.arange(tm)[None, :]).astype(BF16)
    kern = functools.partial(_route_kernel, n_tiles=n_tiles)
    rows = (n_tiles + 1) * tm
    return pl.pallas_call(
        kern,
        grid=(n_tiles + 1,),
        in_specs=[
            pl.BlockSpec((tm, d), lambda i: (jnp.minimum(i, n_tiles - 1), 0)),
            _const_spec(o_meta.shape),
            pl.BlockSpec((tm, d), lambda i: (i, 0)),
            _const_spec(wout.shape), _const_spec(g.shape), _const_spec(wrt.shape),
            _const_spec(brc.shape), _const_spec(triu.shape),
        ],
        out_specs=[
            pl.BlockSpec((tm, d), lambda i: (i, 0)),
            pl.BlockSpec((tm, d), lambda i: (i, 0)),
            pl.BlockSpec((N_EXPERTS, tm), lambda i: (0, i)),
            pl.BlockSpec((1, N_EXPERTS, LANES), lambda i: (i, 0, 0)),
        ],
        out_shape=[jax.ShapeDtypeStruct((rows, d), F32),
                   jax.ShapeDtypeStruct((rows, d), BF16),
                   jax.ShapeDtypeStruct((N_EXPERTS, rows), F32),
                   jax.ShapeDtypeStruct((n_tiles + 1, N_EXPERTS, LANES), F32)],
        compiler_params=pltpu.CompilerParams(
            dimension_semantics=("arbitrary",), vmem_limit_bytes=VMEM_LIMIT),
        name="outproj_router",
    )(o_main, o_meta, h, wout, g, wrt, brc, triu)


def _for_each_run(tab_ref, tm, fn):
    for e in range(N_EXPERTS):
        local, length, sorted_row = tab_ref[0, 0, e], tab_ref[0, 1, e], tab_ref[0, 2, e]
        n = tm
        while n >= SEG_ALIGN:
            @pl.when((length & n) != 0)
            def _(n=n):
                done = (length // (2 * n)) * (2 * n)
                fn(pl.multiple_of(local + done, SEG_ALIGN),
                   pl.multiple_of(sorted_row + done, SEG_ALIGN), n)
            n //= 2


def _dispatch_kernel(tab_ref, tabp_ref, tail_ref, slot_ref, route_ref, x_ref, xs_ref, stage_ref,
                     zero_ref, sems, zsem, *, tm, te):
    i = pl.program_id(0)
    d = x_ref.shape[1]
    slot = i % 2

    @pl.when(i == 0)
    def _():
        zero_ref[...] = jnp.zeros(zero_ref.shape, F32)

        def zero_tile(start):
            cp = pltpu.make_async_copy(
                zero_ref, xs_ref.at[pl.ds(pl.multiple_of(start, te), te)], zsem)
            cp.start()
            cp.wait()

        for e in range(N_EXPERTS):
            @pl.when(tail_ref[e] >= 0)
            def _():
                zero_tile(tail_ref[e])

        def zero_unused(u, carry):
            zero_tile(u * te)
            return carry

        lax.fori_loop(tail_ref[N_EXPERTS], xs_ref.shape[0] // te, zero_unused, 0)

    s_rows = stage_ref.shape[1]
    slot_iota = lax.broadcasted_iota(jnp.int32, (s_rows, tm), 0)
    first = slot_iota == slot_ref[0:1, :]
    second = slot_iota == slot_ref[1:2, :]
    perm = jnp.where(first | second, 1.0, 0.0).astype(BF16)
    stage_ref[slot, :, 0:d] = _dot(perm, x_ref[...])
    gate = jnp.sum(jnp.where(first, route_ref[4:5, :], 0.0)
                   + jnp.where(second, route_ref[5:6, :], 0.0), axis=1, keepdims=True)
    stage_ref[slot, :, d:] = jnp.broadcast_to(gate, (s_rows, LANES))

    def piece(buf, local_row, sorted_row, n):
        return pltpu.make_async_copy(
            stage_ref.at[buf, pl.ds(local_row, n)], xs_ref.at[pl.ds(sorted_row, n)],
            sems.at[buf])

    _for_each_run(tab_ref, tm, lambda lr, sr, n: piece(slot, lr, sr, n).start())

    @pl.when(i > 0)
    def _():
        _for_each_run(tabp_ref, tm, lambda lr, sr, n: piece(1 - slot, lr, sr, n).wait())

    @pl.when(i == pl.num_programs(0) - 1)
    def _():
        _for_each_run(tab_ref, tm, lambda lr, sr, n: piece(slot, lr, sr, n).wait())


def _dispatch(xn, slot_t, route_t, tab, tails, plan):
    tm = plan["tm"]
    te = plan["te"]
    d = xn.shape[1]
    kern = functools.partial(_dispatch_kernel, tm=tm, te=te)
    return pl.pallas_call(
        kern,
        grid=(xn.shape[0] // tm,),
        in_specs=[
            pl.BlockSpec((1, 3, N_EXPERTS), lambda i: (i, 0, 0), memory_space=pltpu.SMEM),
            pl.BlockSpec((1, 3, N_EXPERTS), lambda i: (jnp.maximum(i - 1, 0), 0, 0),
                         memory_space=pltpu.SMEM),
            pl.BlockSpec(memory_space=pltpu.SMEM),
            pl.BlockSpec((2, tm), lambda i: (0, i)),
            pl.BlockSpec((N_EXPERTS, tm), lambda i: (0, i)),
            pl.BlockSpec((tm, d), lambda i: (i, 0)),
        ],
        out_specs=pl.BlockSpec(memory_space=pl.ANY),
        out_shape=jax.ShapeDtypeStruct((plan["rows_sorted"], d + LANES), F32),
        scratch_shapes=[pltpu.VMEM((2, plan["s_rows"], d + LANES), F32),
                        pltpu.VMEM((te, d + LANES), F32),
                        pltpu.SemaphoreType.DMA((2,)), pltpu.SemaphoreType.DMA(())],
        compiler_params=pltpu.CompilerParams(
            dimension_semantics=("arbitrary",), vmem_limit_bytes=VMEM_LIMIT),
        name="moe_dispatch",
    )(tab, tab, tails, slot_t, route_t, xn)


def _expert_kernel(te_ref, nu_ref, x_ref, wg_ref, wu_ref, wd_ref, out_ref, *, sub):
    j = pl.program_id(0)
    c = pl.program_id(1)

    @pl.when(c == 0)
    def _():
        out_ref[...] = jnp.zeros(out_ref.shape, F32)

    @pl.when(j < nu_ref[0])
    def _():
        d = out_ref.shape[1]
        xb = x_ref[:, 0:d].astype(BF16)
        fc = wg_ref.shape[1]
        y = None
        for lo in range(0, fc, sub):
            hi = min(lo + sub, fc)
            act = (_silu(_dot(xb, wg_ref[:, lo:hi])) * _dot(xb, wu_ref[:, lo:hi])).astype(BF16)
            part = _dot(act, wd_ref[lo:hi, :])
            y = part if y is None else y + part
        out_ref[...] += x_ref[:, d:d + 1] * y


def _experts(xs, tile_expert, n_used, wg, wu, wd, plan):
    te = plan["te"]
    fc = plan["fc"]
    d = xs.shape[1] - LANES
    n_tiles = xs.shape[0] // te
    n_c = wg.shape[2] // fc

    def row(j, c, te_ref, nu_ref):
        return (jnp.minimum(j, nu_ref[0] - 1), 0)

    def chunk(j, c, nu_ref):
        return jnp.where(j < nu_ref[0], c, n_c - 1)

    def w_in(j, c, te_ref, nu_ref):
        return (te_ref[jnp.minimum(j, nu_ref[0] - 1)], 0, chunk(j, c, nu_ref))

    def w_out(j, c, te_ref, nu_ref):
        return (te_ref[jnp.minimum(j, nu_ref[0] - 1)], chunk(j, c, nu_ref), 0)

    grid_spec = pltpu.PrefetchScalarGridSpec(
        num_scalar_prefetch=2,
        grid=(n_tiles, n_c),
        in_specs=[
            pl.BlockSpec((te, d + LANES), row),
            pl.BlockSpec((None, d, fc), w_in),
            pl.BlockSpec((None, d, fc), w_in),
            pl.BlockSpec((None, fc, d), w_out),
        ],
        out_specs=pl.BlockSpec((te, d), lambda j, c, te_ref, nu_ref: (j, 0)),
    )
    return pl.pallas_call(
        functools.partial(_expert_kernel, sub=plan["fsub"]),
        grid_spec=grid_spec,
        out_shape=jax.ShapeDtypeStruct((xs.shape[0], d), F32),
        compiler_params=pltpu.CompilerParams(
            dimension_semantics=("arbitrary", "arbitrary"), vmem_limit_bytes=VMEM_LIMIT),
        name="moe_experts",
    )(tile_expert, n_used, xs, wg, wu, wd)


def _combine_kernel(tabc_ref, tabn_ref, slot_ref, h_ref, g_ref, ys_ref, outp_ref, outs_ref,
                    y_ref, sems, *, tm, n_tiles, n_p_tiles):
    i = pl.program_id(0)

    def piece(slot, local_row, sorted_row, n):
        return pltpu.make_async_copy(
            ys_ref.at[pl.ds(sorted_row, n)], y_ref.at[slot, pl.ds(local_row, n)], sems.at[slot])

    @pl.when(i == 0)
    def _():
        y_ref[...] = jnp.zeros(y_ref.shape, F32)
        _for_each_run(tabc_ref, tm, lambda lr, sr, n: piece(0, lr, sr, n).start())

    @pl.when(i + 1 < n_tiles)
    def _():
        _for_each_run(tabn_ref, tm, lambda lr, sr, n: piece((i + 1) % 2, lr, sr, n).start())

    slot = i % 2
    _for_each_run(tabc_ref, tm, lambda lr, sr, n: piece(slot, lr, sr, n).wait())

    y = y_ref[slot]
    y_hi = y.astype(BF16)
    y_lo = (y - y_hi.astype(F32)).astype(BF16)
    s_rows = y.shape[0]
    slot_iota = lax.broadcasted_iota(jnp.int32, (tm, s_rows), 1)
    pick = jnp.where((slot_iota == slot_ref[:, 0:1]) | (slot_iota == slot_ref[:, 1:2]),
                     1.0, 0.0).astype(BF16)
    h = h_ref[...] + _dot(pick, y_hi) + _dot(pick, y_lo)
    out = _rms(h, g_ref[...])

    @pl.when(i < n_p_tiles)
    def _():
        outp_ref[...] = out

    @pl.when(i >= n_p_tiles)
    def _():
        outs_ref[...] = out


def _combine(h, slot_c, tab, ys, g, plan):
    tm = plan["tm"]
    d = h.shape[1]
    n_tiles = plan["n_main"] // tm
    n_p_tiles = plan["n_p"] // tm
    kern = functools.partial(_combine_kernel, tm=tm, n_tiles=n_tiles, n_p_tiles=n_p_tiles)
    return pl.pallas_call(
        kern,
        grid=(n_tiles,),
        in_specs=[
            pl.BlockSpec((1, 3, N_EXPERTS), lambda i: (i, 0, 0), memory_space=pltpu.SMEM),
            pl.BlockSpec((1, 3, N_EXPERTS), lambda i: (jnp.minimum(i + 1, n_tiles - 1), 0, 0),
                         memory_space=pltpu.SMEM),
            pl.BlockSpec((tm, 2), lambda i: (i, 0)),
            pl.BlockSpec((tm, d), lambda i: (i, 0)),
            _const_spec(g.shape),
            pl.BlockSpec(memory_space=pl.ANY),
        ],
        out_specs=[
            pl.BlockSpec((tm, d), lambda i: (jnp.minimum(i, n_p_tiles - 1), 0)),
            pl.BlockSpec((tm, d), lambda i: (jnp.maximum(i - n_p_tiles, 0), 0)),
        ],
        out_shape=[jax.ShapeDtypeStruct((plan["n_p"], d), F32),
                   jax.ShapeDtypeStruct((plan["n_main"] - plan["n_p"], d), F32)],
        scratch_shapes=[pltpu.VMEM((2, plan["s_rows"], d), F32),
                        pltpu.SemaphoreType.DMA((2,))],
        compiler_params=pltpu.CompilerParams(
            dimension_semantics=("arbitrary",), vmem_limit_bytes=VMEM_LIMIT),
        name="moe_combine_norm",
    )(tab, tab, slot_c, h, g, ys)


def _make_plan(b_p, t_p, b_s, t_s, d_ff_expert):
    tm = 512 if t_p % 512 == 0 and t_s % 512 == 0 else 128
    te = 1024 if tm == 512 else 256
    n_p = b_p * t_p
    n_main = n_p + b_s * t_s
    n_seq = b_p + b_s
    assert n_seq * N_META <= tm and t_p % tm == 0 and t_s % tm == 0
    np_rows = n_main + tm
    rows_p, rows_s = t_p // GRID_W, t_s // GRID_W
    cr = 64 if rows_p % 64 == 0 and rows_s % 64 == 0 else 4 * Q_ROWS
    assert rows_p % cr == 0 and rows_s % cr == 0 and cr % (4 * Q_ROWS) == 0
    fc = 1792 if d_ff_expert % 1792 == 0 else d_ff_expert
    fsub = 2 * MXU_DIM
    run_pad = N_EXPERTS * (SEG_ALIGN - 1)
    s_rows = -(-(2 * tm + run_pad) // SEG_ALIGN) * SEG_ALIGN
    rows_sorted = -(-(2 * np_rows + run_pad * (np_rows // tm)) // te) * te + N_EXPERTS * te
    return dict(tm=tm, te=te, n_p=n_p, n_main=n_main, n_seq=n_seq, np_rows=np_rows, t_p=t_p,
                t_s=t_s, cr=cr, fc=fc, fsub=fsub, s_rows=s_rows, rows_sorted=rows_sorted)


def kernel(x_prompt, x_sample, meta_tokens, norm_mix, norm_ffn, norm_final, even_w_in, pool_w,
           pool_scale, conv_w, even_w_out, ffn_w_gate, ffn_w_up, ffn_w_down, na_w_qkv,
           na_rel_bias, na_meta_bias, na_w_out, moe_w_router, moe_b_router, moe_w_gate,
           moe_w_up, moe_w_down):
    b_p, t_p, d = x_prompt.shape
    b_s, t_s, _ = x_sample.shape
    plan = _make_plan(b_p, t_p, b_s, t_s, moe_w_gate.shape[-1])
    tm, te = plan["tm"], plan["te"]
    n_main, n_seq, np_rows = plan["n_main"], plan["n_seq"], plan["np_rows"]
    n_meta_rows = n_seq * N_META

    xfirst = jnp.concatenate([x_prompt[:, :EXT].reshape(-1, d), x_sample[:, :EXT].reshape(-1, d)])

    row = lambda v: v.reshape(1, -1).astype(F32)
    h = _even_layer(x_prompt.reshape(-1, d), x_sample.reshape(-1, d), meta_tokens, xfirst,
                    row(norm_mix[0]), even_w_in[0].astype(BF16), pool_w[0].astype(BF16),
                    row(pool_scale[0]), conv_w[0], even_w_out[0].astype(BF16), plan)
    h = _ffn_layer(h, row(norm_ffn[0]), ffn_w_gate[0].astype(BF16), ffn_w_up[0].astype(BF16),
                   ffn_w_down[0].astype(BF16), plan)

    width = NA_HEADS * HEAD_DIM
    log2e = float(np.log2(np.e))
    qscale = jnp.concatenate([jnp.full((width,), HEAD_DIM ** -0.5 * log2e, F32),
                              jnp.ones((width,), F32)])
    qk, vt = _qkv_layer(h, row(norm_mix[1]), (na_w_qkv[0][:, :2 * width] * qscale).astype(BF16),
                        na_w_qkv[0][:, 2 * width:].T.astype(BF16), plan)
    vt_meta = vt[:, n_main:n_main + n_meta_rows]
    n_hb = NA_HEADS // HEADS_PER_BLOCK
    o_main, o_meta = _attn_layer(
        qk, vt, vt_meta.T, vt_meta.reshape(width, n_seq, N_META).transpose(1, 0, 2),
        _attn_bias_table(na_rel_bias[0] * log2e, na_meta_bias[0] * log2e),
        (na_meta_bias[0] * log2e).reshape(n_hb, HEADS_PER_BLOCK, 1, N_META), plan)
    o_meta = jnp.concatenate([o_meta, jnp.zeros((tm - n_meta_rows, width), BF16)])

    h, xn, route_t, tile_cnt = _route_layer(
        o_main, o_meta, h, na_w_out[0].astype(BF16), row(norm_ffn[1]),
        moe_w_router[0].T.astype(BF16), moe_b_router[0].reshape(-1, 1).astype(F32), plan)

    cnt = tile_cnt[:, :, 0].astype(jnp.int32)
    seg = (cnt + SEG_ALIGN - 1) // SEG_ALIGN * SEG_ALIGN
    local = jnp.cumsum(seg, axis=1) - seg
    before = jnp.cumsum(seg, axis=0) - seg
    padded = (jnp.sum(seg, axis=0) + te - 1) // te * te
    ends = jnp.cumsum(padded)
    tab = jnp.stack([local, seg, (ends - padded)[None, :] + before], axis=1).astype(jnp.int32)
    experts_t = route_t[0:2].astype(jnp.int32)
    local_tok = jnp.repeat(local, tm, axis=0).T
    pick = experts_t[:, None, :] == jnp.arange(N_EXPERTS)[None, :, None]
    slot_t = jnp.sum(jnp.where(pick, local_tok[None], 0), axis=1) + route_t[2:4].astype(jnp.int32)
    tails = jnp.concatenate([jnp.where(padded > 0, ends - te, -1), ends[-1:] // te])
    tails = tails.astype(jnp.int32)
    n_row_tiles = plan["rows_sorted"] // te
    tile_expert = jnp.minimum(
        jnp.sum(ends[None, :] <= (jnp.arange(n_row_tiles) * te)[:, None], axis=1),
        N_EXPERTS - 1).astype(jnp.int32)
    n_used = (ends[-1:] // te).astype(jnp.int32)

    xs = _dispatch(xn, slot_t, route_t, tab, tails, plan)
    ys = _experts(xs, tile_expert, n_used, moe_w_gate[0].astype(BF16), moe_w_up[0].astype(BF16),
                  moe_w_down[0].astype(BF16), plan)
    y_p, y_s = _combine(h, slot_t.T, tab, ys, row(norm_final), plan)
    return (y_p.reshape(b_p, t_p, d), y_s.reshape(b_s, t_s, d))
```

```python
import functools

import jax
import jax.numpy as jnp
import numpy as np
from jax import lax
from jax.experimental import pallas as pl
from jax.experimental.pallas import tpu as pltpu

F32 = jnp.float32
BF16 = jnp.bfloat16

N_META = 16
GRID_W = 64
RMS_EPS = 1e-6
POOL_WINDOWS = (2, 4, 8, 16)
POOL_GROUP = 128
POOL_WIDTH = 512
CONV_WIDTH = 512
NA_HEADS = 16
HEAD_DIM = 64
NA_WIN_ROWS = 8
NA_WIN_COLS = 16
N_EXPERTS = 8
LANES = 128
MXU_DIM = 256
SEG_ALIGN = 8
HEADS_PER_BLOCK = LANES // HEAD_DIM
HALO_ROWS = NA_WIN_ROWS // 2
Q_ROWS = 4
WIN_ROWS = Q_ROWS + NA_WIN_ROWS
NQ = Q_ROWS * GRID_W
NK = WIN_ROWS * GRID_W
V_ROWS = HEAD_DIM + 16
EXT = 16
VMEM_LIMIT = 56 * 1024 * 1024


def _const_spec(shape):
    nd = len(shape)
    return pl.BlockSpec(shape, lambda *_: (0,) * nd, pipeline_mode=pl.Buffered(1))


def _rms(x, g):
    ms = jnp.mean(x * x, axis=-1, keepdims=True)
    return x * lax.rsqrt(ms + RMS_EPS) * g


def _dot(a, b):
    return jnp.dot(a, b, preferred_element_type=F32)


def _dot_nt(a, b):
    return lax.dot_general(a, b, (((1,), (1,)), ((), ())), preferred_element_type=F32)


def _silu(x):
    return x * jax.nn.sigmoid(x)


def _even_mix(xe, pos, seq_len, g, win_ref, pw_ref, ps, cw, wout_ref):
    n = xe.shape[0]
    xn = _rms(xe, g).astype(BF16)
    z = _dot(xn, win_ref[...])
    a_parts = []
    for gi, k in enumerate(POOL_WINDOWS):
        p = z[:, gi * POOL_GROUP:(gi + 1) * POOL_GROUP]
        s = p
        w = 1
        while w < k:
            s = s + pltpu.roll(s, n - w, 0)
            w *= 2
        left = k // 2
        right = k - 1 - left
        s = pltpu.roll(s, left, 0)
        cnt = jnp.minimum(pos + right + 1, seq_len) - jnp.maximum(pos - left, 0)
        inv = 1.0 / jnp.maximum(cnt, 1).astype(F32)
        pooled = (s * inv - p).astype(BF16)
        a_parts.append(_dot(pooled, pw_ref[gi]))
    a = jnp.concatenate(a_parts, axis=-1) * ps
    gb = z[:, POOL_WIDTH:POOL_WIDTH + CONV_WIDTH]
    gc = z[:, POOL_WIDTH + CONV_WIDTH:POOL_WIDTH + 2 * CONV_WIDTH]
    hc = z[:, POOL_WIDTH + 2 * CONV_WIDTH:]
    u = gc * hc
    conv = pltpu.roll(u, 1, 0) * cw[0:1] + u * cw[1:2] + pltpu.roll(u, n - 1, 0) * cw[2:3]
    c = gb * conv
    return (_dot(a.astype(BF16), wout_ref[0:POOL_WIDTH, :])
            + _dot(c.astype(BF16), wout_ref[POOL_WIDTH:, :]))


def _even_kernel(xp_ref, xpp_ref, xpn_ref, xs_ref, xsp_ref, xsn_ref, meta_ref, first_ref, g_ref,
                 win_ref, pw_ref, ps_ref, cw_ref, wout_ref, out_ref, xe_ref, *, tm, n_tiles,
                 tiles_p, n_p_tiles, tiles_s, t_p, t_s, n_seq):
    i = pl.program_id(0)
    weights = (g_ref[...], win_ref, pw_ref, ps_ref[...], cw_ref[...], wout_ref)
    d = xe_ref.shape[1]

    def token_tile(tile_ref, prev_ref, next_ref, tin, last, t_seq):
        n = tm + 2 * EXT

        @pl.when(tin == 0)
        def _():
            xe_ref[0:EXT, :] = meta_ref[...]

        @pl.when(tin != 0)
        def _():
            xe_ref[0:EXT, :] = prev_ref[...]

        xe_ref[EXT:EXT + tm, :] = tile_ref[...]

        @pl.when(tin == last)
        def _():
            xe_ref[EXT + tm:n, :] = jnp.zeros((EXT, d), F32)

        @pl.when(tin != last)
        def _():
            xe_ref[EXT + tm:n, :] = next_ref[...]

        pos = lax.broadcasted_iota(jnp.int32, (n, 1), 0) + tin * tm
        y = _even_mix(xe_ref[0:n, :], pos, t_seq + N_META, *weights)
        out_ref[...] = xe_ref[EXT:EXT + tm, :] + y[EXT:EXT + tm]

    @pl.when(i < n_p_tiles)
    def _prompt():
        token_tile(xp_ref, xpp_ref, xpn_ref, i % tiles_p, tiles_p - 1, t_p)

    @pl.when((i >= n_p_tiles) & (i < n_tiles))
    def _sample():
        token_tile(xs_ref, xsp_ref, xsn_ref, (i - n_p_tiles) % tiles_s, tiles_s - 1, t_s)

    @pl.when(i == n_tiles)
    def _meta():
        grp = 3 * EXT
        n = grp * n_seq
        for b in range(n_seq):
            xe_ref[grp * b:grp * b + EXT, :] = jnp.zeros((EXT, d), F32)
            xe_ref[grp * b + EXT:grp * b + 2 * EXT, :] = meta_ref[...]
            xe_ref[grp * b + 2 * EXT:grp * (b + 1), :] = first_ref[EXT * b:EXT * (b + 1), :]
        pos = lax.broadcasted_iota(jnp.int32, (n, 1), 0) % grp - EXT
        y = _even_mix(xe_ref[0:n, :], pos, jnp.int32(1 << 30), *weights)
        out_ref[...] = jnp.zeros(out_ref.shape, F32)
        for b in range(n_seq):
            out_ref[EXT * b:EXT * (b + 1), :] = meta_ref[...] + y[grp * b + EXT:grp * b + 2 * EXT]


def _even_layer(xp, xs, meta, xfirst, g, win, pw, ps, cw, wout, plan):
    tm = plan["tm"]
    d = xp.shape[1]
    n_tiles = plan["n_main"] // tm
    n_p_tiles = plan["n_p"] // tm
    n_s_tiles = n_tiles - n_p_tiles
    blk = tm // EXT
    kern = functools.partial(
        _even_kernel, tm=tm, n_tiles=n_tiles, tiles_p=plan["t_p"] // tm, n_p_tiles=n_p_tiles,
        tiles_s=plan["t_s"] // tm, t_p=plan["t_p"], t_s=plan["t_s"], n_seq=plan["n_seq"])
    xe_rows = max(tm + 2 * EXT, 3 * EXT * plan["n_seq"])

    def specs(first_tile, count):
        last = count * blk - 1
        tile = lambda i: jnp.clip(i - first_tile, 0, count - 1)
        return [
            pl.BlockSpec((tm, d), lambda i: (tile(i), 0)),
            pl.BlockSpec((EXT, d), lambda i: (jnp.clip(tile(i) * blk - 1, 0, last), 0)),
            pl.BlockSpec((EXT, d), lambda i: (jnp.clip((tile(i) + 1) * blk, 0, last), 0)),
        ]

    return pl.pallas_call(
        kern,
        grid=(n_tiles + 1,),
        in_specs=specs(0, n_p_tiles) + specs(n_p_tiles, n_s_tiles) + [
            _const_spec(meta.shape), _const_spec(xfirst.shape), _const_spec(g.shape),
            _const_spec(win.shape), _const_spec(pw.shape), _const_spec(ps.shape),
            _const_spec(cw.shape), _const_spec(wout.shape),
        ],
        out_specs=pl.BlockSpec((tm, d), lambda i: (i, 0)),
        out_shape=jax.ShapeDtypeStruct((plan["np_rows"], d), F32),
        scratch_shapes=[pltpu.VMEM((xe_rows, d), F32)],
        compiler_params=pltpu.CompilerParams(
            dimension_semantics=("arbitrary",), vmem_limit_bytes=VMEM_LIMIT),
        name="even_mixer",
    )(xp, xp, xp, xs, xs, xs, meta, xfirst, g, win, pw, ps, cw, wout)


def _ffn_kernel(x_ref, g_ref, wg_ref, wu_ref, wd_ref, out_ref):
    x = x_ref[...]
    xn = _rms(x, g_ref[...]).astype(BF16)
    act = (_silu(_dot(xn, wg_ref[...])) * _dot(xn, wu_ref[...])).astype(BF16)
    out_ref[...] = x + _dot(act, wd_ref[...])


def _ffn_layer(h, g, wg, wu, wd, plan):
    tm = plan["tm"]
    d = h.shape[1]
    return pl.pallas_call(
        _ffn_kernel,
        grid=(h.shape[0] // tm,),
        in_specs=[pl.BlockSpec((tm, d), lambda i: (i, 0)), _const_spec(g.shape),
                  _const_spec(wg.shape), _const_spec(wu.shape), _const_spec(wd.shape)],
        out_specs=pl.BlockSpec((tm, d), lambda i: (i, 0)),
        out_shape=jax.ShapeDtypeStruct(h.shape, F32),
        compiler_params=pltpu.CompilerParams(
            dimension_semantics=("arbitrary",), vmem_limit_bytes=VMEM_LIMIT),
        name="dense_swiglu",
    )(h, g, wg, wu, wd)


def _qkv_kernel(x_ref, g_ref, wqk_ref, wvt_ref, qk_ref, vt_ref):
    xn = _rms(x_ref[...], g_ref[...]).astype(BF16)
    qk_ref[...] = _dot(xn, wqk_ref[...]).astype(BF16)
    vt_ref[...] = _dot_nt(wvt_ref[...], xn).astype(BF16)


def _qkv_layer(h, g, wqk, wvt, plan):
    tm = plan["tm"]
    d = h.shape[1]
    return pl.pallas_call(
        _qkv_kernel,
        grid=(h.shape[0] // tm,),
        in_specs=[pl.BlockSpec((tm, d), lambda i: (i, 0)), _const_spec(g.shape),
                  _const_spec(wqk.shape), _const_spec(wvt.shape)],
        out_specs=[pl.BlockSpec((tm, wqk.shape[1]), lambda i: (i, 0)),
                   pl.BlockSpec((wvt.shape[0], tm), lambda i: (0, i))],
        out_shape=[jax.ShapeDtypeStruct((h.shape[0], wqk.shape[1]), BF16),
                   jax.ShapeDtypeStruct((wvt.shape[0], h.shape[0]), BF16)],
        compiler_params=pltpu.CompilerParams(
            dimension_semantics=("arbitrary",), vmem_limit_bytes=VMEM_LIMIT),
        name="qkv_proj",
    )(h, g, wqk, wvt)


def _attn_kernel(q_ref, kc_ref, kp_ref, kn_ref, vc_ref, vp_ref, vn_ref, qm_ref, km_ref, vm_ref,
                 vmt_ref, tab_ref, mb_ref, o_ref, om_ref, kx_ref, vx_ref, s_ref, *, cr, chunks_p,
                 n_p_chunks, chunks_s, rows_p, rows_s):
    ci = pl.program_id(1)
    is_p = ci < n_p_chunks
    cin = jnp.where(is_p, ci % chunks_p, (ci - n_p_chunks) % chunks_s)
    rows = jnp.where(is_p, rows_p, rows_s)
    halo = HALO_ROWS * GRID_W
    ct = cr * GRID_W

    kx_ref[0:halo, :] = kp_ref[...]
    kx_ref[halo:halo + ct, :] = kc_ref[...]
    kx_ref[halo + ct:, :] = kn_ref[...]
    ones_rows = V_ROWS - HEAD_DIM
    vmt_aug = []
    for hh in range(HEADS_PER_BLOCK):
        hs = slice(hh * HEAD_DIM, (hh + 1) * HEAD_DIM)
        vx_ref[hh, 0:HEAD_DIM, 0:halo] = vp_ref[hs, :]
        vx_ref[hh, 0:HEAD_DIM, halo:halo + ct] = vc_ref[hs, :]
        vx_ref[hh, 0:HEAD_DIM, halo + ct:] = vn_ref[hs, :]
        vx_ref[hh, HEAD_DIM:, :] = jnp.ones((ones_rows, ct + 2 * halo), BF16)
        vmt_aug.append(jnp.concatenate(
            [vmt_ref[hs, :], jnp.ones((ones_rows, N_META), BF16)], axis=0))

    lane = lax.broadcasted_iota(jnp.int32, (1, LANES), 1)
    head_masks = [(lane // HEAD_DIM) == hh for hh in range(HEADS_PER_BLOCK)]
    km = km_ref[...]
    r0c = cin * cr

    n_blocks = cr // Q_ROWS

    def scores(bi, slot):
        r0 = r0c + bi * Q_ROWS
        edge = jnp.where(r0 == 0, 0, jnp.where(r0 == rows - Q_ROWS, 2, 1))
        off = pl.multiple_of(bi * NQ, NQ)
        q = q_ref[pl.ds(off, NQ), :]
        qq = jnp.concatenate(
            [jnp.where(head_masks[hh], q, jnp.zeros_like(q)) for hh in range(HEADS_PER_BLOCK)],
            axis=0)
        nq2 = HEADS_PER_BLOCK * NQ
        s_ref[slot, 0:NK, 0:nq2] = (_dot_nt(kx_ref[pl.ds(off, NK), :], qq)
                                    + tab_ref[edge, 0:NK, :])
        s_ref[slot, NK:NK + N_META, 0:nq2] = (_dot_nt(km, qq)
                                              + tab_ref[edge, NK:NK + N_META, :])

    def softmax_pv(bi, slot):
        off = pl.multiple_of(bi * NQ, NQ)
        s = s_ref[slot, :, 0:HEADS_PER_BLOCK * NQ]
        pb = jnp.exp2(s - jnp.max(s, axis=0, keepdims=True)).astype(BF16)
        outs = []
        for hh in range(HEADS_PER_BLOCK):
            qs = slice(hh * NQ, (hh + 1) * NQ)
            oa = (_dot(vx_ref[hh, :, pl.ds(off, NK)], pb[0:NK, qs])
                  + _dot(vmt_aug[hh], pb[NK:NK + N_META, qs]))
            outs.append(oa[0:HEAD_DIM] / oa[HEAD_DIM:HEAD_DIM + 1])
        o_ref[pl.ds(off, NQ), :] = jnp.concatenate(outs, axis=0).T.astype(BF16)

    scores(0, 0)
    scores(1, 1)

    def quad_body(b4, carry):
        b = 4 * b4
        for k in range(4):
            scores(jnp.minimum(b + k + 2, n_blocks - 1), (k + 2) % 4)
            softmax_pv(b + k, k)
        return carry

    lax.fori_loop(0, n_blocks // 4, quad_body, 0)

    qm = qm_ref[...]
    vm = vm_ref[...]
    out = None
    for hh in range(HEADS_PER_BLOCK):
        qh = jnp.where(head_masks[hh], qm, jnp.zeros_like(qm))
        sm = _dot_nt(qh, km) + mb_ref[hh]
        pm = jnp.exp2(sm - jnp.max(sm, axis=-1, keepdims=True))
        o = _dot(pm.astype(BF16), vm) / jnp.sum(pm, axis=-1, keepdims=True)
        out = o if out is None else jnp.where(head_masks[hh], o, out)
    om_ref[...] = out.astype(BF16)


def _attn_layer(qk, vt, vm, vmt, bias_tab, meta_bias, plan):
    cr = plan["cr"]
    ct = cr * GRID_W
    halo = HALO_ROWS * GRID_W
    n_main = plan["n_main"]
    n_chunks = n_main // ct
    n_hb = NA_HEADS // HEADS_PER_BLOCK
    hb = ct // halo
    last_halo = n_main // halo - 1
    meta_blk0 = n_main // N_META
    n_p_chunks = plan["n_p"] // ct
    chunks_p = plan["t_p"] // ct
    chunks_s = plan["t_s"] // ct
    b_p = plan["n_p"] // plan["t_p"]

    def seq_of(ci):
        return jnp.where(ci < n_p_chunks, ci // chunks_p, b_p + (ci - n_p_chunks) // chunks_s)

    def cur(off):
        return pl.BlockSpec((ct, LANES), lambda h, c: (c, off + h))

    def prev(off):
        return pl.BlockSpec((halo, LANES), lambda h, c: (jnp.maximum(c * hb - 1, 0), off + h))

    def nxt(off):
        return pl.BlockSpec((halo, LANES),
                            lambda h, c: (jnp.minimum((c + 1) * hb, last_halo), off + h))

    def meta(off):
        return pl.BlockSpec((N_META, LANES), lambda h, c: (meta_blk0 + seq_of(c), off + h))

    def vt_cur():
        return pl.BlockSpec((LANES, ct), lambda h, c: (h, c))

    def vt_prev():
        return pl.BlockSpec((LANES, halo), lambda h, c: (h, jnp.maximum(c * hb - 1, 0)))

    def vt_next():
        return pl.BlockSpec((LANES, halo), lambda h, c: (h, jnp.minimum((c + 1) * hb, last_halo)))

    kern = functools.partial(
        _attn_kernel, cr=cr, chunks_p=chunks_p, n_p_chunks=n_p_chunks, chunks_s=chunks_s,
        rows_p=plan["t_p"] // GRID_W, rows_s=plan["t_s"] // GRID_W)
    width = NA_HEADS * HEAD_DIM
    return pl.pallas_call(
        kern,
        grid=(n_hb, n_chunks),
        in_specs=[
            cur(0), cur(n_hb), prev(n_hb), nxt(n_hb), vt_cur(), vt_prev(), vt_next(),
            meta(0), meta(n_hb),
            pl.BlockSpec((N_META, LANES), lambda h, c: (seq_of(c), h)),
            pl.BlockSpec((None, LANES, N_META), lambda h, c: (seq_of(c), h, 0)),
            pl.BlockSpec((None,) + bias_tab.shape[1:], lambda h, c: (h, 0, 0, 0)),
            pl.BlockSpec((None, HEADS_PER_BLOCK, 1, N_META), lambda h, c: (h, 0, 0, 0)),
        ],
        out_specs=[
            pl.BlockSpec((ct, LANES), lambda h, c: (c, h)),
            pl.BlockSpec((N_META, LANES), lambda h, c: (seq_of(c), h)),
        ],
        out_shape=[jax.ShapeDtypeStruct((n_main, width), BF16),
                   jax.ShapeDtypeStruct((plan["n_seq"] * N_META, width), BF16)],
        scratch_shapes=[pltpu.VMEM((ct + 2 * halo, LANES), BF16),
                        pltpu.VMEM((HEADS_PER_BLOCK, V_ROWS, ct + 2 * halo), BF16),
                        pltpu.VMEM((4, NK + N_META, HEADS_PER_BLOCK * NQ + LANES), F32)],
        compiler_params=pltpu.CompilerParams(
            dimension_semantics=("arbitrary", "arbitrary"), vmem_limit_bytes=VMEM_LIMIT),
        name="neighbourhood_attention",
    )(qk, qk, qk, qk, vt, vt, vt, qk, qk, vm, vmt, bias_tab, meta_bias)


def _attn_bias_table(rel_bias, meta_bias):
    n_heads, n_row_off, n_col_off = rel_bias.shape
    half = n_col_off // 2
    ring = jnp.concatenate([
        rel_bias[:, :, half::-1],
        jnp.zeros((n_heads, n_row_off, 2 * GRID_W - n_col_off), F32),
        rel_bias[:, :, :half:-1]], axis=-1)
    toe = jnp.tile(ring, (1, 1, GRID_W))[:, :, :GRID_W * (2 * GRID_W - 1)]
    toe = toe.reshape(n_heads, n_row_off, GRID_W, 2 * GRID_W - 1)[..., :GRID_W]
    qc = np.arange(GRID_W)[None, :]
    kc = np.arange(GRID_W)[:, None]
    cstart = np.clip(qc - NA_WIN_COLS // 2, 0, GRID_W - NA_WIN_COLS)
    col_ok = (kc >= cstart) & (kc < cstart + NA_WIN_COLS)
    toe = jnp.where(col_ok[None, None], toe, -1e30)
    n_hb = n_heads // HEADS_PER_BLOCK
    toe = toe.reshape(n_hb, HEADS_PER_BLOCK, n_row_off, GRID_W, GRID_W)
    per_a = jnp.stack([toe[:, :, HALO_ROWS - 1 - a:HALO_ROWS - 1 - a + WIN_ROWS]
                       for a in range(Q_ROWS)], axis=2)
    per_a = jnp.transpose(per_a, (0, 3, 4, 1, 2, 5))
    a = np.arange(Q_ROWS)[None, :]
    b = np.arange(WIN_ROWS)[:, None]
    row_ok = np.stack([(b >= HALO_ROWS) & (a >= 0),
                       (b - a >= 0) & (b - a < NA_WIN_ROWS),
                       (b < NA_WIN_ROWS) & (a >= 0)])
    tab = jnp.where(row_ok[None, :, :, None, None, :, None], per_a[:, None], -1e30)
    tab = tab.reshape(n_hb, 3, NK, HEADS_PER_BLOCK * NQ)
    meta = jnp.transpose(meta_bias.reshape(n_hb, HEADS_PER_BLOCK, N_META), (0, 2, 1))
    meta = jnp.broadcast_to(meta[:, None, :, :, None], (n_hb, 3, N_META, HEADS_PER_BLOCK, NQ))
    meta = meta.reshape(n_hb, 3, N_META, HEADS_PER_BLOCK * NQ)
    return jnp.concatenate([tab, meta], axis=2).astype(F32)


def _route_kernel(o_ref, om_ref, h_ref, wout_ref, g_ref, wrt_ref, brc_ref, triu_ref,
                  h_out, xn_out, route_out, cnt_out, *, n_tiles):
    i = pl.program_id(0)

    def body(o):
        h = h_ref[...] + _dot(o, wout_ref[...])
        h_out[...] = h
        xn = _rms(h, g_ref[...]).astype(BF16)
        xn_out[...] = xn
        logits = _dot_nt(wrt_ref[...], xn) + brc_ref[...]
        tm = logits.shape[1]
        sub = lax.broadcasted_iota(jnp.int32, (N_EXPERTS, tm), 0)
        m1 = jnp.max(logits, axis=0, keepdims=True)
        i1 = jnp.min(jnp.where(logits == m1, sub, N_EXPERTS), axis=0, keepdims=True)
        oh1 = sub == i1
        rest = jnp.where(oh1, -jnp.inf, logits)
        m2 = jnp.max(rest, axis=0, keepdims=True)
        i2 = jnp.min(jnp.where(rest == m2, sub, N_EXPERTS), axis=0, keepdims=True)
        oh2 = sub == i2
        t = jnp.exp(m2 - m1)
        g1 = 1.0 / (1.0 + t)
        g2 = t * g1
        oh = jnp.where(oh1 | oh2, 1.0, 0.0)
        cum = _dot(oh.astype(BF16), triu_ref[...])
        lr1 = jnp.sum(jnp.where(oh1, cum, 0.0), axis=0, keepdims=True)
        lr2 = jnp.sum(jnp.where(oh2, cum, 0.0), axis=0, keepdims=True)
        route = jnp.zeros((N_EXPERTS, tm), F32)
        for k, val in enumerate([i1.astype(F32), i2.astype(F32), lr1, lr2, g1, g2]):
            route = jnp.where(sub == k, val, route)
        route_out[...] = route
        cnt_out[0] = jnp.broadcast_to(jnp.sum(oh, axis=1, keepdims=True), (N_EXPERTS, LANES))

    @pl.when(i < n_tiles)
    def _():
        body(o_ref[...])

    @pl.when(i == n_tiles)
    def _():
        body(om_ref[...])


def _route_layer(o_main, o_meta, h, wout, g, wrt, brc, plan):
    tm = plan["tm"]
    d = h.shape[1]
    n_tiles = plan["n_main"] // tm
    triu = (jnp.arange(tm)[:, None] < jnp.arange(tm)[None, :]).astype(BF16)
    kern = functools.partial(_route_kernel, n_tiles=n_tiles)
    rows = (n_tiles + 1) * tm
    return pl.pallas_call(
        kern,
        grid=(n_tiles + 1,),
        in_specs=[
            pl.BlockSpec((tm, d), lambda i: (jnp.minimum(i, n_tiles - 1), 0)),
            _const_spec(o_meta.shape),
            pl.BlockSpec((tm, d), lambda i: (i, 0)),
            _const_spec(wout.shape), _const_spec(g.shape), _const_spec(wrt.shape),
            _const_spec(brc.shape), _const_spec(triu.shape),
        ],
        out_specs=[
            pl.BlockSpec((tm, d), lambda i: (i, 0)),
            pl.BlockSpec((tm, d), lambda i: (i, 0)),
            pl.BlockSpec((N_EXPERTS, tm), lambda i: (0, i)),
            pl.BlockSpec((1, N_EXPERTS, LANES), lambda i: (i, 0, 0)),
        ],
        out_shape=[jax.ShapeDtypeStruct((rows, d), F32),
                   jax.ShapeDtypeStruct((rows, d), BF16),
                   jax.ShapeDtypeStruct((N_EXPERTS, rows), F32),
                   jax.ShapeDtypeStruct((n_tiles + 1, N_EXPERTS, LANES), F32)],
        compiler_params=pltpu.CompilerParams(
            dimension_semantics=("arbitrary",), vmem_limit_bytes=VMEM_LIMIT),
        name="outproj_router",
    )(o_main, o_meta, h, wout, g, wrt, brc, triu)


def _for_each_run(tab_ref, tm, fn):
    for e in range(N_EXPERTS):
        local, length, sorted_row = tab_ref[0, 0, e], tab_ref[0, 1, e], tab_ref[0, 2, e]
        n = tm
        while n >= SEG_ALIGN:
            @pl.when((length & n) != 0)
            def _(n=n):
                done = (length // (2 * n)) * (2 * n)
                fn(pl.multiple_of(local + done, SEG_ALIGN),
                   pl.multiple_of(sorted_row + done, SEG_ALIGN), n)
            n //= 2


def _dispatch_kernel(tab_ref, tabp_ref, tail_ref, slot_ref, route_ref, x_ref, xs_ref, stage_ref,
                     zero_ref, sems, zsem, *, tm, te):
    i = pl.program_id(0)
    d = x_ref.shape[1]
    slot = i % 2

    @pl.when(i == 0)
    def _():
        zero_ref[...] = jnp.zeros(zero_ref.shape, F32)

        def zero_tile(start):
            cp = pltpu.make_async_copy(
                zero_ref, xs_ref.at[pl.ds(pl.multiple_of(start, te), te)], zsem)
            cp.start()
            cp.wait()

        for e in range(N_EXPERTS):
            @pl.when(tail_ref[e] >= 0)
            def _():
                zero_tile(tail_ref[e])

        def zero_unused(u, carry):
            zero_tile(u * te)
            return carry

        lax.fori_loop(tail_ref[N_EXPERTS], xs_ref.shape[0] // te, zero_unused, 0)

    s_rows = stage_ref.shape[1]
    slot_iota = lax.broadcasted_iota(jnp.int32, (s_rows, tm), 0)
    first = slot_iota == slot_ref[0:1, :]
    second = slot_iota == slot_ref[1:2, :]
    perm = jnp.where(first | second, 1.0, 0.0).astype(BF16)
    stage_ref[slot, :, 0:d] = _dot(perm, x_ref[...])
    gate = jnp.sum(jnp.where(first, route_ref[4:5, :], 0.0)
                   + jnp.where(second, route_ref[5:6, :], 0.0), axis=1, keepdims=True)
    stage_ref[slot, :, d:] = jnp.broadcast_to(gate, (s_rows, LANES))

    def piece(buf, local_row, sorted_row, n):
        return pltpu.make_async_copy(
            stage_ref.at[buf, pl.ds(local_row, n)], xs_ref.at[pl.ds(sorted_row, n)],
            sems.at[buf])

    _for_each_run(tab_ref, tm, lambda lr, sr, n: piece(slot, lr, sr, n).start())

    @pl.when(i > 0)
    def _():
        _for_each_run(tabp_ref, tm, lambda lr, sr, n: piece(1 - slot, lr, sr, n).wait())

    @pl.when(i == pl.num_programs(0) - 1)
    def _():
        _for_each_run(tab_ref, tm, lambda lr, sr, n: piece(slot, lr, sr, n).wait())


def _dispatch(xn, slot_t, route_t, tab, tails, plan):
    tm = plan["tm"]
    te = plan["te"]
    d = xn.shape[1]
    kern = functools.partial(_dispatch_kernel, tm=tm, te=te)
    return pl.pallas_call(
        kern,
        grid=(xn.shape[0] // tm,),
        in_specs=[
            pl.BlockSpec((1, 3, N_EXPERTS), lambda i: (i, 0, 0), memory_space=pltpu.SMEM),
            pl.BlockSpec((1, 3, N_EXPERTS), lambda i: (jnp.maximum(i - 1, 0), 0, 0),
                         memory_space=pltpu.SMEM),
            pl.BlockSpec(memory_space=pltpu.SMEM),
            pl.BlockSpec((2, tm), lambda i: (0, i)),
            pl.BlockSpec((N_EXPERTS, tm), lambda i: (0, i)),
            pl.BlockSpec((tm, d), lambda i: (i, 0)),
        ],
        out_specs=pl.BlockSpec(memory_space=pl.ANY),
        out_shape=jax.ShapeDtypeStruct((plan["rows_sorted"], d + LANES), F32),
        scratch_shapes=[pltpu.VMEM((2, plan["s_rows"], d + LANES), F32),
                        pltpu.VMEM((te, d + LANES), F32),
                        pltpu.SemaphoreType.DMA((2,)), pltpu.SemaphoreType.DMA(())],
        compiler_params=pltpu.CompilerParams(
            dimension_semantics=("arbitrary",), vmem_limit_bytes=VMEM_LIMIT),
        name="moe_dispatch",
    )(tab, tab, tails, slot_t, route_t, xn)


def _expert_kernel(te_ref, nu_ref, x_ref, wg_ref, wu_ref, wd_ref, out_ref, *, sub):
    j = pl.program_id(0)
    c = pl.program_id(1)

    @pl.when(c == 0)
    def _():
        out_ref[...] = jnp.zeros(out_ref.shape, F32)

    @pl.when(j < nu_ref[0])
    def _():
        d = out_ref.shape[1]
        xb = x_ref[:, 0:d].astype(BF16)
        fc = wg_ref.shape[1]
        y = None
        for lo in range(0, fc, sub):
            hi = min(lo + sub, fc)
            act = (_silu(_dot(xb, wg_ref[:, lo:hi])) * _dot(xb, wu_ref[:, lo:hi])).astype(BF16)
            part = _dot(act, wd_ref[lo:hi, :])
            y = part if y is None else y + part
        out_ref[...] += x_ref[:, d:d + 1] * y


def _experts(xs, tile_expert, n_used, wg, wu, wd, plan):
    te = plan["te"]
    fc = plan["fc"]
    d = xs.shape[1] - LANES
    n_tiles = xs.shape[0] // te
    n_c = wg.shape[2] // fc

    def row(j, c, te_ref, nu_ref):
        return (jnp.minimum(j, nu_ref[0] - 1), 0)

    def chunk(j, c, nu_ref):
        return jnp.where(j < nu_ref[0], c, n_c - 1)

    def w_in(j, c, te_ref, nu_ref):
        return (te_ref[jnp.minimum(j, nu_ref[0] - 1)], 0, chunk(j, c, nu_ref))

    def w_out(j, c, te_ref, nu_ref):
        return (te_ref[jnp.minimum(j, nu_ref[0] - 1)], chunk(j, c, nu_ref), 0)

    grid_spec = pltpu.PrefetchScalarGridSpec(
        num_scalar_prefetch=2,
        grid=(n_tiles, n_c),
        in_specs=[
            pl.BlockSpec((te, d + LANES), row),
            pl.BlockSpec((None, d, fc), w_in),
            pl.BlockSpec((None, d, fc), w_in),
            pl.BlockSpec((None, fc, d), w_out),
        ],
        out_specs=pl.BlockSpec((te, d), lambda j, c, te_ref, nu_ref: (j, 0)),
    )
    return pl.pallas_call(
        functools.partial(_expert_kernel, sub=plan["fsub"]),
        grid_spec=grid_spec,
        out_shape=jax.ShapeDtypeStruct((xs.shape[0], d), F32),
        compiler_params=pltpu.CompilerParams(
            dimension_semantics=("arbitrary", "arbitrary"), vmem_limit_bytes=VMEM_LIMIT),
        name="moe_experts",
    )(tile_expert, n_used, xs, wg, wu, wd)


def _combine_kernel(tabc_ref, tabn_ref, slot_ref, h_ref, g_ref, ys_ref, outp_ref, outs_ref,
                    y_ref, sems, *, tm, n_tiles, n_p_tiles):
    i = pl.program_id(0)

    def piece(slot, local_row, sorted_row, n):
        return pltpu.make_async_copy(
            ys_ref.at[pl.ds(sorted_row, n)], y_ref.at[slot, pl.ds(local_row, n)], sems.at[slot])

    @pl.when(i == 0)
    def _():
        y_ref[...] = jnp.zeros(y_ref.shape, F32)
        _for_each_run(tabc_ref, tm, lambda lr, sr, n: piece(0, lr, sr, n).start())

    @pl.when(i + 1 < n_tiles)
    def _():
        _for_each_run(tabn_ref, tm, lambda lr, sr, n: piece((i + 1) % 2, lr, sr, n).start())

    slot = i % 2
    _for_each_run(tabc_ref, tm, lambda lr, sr, n: piece(slot, lr, sr, n).wait())

    y = y_ref[slot].astype(BF16)
    s_rows = y.shape[0]
    slot_iota = lax.broadcasted_iota(jnp.int32, (tm, s_rows), 1)
    pick = jnp.where((slot_iota == slot_ref[:, 0:1]) | (slot_iota == slot_ref[:, 1:2]),
                     1.0, 0.0).astype(BF16)
    h = h_ref[...] + _dot(pick, y)
    out = _rms(h, g_ref[...])

    @pl.when(i < n_p_tiles)
    def _():
        outp_ref[...] = out

    @pl.when(i >= n_p_tiles)
    def _():
        outs_ref[...] = out


def _combine(h, slot_c, tab, ys, g, plan):
    tm = plan["tm"]
    d = h.shape[1]
    n_tiles = plan["n_main"] // tm
    n_p_tiles = plan["n_p"] // tm
    kern = functools.partial(_combine_kernel, tm=tm, n_tiles=n_tiles, n_p_tiles=n_p_tiles)
    return pl.pallas_call(
        kern,
        grid=(n_tiles,),
        in_specs=[
            pl.BlockSpec((1, 3, N_EXPERTS), lambda i: (i, 0, 0), memory_space=pltpu.SMEM),
            pl.BlockSpec((1, 3, N_EXPERTS), lambda i: (jnp.minimum(i + 1, n_tiles - 1), 0, 0),
                         memory_space=pltpu.SMEM),
            pl.BlockSpec((tm, 2), lambda i: (i, 0)),
            pl.BlockSpec((tm, d), lambda i: (i, 0)),
            _const_spec(g.shape),
            pl.BlockSpec(memory_space=pl.ANY),
        ],
        out_specs=[
            pl.BlockSpec((tm, d), lambda i: (jnp.minimum(i, n_p_tiles - 1), 0)),
            pl.BlockSpec((tm, d), lambda i: (jnp.maximum(i - n_p_tiles, 0), 0)),
        ],
        out_shape=[jax.ShapeDtypeStruct((plan["n_p"], d), F32),
                   jax.ShapeDtypeStruct((plan["n_main"] - plan["n_p"], d), F32)],
        scratch_shapes=[pltpu.VMEM((2, plan["s_rows"], d), F32),
                        pltpu.SemaphoreType.DMA((2,))],
        compiler_params=pltpu.CompilerParams(
            dimension_semantics=("arbitrary",), vmem_limit_bytes=VMEM_LIMIT),
        name="moe_combine_norm",
    )(tab, tab, slot_c, h, g, ys)


def _make_plan(b_p, t_p, b_s, t_s, d_ff_expert):
    tm = 512 if t_p % 512 == 0 and t_s % 512 == 0 else 128
    te = 1024 if tm == 512 else 256
    n_p = b_p * t_p
    n_main = n_p + b_s * t_s
    n_seq = b_p + b_s
    assert n_seq * N_META <= tm and t_p % tm == 0 and t_s % tm == 0
    np_rows = n_main + tm
    rows_p, rows_s = t_p // GRID_W, t_s // GRID_W
    cr = 64 if rows_p % 64 == 0 and rows_s % 64 == 0 else 4 * Q_ROWS
    assert rows_p % cr == 0 and rows_s % cr == 0 and cr % (4 * Q_ROWS) == 0
    fc = 1792 if d_ff_expert % 1792 == 0 else d_ff_expert
    fsub = 2 * MXU_DIM
    run_pad = N_EXPERTS * (SEG_ALIGN - 1)
    s_rows = -(-(2 * tm + run_pad) // SEG_ALIGN) * SEG_ALIGN
    rows_sorted = -(-(2 * np_rows + run_pad * (np_rows // tm)) // te) * te + N_EXPERTS * te
    return dict(tm=tm, te=te, n_p=n_p, n_main=n_main, n_seq=n_seq, np_rows=np_rows, t_p=t_p,
                t_s=t_s, cr=cr, fc=fc, fsub=fsub, s_rows=s_rows, rows_sorted=rows_sorted)


def kernel(x_prompt, x_sample, meta_tokens, norm_mix, norm_ffn, norm_final, even_w_in, pool_w,
           pool_scale, conv_w, even_w_out, ffn_w_gate, ffn_w_up, ffn_w_down, na_w_qkv,
           na_rel_bias, na_meta_bias, na_w_out, moe_w_router, moe_b_router, moe_w_gate,
           moe_w_up, moe_w_down):
    b_p, t_p, d = x_prompt.shape
    b_s, t_s, _ = x_sample.shape
    plan = _make_plan(b_p, t_p, b_s, t_s, moe_w_gate.shape[-1])
    tm, te = plan["tm"], plan["te"]
    n_main, n_seq, np_rows = plan["n_main"], plan["n_seq"], plan["np_rows"]
    n_meta_rows = n_seq * N_META

    xfirst = jnp.concatenate([x_prompt[:, :EXT].reshape(-1, d), x_sample[:, :EXT].reshape(-1, d)])

    row = lambda v: v.reshape(1, -1).astype(F32)
    h = _even_layer(x_prompt.reshape(-1, d), x_sample.reshape(-1, d), meta_tokens, xfirst,
                    row(norm_mix[0]), even_w_in[0].astype(BF16), pool_w[0].astype(BF16),
                    row(pool_scale[0]), conv_w[0], even_w_out[0].astype(BF16), plan)
    h = _ffn_layer(h, row(norm_ffn[0]), ffn_w_gate[0].astype(BF16), ffn_w_up[0].astype(BF16),
                   ffn_w_down[0].astype(BF16), plan)

    width = NA_HEADS * HEAD_DIM
    log2e = float(np.log2(np.e))
    qscale = jnp.concatenate([jnp.full((width,), HEAD_DIM ** -0.5 * log2e, F32),
                              jnp.ones((width,), F32)])
    qk, vt = _qkv_layer(h, row(norm_mix[1]), (na_w_qkv[0][:, :2 * width] * qscale).astype(BF16),
                        na_w_qkv[0][:, 2 * width:].T.astype(BF16), plan)
    vt_meta = vt[:, n_main:n_main + n_meta_rows]
    n_hb = NA_HEADS // HEADS_PER_BLOCK
    o_main, o_meta = _attn_layer(
        qk, vt, vt_meta.T, vt_meta.reshape(width, n_seq, N_META).transpose(1, 0, 2),
        _attn_bias_table(na_rel_bias[0] * log2e, na_meta_bias[0] * log2e),
        (na_meta_bias[0] * log2e).reshape(n_hb, HEADS_PER_BLOCK, 1, N_META), plan)
    o_meta = jnp.concatenate([o_meta, jnp.zeros((tm - n_meta_rows, width), BF16)])

    h, xn, route_t, tile_cnt = _route_layer(
        o_main, o_meta, h, na_w_out[0].astype(BF16), row(norm_ffn[1]),
        moe_w_router[0].T.astype(BF16), moe_b_router[0].reshape(-1, 1).astype(F32), plan)

    cnt = tile_cnt[:, :, 0].astype(jnp.int32)
    seg = (cnt + SEG_ALIGN - 1) // SEG_ALIGN * SEG_ALIGN
    local = jnp.cumsum(seg, axis=1) - seg
    before = jnp.cumsum(seg, axis=0) - seg
    padded = (jnp.sum(seg, axis=0) + te - 1) // te * te
    ends = jnp.cumsum(padded)
    tab = jnp.stack([local, seg, (ends - padded)[None, :] + before], axis=1).astype(jnp.int32)
    experts_t = route_t[0:2].astype(jnp.int32)
    local_tok = jnp.repeat(local, tm, axis=0).T
    pick = experts_t[:, None, :] == jnp.arange(N_EXPERTS)[None, :, None]
    slot_t = jnp.sum(jnp.where(pick, local_tok[None], 0), axis=1) + route_t[2:4].astype(jnp.int32)
    tails = jnp.concatenate([jnp.where(padded > 0, ends - te, -1), ends[-1:] // te])
    tails = tails.astype(jnp.int32)
    n_row_tiles = plan["rows_sorted"] // te
    tile_expert = jnp.minimum(
        jnp.sum(ends[None, :] <= (jnp.arange(n_row_tiles) * te)[:, None], axis=1),
        N_EXPERTS - 1).astype(jnp.int32)
    n_used = (ends[-1:] // te).astype(jnp.int32)

    xs = _dispatch(xn, slot_t, route_t, tab, tails, plan)
    ys = _experts(xs, tile_expert, n_used, moe_w_gate[0].astype(BF16), moe_w_up[0].astype(BF16),
                  moe_w_down[0].astype(BF16), plan)
    y_p, y_s = _combine(h, slot_t.T, tab, ys, row(norm_final), plan)
    return (y_p.reshape(b_p, t_p, d), y_s.reshape(b_s, t_s, d))
```

```python
import functools

import jax
import jax.numpy as jnp
import numpy as np
from jax import lax
from jax.experimental import pallas as pl
from jax.experimental.pallas import tpu as pltpu

F32 = jnp.float32
BF16 = jnp.bfloat16

N_META = 16
GRID_W = 64
RMS_EPS = 1e-6
POOL_WINDOWS = (2, 4, 8, 16)
POOL_GROUP = 128
POOL_WIDTH = 512
CONV_WIDTH = 512
NA_HEADS = 16
HEAD_DIM = 64
NA_WIN_ROWS = 8
NA_WIN_COLS = 16
N_EXPERTS = 8
LANES = 128
MXU_DIM = 256
SEG_ALIGN = 8
HEADS_PER_BLOCK = LANES // HEAD_DIM
HALO_ROWS = NA_WIN_ROWS // 2
Q_ROWS = 4
WIN_ROWS = Q_ROWS + NA_WIN_ROWS
NQ = Q_ROWS * GRID_W
NK = WIN_ROWS * GRID_W
V_ROWS = HEAD_DIM + 16
EXT = 16
VMEM_LIMIT = 56 * 1024 * 1024


def _const_spec(shape):
    nd = len(shape)
    return pl.BlockSpec(shape, lambda *_: (0,) * nd, pipeline_mode=pl.Buffered(1))


def _rms(x, g):
    ms = jnp.mean(x * x, axis=-1, keepdims=True)
    return x * lax.rsqrt(ms + RMS_EPS) * g


def _dot(a, b):
    return jnp.dot(a, b, preferred_element_type=F32)


def _dot_nt(a, b):
    return lax.dot_general(a, b, (((1,), (1,)), ((), ())), preferred_element_type=F32)


def _silu(x):
    return x * jax.nn.sigmoid(x)


def _even_mix(xe, pos, seq_len, g, win_ref, pw_ref, ps, cw, wout_ref):
    n = xe.shape[0]
    xn = _rms(xe, g).astype(BF16)
    z = _dot(xn, win_ref[...])
    a_parts = []
    for gi, k in enumerate(POOL_WINDOWS):
        p = z[:, gi * POOL_GROUP:(gi + 1) * POOL_GROUP]
        s = p
        w = 1
        while w < k:
            s = s + pltpu.roll(s, n - w, 0)
            w *= 2
        left = k // 2
        right = k - 1 - left
        s = pltpu.roll(s, left, 0)
        cnt = jnp.minimum(pos + right + 1, seq_len) - jnp.maximum(pos - left, 0)
        inv = 1.0 / jnp.maximum(cnt, 1).astype(F32)
        pooled = (s * inv - p).astype(BF16)
        a_parts.append(_dot(pooled, pw_ref[gi]))
    a = jnp.concatenate(a_parts, axis=-1) * ps
    gb = z[:, POOL_WIDTH:POOL_WIDTH + CONV_WIDTH]
    gc = z[:, POOL_WIDTH + CONV_WIDTH:POOL_WIDTH + 2 * CONV_WIDTH]
    hc = z[:, POOL_WIDTH + 2 * CONV_WIDTH:]
    u = gc * hc
    conv = pltpu.roll(u, 1, 0) * cw[0:1] + u * cw[1:2] + pltpu.roll(u, n - 1, 0) * cw[2:3]
    c = gb * conv
    return (_dot(a.astype(BF16), wout_ref[0:POOL_WIDTH, :])
            + _dot(c.astype(BF16), wout_ref[POOL_WIDTH:, :]))


def _even_kernel(xp_ref, xpp_ref, xpn_ref, xs_ref, xsp_ref, xsn_ref, meta_ref, first_ref, g_ref,
                 win_ref, pw_ref, ps_ref, cw_ref, wout_ref, out_ref, xe_ref, *, tm, n_tiles,
                 tiles_p, n_p_tiles, tiles_s, t_p, t_s, n_seq):
    i = pl.program_id(0)
    weights = (g_ref[...], win_ref, pw_ref, ps_ref[...], cw_ref[...], wout_ref)
    d = xe_ref.shape[1]

    def token_tile(tile_ref, prev_ref, next_ref, tin, last, t_seq):
        n = tm + 2 * EXT

        @pl.when(tin == 0)
        def _():
            xe_ref[0:EXT, :] = meta_ref[...]

        @pl.when(tin != 0)
        def _():
            xe_ref[0:EXT, :] = prev_ref[...]

        xe_ref[EXT:EXT + tm, :] = tile_ref[...]

        @pl.when(tin == last)
        def _():
            xe_ref[EXT + tm:n, :] = jnp.zeros((EXT, d), F32)

        @pl.when(tin != last)
        def _():
            xe_ref[EXT + tm:n, :] = next_ref[...]

        pos = lax.broadcasted_iota(jnp.int32, (n, 1), 0) + tin * tm
        y = _even_mix(xe_ref[0:n, :], pos, t_seq + N_META, *weights)
        out_ref[...] = xe_ref[EXT:EXT + tm, :] + y[EXT:EXT + tm]

    @pl.when(i < n_p_tiles)
    def _prompt():
        token_tile(xp_ref, xpp_ref, xpn_ref, i % tiles_p, tiles_p - 1, t_p)

    @pl.when((i >= n_p_tiles) & (i < n_tiles))
    def _sample():
        token_tile(xs_ref, xsp_ref, xsn_ref, (i - n_p_tiles) % tiles_s, tiles_s - 1, t_s)

    @pl.when(i == n_tiles)
    def _meta():
        grp = 3 * EXT
        n = grp * n_seq
        for b in range(n_seq):
            xe_ref[grp * b:grp * b + EXT, :] = jnp.zeros((EXT, d), F32)
            xe_ref[grp * b + EXT:grp * b + 2 * EXT, :] = meta_ref[...]
            xe_ref[grp * b + 2 * EXT:grp * (b + 1), :] = first_ref[EXT * b:EXT * (b + 1), :]
        pos = lax.broadcasted_iota(jnp.int32, (n, 1), 0) % grp - EXT
        y = _even_mix(xe_ref[0:n, :], pos, jnp.int32(1 << 30), *weights)
        out_ref[...] = jnp.zeros(out_ref.shape, F32)
        for b in range(n_seq):
            out_ref[EXT * b:EXT * (b + 1), :] = meta_ref[...] + y[grp * b + EXT:grp * b + 2 * EXT]


def _even_layer(xp, xs, meta, xfirst, g, win, pw, ps, cw, wout, plan):
    tm = plan["tm"]
    d = xp.shape[1]
    n_tiles = plan["n_main"] // tm
    n_p_tiles = plan["n_p"] // tm
    n_s_tiles = n_tiles - n_p_tiles
    blk = tm // EXT
    kern = functools.partial(
        _even_kernel, tm=tm, n_tiles=n_tiles, tiles_p=plan["t_p"] // tm, n_p_tiles=n_p_tiles,
        tiles_s=plan["t_s"] // tm, t_p=plan["t_p"], t_s=plan["t_s"], n_seq=plan["n_seq"])
    xe_rows = max(tm + 2 * EXT, 3 * EXT * plan["n_seq"])

    def specs(first_tile, count):
        last = count * blk - 1
        tile = lambda i: jnp.clip(i - first_tile, 0, count - 1)
        return [
            pl.BlockSpec((tm, d), lambda i: (tile(i), 0)),
            pl.BlockSpec((EXT, d), lambda i: (jnp.clip(tile(i) * blk - 1, 0, last), 0)),
            pl.BlockSpec((EXT, d), lambda i: (jnp.clip((tile(i) + 1) * blk, 0, last), 0)),
        ]

    return pl.pallas_call(
        kern,
        grid=(n_tiles + 1,),
        in_specs=specs(0, n_p_tiles) + specs(n_p_tiles, n_s_tiles) + [
            _const_spec(meta.shape), _const_spec(xfirst.shape), _const_spec(g.shape),
            _const_spec(win.shape), _const_spec(pw.shape), _const_spec(ps.shape),
            _const_spec(cw.shape), _const_spec(wout.shape),
        ],
        out_specs=pl.BlockSpec((tm, d), lambda i: (i, 0)),
        out_shape=jax.ShapeDtypeStruct((plan["np_rows"], d), F32),
        scratch_shapes=[pltpu.VMEM((xe_rows, d), F32)],
        compiler_params=pltpu.CompilerParams(
            dimension_semantics=("arbitrary",), vmem_limit_bytes=VMEM_LIMIT),
        name="even_mixer",
    )(xp, xp, xp, xs, xs, xs, meta, xfirst, g, win, pw, ps, cw, wout)


def _ffn_kernel(x_ref, g_ref, wg_ref, wu_ref, wd_ref, out_ref):
    x = x_ref[...]
    xn = _rms(x, g_ref[...]).astype(BF16)
    act = (_silu(_dot(xn, wg_ref[...])) * _dot(xn, wu_ref[...])).astype(BF16)
    out_ref[...] = x + _dot(act, wd_ref[...])


def _ffn_layer(h, g, wg, wu, wd, plan):
    tm = plan["tm"]
    d = h.shape[1]
    return pl.pallas_call(
        _ffn_kernel,
        grid=(h.shape[0] // tm,),
        in_specs=[pl.BlockSpec((tm, d), lambda i: (i, 0)), _const_spec(g.shape),
                  _const_spec(wg.shape), _const_spec(wu.shape), _const_spec(wd.shape)],
        out_specs=pl.BlockSpec((tm, d), lambda i: (i, 0)),
        out_shape=jax.ShapeDtypeStruct(h.shape, F32),
        compiler_params=pltpu.CompilerParams(
            dimension_semantics=("arbitrary",), vmem_limit_bytes=VMEM_LIMIT),
        name="dense_swiglu",
    )(h, g, wg, wu, wd)


def _qkv_kernel(x_ref, g_ref, wqk_ref, wvt_ref, qk_ref, vt_ref):
    xn = _rms(x_ref[...], g_ref[...]).astype(BF16)
    qk_ref[...] = _dot(xn, wqk_ref[...]).astype(BF16)
    vt_ref[...] = _dot_nt(wvt_ref[...], xn).astype(BF16)


def _qkv_layer(h, g, wqk, wvt, plan):
    tm = plan["tm"]
    d = h.shape[1]
    return pl.pallas_call(
        _qkv_kernel,
        grid=(h.shape[0] // tm,),
        in_specs=[pl.BlockSpec((tm, d), lambda i: (i, 0)), _const_spec(g.shape),
                  _const_spec(wqk.shape), _const_spec(wvt.shape)],
        out_specs=[pl.BlockSpec((tm, wqk.shape[1]), lambda i: (i, 0)),
                   pl.BlockSpec((wvt.shape[0], tm), lambda i: (0, i))],
        out_shape=[jax.ShapeDtypeStruct((h.shape[0], wqk.shape[1]), BF16),
                   jax.ShapeDtypeStruct((wvt.shape[0], h.shape[0]), BF16)],
        compiler_params=pltpu.CompilerParams(
            dimension_semantics=("arbitrary",), vmem_limit_bytes=VMEM_LIMIT),
        name="qkv_proj",
    )(h, g, wqk, wvt)


def _attn_kernel(q_ref, kc_ref, kp_ref, kn_ref, vc_ref, vp_ref, vn_ref, qm_ref, km_ref, vm_ref,
                 vmt_ref, tab_ref, mb_ref, o_ref, om_ref, kx_ref, vx_ref, s_ref, *, cr, chunks_p,
                 n_p_chunks, chunks_s, rows_p, rows_s):
    ci = pl.program_id(1)
    is_p = ci < n_p_chunks
    cin = jnp.where(is_p, ci % chunks_p, (ci - n_p_chunks) % chunks_s)
    rows = jnp.where(is_p, rows_p, rows_s)
    halo = HALO_ROWS * GRID_W
    ct = cr * GRID_W

    kx_ref[0:halo, :] = kp_ref[...]
    kx_ref[halo:halo + ct, :] = kc_ref[...]
    kx_ref[halo + ct:, :] = kn_ref[...]
    ones_rows = V_ROWS - HEAD_DIM
    vmt_aug = []
    for hh in range(HEADS_PER_BLOCK):
        hs = slice(hh * HEAD_DIM, (hh + 1) * HEAD_DIM)
        vx_ref[hh, 0:HEAD_DIM, 0:halo] = vp_ref[hs, :]
        vx_ref[hh, 0:HEAD_DIM, halo:halo + ct] = vc_ref[hs, :]
        vx_ref[hh, 0:HEAD_DIM, halo + ct:] = vn_ref[hs, :]
        vx_ref[hh, HEAD_DIM:, :] = jnp.ones((ones_rows, ct + 2 * halo), BF16)
        vmt_aug.append(jnp.concatenate(
            [vmt_ref[hs, :], jnp.ones((ones_rows, N_META), BF16)], axis=0))

    lane = lax.broadcasted_iota(jnp.int32, (1, LANES), 1)
    head_masks = [(lane // HEAD_DIM) == hh for hh in range(HEADS_PER_BLOCK)]
    km = km_ref[...]
    r0c = cin * cr

    n_blocks = cr // Q_ROWS

    def scores(bi, slot):
        r0 = r0c + bi * Q_ROWS
        edge = jnp.where(r0 == 0, 0, jnp.where(r0 == rows - Q_ROWS, 2, 1))
        off = pl.multiple_of(bi * NQ, NQ)
        q = q_ref[pl.ds(off, NQ), :]
        qq = jnp.concatenate(
            [jnp.where(head_masks[hh], q, jnp.zeros_like(q)) for hh in range(HEADS_PER_BLOCK)],
            axis=0)
        s_ref[slot, 0:NK, :] = _dot_nt(kx_ref[pl.ds(off, NK), :], qq) + tab_ref[edge, 0:NK, :]
        s_ref[slot, NK:NK + N_META, :] = _dot_nt(km, qq) + tab_ref[edge, NK:NK + N_META, :]

    def softmax_pv(bi, slot):
        off = pl.multiple_of(bi * NQ, NQ)
        s = s_ref[slot]
        pb = jnp.exp2(s - jnp.max(s, axis=0, keepdims=True)).astype(BF16)
        outs = []
        for hh in range(HEADS_PER_BLOCK):
            qs = slice(hh * NQ, (hh + 1) * NQ)
            oa = (_dot(vx_ref[hh, :, pl.ds(off, NK)], pb[0:NK, qs])
                  + _dot(vmt_aug[hh], pb[NK:NK + N_META, qs]))
            outs.append(oa[0:HEAD_DIM] / oa[HEAD_DIM:HEAD_DIM + 1])
        o_ref[pl.ds(off, NQ), :] = jnp.concatenate(outs, axis=0).T.astype(BF16)

    scores(0, 0)
    scores(1, 1)

    def quad_body(b4, carry):
        b = 4 * b4
        for k in range(4):
            scores(jnp.minimum(b + k + 2, n_blocks - 1), (k + 2) % 4)
            softmax_pv(b + k, k)
        return carry

    lax.fori_loop(0, n_blocks // 4, quad_body, 0)

    qm = qm_ref[...]
    vm = vm_ref[...]
    out = None
    for hh in range(HEADS_PER_BLOCK):
        qh = jnp.where(head_masks[hh], qm, jnp.zeros_like(qm))
        sm = _dot_nt(qh, km) + mb_ref[hh, 0:1, 0:N_META]
        pm = jnp.exp2(sm - jnp.max(sm, axis=-1, keepdims=True))
        o = _dot(pm.astype(BF16), vm) / jnp.sum(pm, axis=-1, keepdims=True)
        out = o if out is None else jnp.where(head_masks[hh], o, out)
    om_ref[...] = out.astype(BF16)


def _attn_layer(qk, vt, vm, vmt, bias_tab, meta_bias, plan):
    cr = plan["cr"]
    ct = cr * GRID_W
    halo = HALO_ROWS * GRID_W
    n_main = plan["n_main"]
    n_chunks = n_main // ct
    n_hb = NA_HEADS // HEADS_PER_BLOCK
    hb = ct // halo
    last_halo = n_main // halo - 1
    meta_blk0 = n_main // N_META
    n_p_chunks = plan["n_p"] // ct
    chunks_p = plan["t_p"] // ct
    chunks_s = plan["t_s"] // ct
    b_p = plan["n_p"] // plan["t_p"]

    def seq_of(ci):
        return jnp.where(ci < n_p_chunks, ci // chunks_p, b_p + (ci - n_p_chunks) // chunks_s)

    def cur(off):
        return pl.BlockSpec((ct, LANES), lambda h, c: (c, off + h))

    def prev(off):
        return pl.BlockSpec((halo, LANES), lambda h, c: (jnp.maximum(c * hb - 1, 0), off + h))

    def nxt(off):
        return pl.BlockSpec((halo, LANES),
                            lambda h, c: (jnp.minimum((c + 1) * hb, last_halo), off + h))

    def meta(off):
        return pl.BlockSpec((N_META, LANES), lambda h, c: (meta_blk0 + seq_of(c), off + h))

    def vt_cur():
        return pl.BlockSpec((LANES, ct), lambda h, c: (h, c))

    def vt_prev():
        return pl.BlockSpec((LANES, halo), lambda h, c: (h, jnp.maximum(c * hb - 1, 0)))

    def vt_next():
        return pl.BlockSpec((LANES, halo), lambda h, c: (h, jnp.minimum((c + 1) * hb, last_halo)))

    kern = functools.partial(
        _attn_kernel, cr=cr, chunks_p=chunks_p, n_p_chunks=n_p_chunks, chunks_s=chunks_s,
        rows_p=plan["t_p"] // GRID_W, rows_s=plan["t_s"] // GRID_W)
    width = NA_HEADS * HEAD_DIM
    return pl.pallas_call(
        kern,
        grid=(n_hb, n_chunks),
        in_specs=[
            cur(0), cur(n_hb), prev(n_hb), nxt(n_hb), vt_cur(), vt_prev(), vt_next(),
            meta(0), meta(n_hb),
            pl.BlockSpec((N_META, LANES), lambda h, c: (seq_of(c), h)),
            pl.BlockSpec((None, LANES, N_META), lambda h, c: (seq_of(c), h, 0)),
            pl.BlockSpec((None,) + bias_tab.shape[1:], lambda h, c: (h, 0, 0, 0)),
            pl.BlockSpec((None, HEADS_PER_BLOCK, 8, LANES), lambda h, c: (h, 0, 0, 0)),
        ],
        out_specs=[
            pl.BlockSpec((ct, LANES), lambda h, c: (c, h)),
            pl.BlockSpec((N_META, LANES), lambda h, c: (seq_of(c), h)),
        ],
        out_shape=[jax.ShapeDtypeStruct((n_main, width), BF16),
                   jax.ShapeDtypeStruct((plan["n_seq"] * N_META, width), BF16)],
        scratch_shapes=[pltpu.VMEM((ct + 2 * halo, LANES), BF16),
                        pltpu.VMEM((HEADS_PER_BLOCK, V_ROWS, ct + 2 * halo), BF16),
                        pltpu.VMEM((4, NK + N_META, HEADS_PER_BLOCK * NQ), F32)],
        compiler_params=pltpu.CompilerParams(
            dimension_semantics=("arbitrary", "arbitrary"), vmem_limit_bytes=VMEM_LIMIT),
        name="neighbourhood_attention",
    )(qk, qk, qk, qk, vt, vt, vt, qk, qk, vm, vmt, bias_tab, meta_bias)


def _attn_bias_table(rel_bias, meta_bias):
    n_heads, n_row_off, n_col_off = rel_bias.shape
    half = n_col_off // 2
    ring = jnp.concatenate([
        rel_bias[:, :, half::-1],
        jnp.zeros((n_heads, n_row_off, 2 * GRID_W - n_col_off), F32),
        rel_bias[:, :, :half:-1]], axis=-1)
    toe = jnp.tile(ring, (1, 1, GRID_W))[:, :, :GRID_W * (2 * GRID_W - 1)]
    toe = toe.reshape(n_heads, n_row_off, GRID_W, 2 * GRID_W - 1)[..., :GRID_W]
    qc = np.arange(GRID_W)[None, :]
    kc = np.arange(GRID_W)[:, None]
    cstart = np.clip(qc - NA_WIN_COLS // 2, 0, GRID_W - NA_WIN_COLS)
    col_ok = (kc >= cstart) & (kc < cstart + NA_WIN_COLS)
    toe = jnp.where(col_ok[None, None], toe, -1e30)
    n_hb = n_heads // HEADS_PER_BLOCK
    toe = toe.reshape(n_hb, HEADS_PER_BLOCK, n_row_off, GRID_W, GRID_W)
    per_a = jnp.stack([toe[:, :, HALO_ROWS - 1 - a:HALO_ROWS - 1 - a + WIN_ROWS]
                       for a in range(Q_ROWS)], axis=2)
    per_a = jnp.transpose(per_a, (0, 3, 4, 1, 2, 5))
    a = np.arange(Q_ROWS)[None, :]
    b = np.arange(WIN_ROWS)[:, None]
    row_ok = np.stack([(b >= HALO_ROWS) & (a >= 0),
                       (b - a >= 0) & (b - a < NA_WIN_ROWS),
                       (b < NA_WIN_ROWS) & (a >= 0)])
    tab = jnp.where(row_ok[None, :, :, None, None, :, None], per_a[:, None], -1e30)
    tab = tab.reshape(n_hb, 3, NK, HEADS_PER_BLOCK * NQ)
    meta = jnp.transpose(meta_bias.reshape(n_hb, HEADS_PER_BLOCK, N_META), (0, 2, 1))
    meta = jnp.broadcast_to(meta[:, None, :, :, None], (n_hb, 3, N_META, HEADS_PER_BLOCK, NQ))
    meta = meta.reshape(n_hb, 3, N_META, HEADS_PER_BLOCK * NQ)
    return jnp.concatenate([tab, meta], axis=2).astype(F32)


def _route_kernel(o_ref, om_ref, h_ref, wout_ref, g_ref, wrt_ref, brc_ref, triu_ref,
                  h_out, xn_out, route_out, cnt_out, *, n_tiles):
    i = pl.program_id(0)

    def body(o):
        h = h_ref[...] + _dot(o, wout_ref[...])
        h_out[...] = h
        xn = _rms(h, g_ref[...]).astype(BF16)
        xn_out[...] = xn
        logits = _dot_nt(wrt_ref[...], xn) + brc_ref[...]
        tm = logits.shape[1]
        sub = lax.broadcasted_iota(jnp.int32, (N_EXPERTS, tm), 0)
        m1 = jnp.max(logits, axis=0, keepdims=True)
        i1 = jnp.min(jnp.where(logits == m1, sub, N_EXPERTS), axis=0, keepdims=True)
        oh1 = sub == i1
        rest = jnp.where(oh1, -jnp.inf, logits)
        m2 = jnp.max(rest, axis=0, keepdims=True)
        i2 = jnp.min(jnp.where(rest == m2, sub, N_EXPERTS), axis=0, keepdims=True)
        oh2 = sub == i2
        t = jnp.exp(m2 - m1)
        g1 = 1.0 / (1.0 + t)
        g2 = t * g1
        oh = jnp.where(oh1 | oh2, 1.0, 0.0)
        cum = _dot(oh.astype(BF16), triu_ref[...])
        lr1 = jnp.sum(jnp.where(oh1, cum, 0.0), axis=0, keepdims=True)
        lr2 = jnp.sum(jnp.where(oh2, cum, 0.0), axis=0, keepdims=True)
        route = jnp.zeros((N_EXPERTS, tm), F32)
        for k, val in enumerate([i1.astype(F32), i2.astype(F32), lr1, lr2, g1, g2]):
            route = jnp.where(sub == k, val, route)
        route_out[...] = route
        cnt_out[0] = jnp.broadcast_to(jnp.sum(oh, axis=1, keepdims=True), (N_EXPERTS, LANES))

    @pl.when(i < n_tiles)
    def _():
        body(o_ref[...])

    @pl.when(i == n_tiles)
    def _():
        body(om_ref[...])


def _route_layer(o_main, o_meta, h, wout, g, wrt, brc, plan):
    tm = plan["tm"]
    d = h.shape[1]
    n_tiles = plan["n_main"] // tm
    triu = (jnp.arange(tm)[:, None] < jnp.arange(tm)[None, :]).astype(BF16)
    kern = functools.partial(_route_kernel, n_tiles=n_tiles)
    rows = (n_tiles + 1) * tm
    return pl.pallas_call(
        kern,
        grid=(n_tiles + 1,),
        in_specs=[
            pl.BlockSpec((tm, d), lambda i: (jnp.minimum(i, n_tiles - 1), 0)),
            _const_spec(o_meta.shape),
            pl.BlockSpec((tm, d), lambda i: (i, 0)),
            _const_spec(wout.shape), _const_spec(g.shape), _const_spec(wrt.shape),
            _const_spec(brc.shape), _const_spec(triu.shape),
        ],
        out_specs=[
            pl.BlockSpec((tm, d), lambda i: (i, 0)),
            pl.BlockSpec((tm, d), lambda i: (i, 0)),
            pl.BlockSpec((N_EXPERTS, tm), lambda i: (0, i)),
            pl.BlockSpec((1, N_EXPERTS, LANES), lambda i: (i, 0, 0)),
        ],
        out_shape=[jax.ShapeDtypeStruct((rows, d), F32),
                   jax.ShapeDtypeStruct((rows, d), BF16),
                   jax.ShapeDtypeStruct((N_EXPERTS, rows), F32),
                   jax.ShapeDtypeStruct((n_tiles + 1, N_EXPERTS, LANES), F32)],
        compiler_params=pltpu.CompilerParams(
            dimension_semantics=("arbitrary",), vmem_limit_bytes=VMEM_LIMIT),
        name="outproj_router",
    )(o_main, o_meta, h, wout, g, wrt, brc, triu)


def _for_each_run(tab_ref, tm, fn):
    for e in range(N_EXPERTS):
        local, length, sorted_row = tab_ref[0, 0, e], tab_ref[0, 1, e], tab_ref[0, 2, e]
        n = tm
        while n >= SEG_ALIGN:
            @pl.when((length & n) != 0)
            def _(n=n):
                done = (length // (2 * n)) * (2 * n)
                fn(pl.multiple_of(local + done, SEG_ALIGN),
                   pl.multiple_of(sorted_row + done, SEG_ALIGN), n)
            n //= 2


def _dispatch_kernel(tab_ref, tabp_ref, tail_ref, slot_ref, route_ref, x_ref, xs_ref, stage_ref,
                     zero_ref, sems, zsem, *, tm, te):
    i = pl.program_id(0)
    d = x_ref.shape[1]
    slot = i % 2

    @pl.when(i == 0)
    def _():
        zero_ref[...] = jnp.zeros(zero_ref.shape, F32)

        def zero_tile(start):
            cp = pltpu.make_async_copy(
                zero_ref, xs_ref.at[pl.ds(pl.multiple_of(start, te), te)], zsem)
            cp.start()
            cp.wait()

        for e in range(N_EXPERTS):
            @pl.when(tail_ref[e] >= 0)
            def _():
                zero_tile(tail_ref[e])

        def zero_unused(u, carry):
            zero_tile(u * te)
            return carry

        lax.fori_loop(tail_ref[N_EXPERTS], xs_ref.shape[0] // te, zero_unused, 0)

    s_rows = stage_ref.shape[1]
    slot_iota = lax.broadcasted_iota(jnp.int32, (s_rows, tm), 0)
    first = slot_iota == slot_ref[0:1, :]
    second = slot_iota == slot_ref[1:2, :]
    perm = jnp.where(first | second, 1.0, 0.0).astype(BF16)
    stage_ref[slot, :, 0:d] = _dot(perm, x_ref[...])
    gate = jnp.sum(jnp.where(first, route_ref[4:5, :], 0.0)
                   + jnp.where(second, route_ref[5:6, :], 0.0), axis=1, keepdims=True)
    stage_ref[slot, :, d:] = jnp.broadcast_to(gate, (s_rows, LANES))

    def piece(buf, local_row, sorted_row, n):
        return pltpu.make_async_copy(
            stage_ref.at[buf, pl.ds(local_row, n)], xs_ref.at[pl.ds(sorted_row, n)],
            sems.at[buf])

    _for_each_run(tab_ref, tm, lambda lr, sr, n: piece(slot, lr, sr, n).start())

    @pl.when(i > 0)
    def _():
        _for_each_run(tabp_ref, tm, lambda lr, sr, n: piece(1 - slot, lr, sr, n).wait())

    @pl.when(i == pl.num_programs(0) - 1)
    def _():
        _for_each_run(tab_ref, tm, lambda lr, sr, n: piece(slot, lr, sr, n).wait())


def _dispatch(xn, slot_t, route_t, tab, tails, plan):
    tm = plan["tm"]
    te = plan["te"]
    d = xn.shape[1]
    kern = functools.partial(_dispatch_kernel, tm=tm, te=te)
    return pl.pallas_call(
        kern,
        grid=(xn.shape[0] // tm,),
        in_specs=[
            pl.BlockSpec((1, 3, N_EXPERTS), lambda i: (i, 0, 0), memory_space=pltpu.SMEM),
            pl.BlockSpec((1, 3, N_EXPERTS), lambda i: (jnp.maximum(i - 1, 0), 0, 0),
                         memory_space=pltpu.SMEM),
            pl.BlockSpec(memory_space=pltpu.SMEM),
            pl.BlockSpec((2, tm), lambda i: (0, i)),
            pl.BlockSpec((N_EXPERTS, tm), lambda i: (0, i)),
            pl.BlockSpec((tm, d), lambda i: (i, 0)),
        ],
        out_specs=pl.BlockSpec(memory_space=pl.ANY),
        out_shape=jax.ShapeDtypeStruct((plan["rows_sorted"], d + LANES), F32),
        scratch_shapes=[pltpu.VMEM((2, plan["s_rows"], d + LANES), F32),
                        pltpu.VMEM((te, d + LANES), F32),
                        pltpu.SemaphoreType.DMA((2,)), pltpu.SemaphoreType.DMA(())],
        compiler_params=pltpu.CompilerParams(
            dimension_semantics=("arbitrary",), vmem_limit_bytes=VMEM_LIMIT),
        name="moe_dispatch",
    )(tab, tab, tails, slot_t, route_t, xn)


def _expert_kernel(te_ref, nu_ref, x_ref, wg_ref, wu_ref, wd_ref, out_ref, *, sub):
    j = pl.program_id(0)
    c = pl.program_id(1)

    @pl.when(c == 0)
    def _():
        out_ref[...] = jnp.zeros(out_ref.shape, F32)

    @pl.when(j < nu_ref[0])
    def _():
        d = out_ref.shape[1]
        xb = x_ref[:, 0:d].astype(BF16)
        fc = wg_ref.shape[1]
        y = None
        for lo in range(0, fc, sub):
            hi = min(lo + sub, fc)
            act = (_silu(_dot(xb, wg_ref[:, lo:hi])) * _dot(xb, wu_ref[:, lo:hi])).astype(BF16)
            part = _dot(act, wd_ref[lo:hi, :])
            y = part if y is None else y + part
        out_ref[...] += x_ref[:, d:d + 1] * y


def _experts(xs, tile_expert, n_used, wg, wu, wd, plan):
    te = plan["te"]
    fc = plan["fc"]
    d = xs.shape[1] - LANES
    n_tiles = xs.shape[0] // te
    n_c = wg.shape[2] // fc

    def row(j, c, te_ref, nu_ref):
        return (jnp.minimum(j, nu_ref[0] - 1), 0)

    def chunk(j, c, nu_ref):
        return jnp.where(j < nu_ref[0], c, n_c - 1)

    def w_in(j, c, te_ref, nu_ref):
        return (te_ref[jnp.minimum(j, nu_ref[0] - 1)], 0, chunk(j, c, nu_ref))

    def w_out(j, c, te_ref, nu_ref):
        return (te_ref[jnp.minimum(j, nu_ref[0] - 1)], chunk(j, c, nu_ref), 0)

    grid_spec = pltpu.PrefetchScalarGridSpec(
        num_scalar_prefetch=2,
        grid=(n_tiles, n_c),
        in_specs=[
            pl.BlockSpec((te, d + LANES), row),
            pl.BlockSpec((None, d, fc), w_in),
            pl.BlockSpec((None, d, fc), w_in),
            pl.BlockSpec((None, fc, d), w_out),
        ],
        out_specs=pl.BlockSpec((te, d), lambda j, c, te_ref, nu_ref: (j, 0)),
    )
    return pl.pallas_call(
        functools.partial(_expert_kernel, sub=plan["fsub"]),
        grid_spec=grid_spec,
        out_shape=jax.ShapeDtypeStruct((xs.shape[0], d), F32),
        compiler_params=pltpu.CompilerParams(
            dimension_semantics=("arbitrary", "arbitrary"), vmem_limit_bytes=VMEM_LIMIT),
        name="moe_experts",
    )(tile_expert, n_used, xs, wg, wu, wd)


def _combine_kernel(tabc_ref, tabn_ref, slot_ref, h_ref, g_ref, ys_ref, outp_ref, outs_ref,
                    y_ref, sems, *, tm, n_tiles, n_p_tiles):
    i = pl.program_id(0)

    def piece(slot, local_row, sorted_row, n):
        return pltpu.make_async_copy(
            ys_ref.at[pl.ds(sorted_row, n)], y_ref.at[slot, pl.ds(local_row, n)], sems.at[slot])

    @pl.when(i == 0)
    def _():
        y_ref[...] = jnp.zeros(y_ref.shape, F32)
        _for_each_run(tabc_ref, tm, lambda lr, sr, n: piece(0, lr, sr, n).start())

    @pl.when(i + 1 < n_tiles)
    def _():
        _for_each_run(tabn_ref, tm, lambda lr, sr, n: piece((i + 1) % 2, lr, sr, n).start())

    slot = i % 2
    _for_each_run(tabc_ref, tm, lambda lr, sr, n: piece(slot, lr, sr, n).wait())

    y = y_ref[slot].astype(BF16)
    s_rows = y.shape[0]
    slot_iota = lax.broadcasted_iota(jnp.int32, (tm, s_rows), 1)
    pick = jnp.where((slot_iota == slot_ref[:, 0:1]) | (slot_iota == slot_ref[:, 1:2]),
                     1.0, 0.0).astype(BF16)
    h = h_ref[...] + _dot(pick, y)
    out = _rms(h, g_ref[...])

    @pl.when(i < n_p_tiles)
    def _():
        outp_ref[...] = out

    @pl.when(i >= n_p_tiles)
    def _():
        outs_ref[...] = out


def _combine(h, slot_c, tab, ys, g, plan):
    tm = plan["tm"]
    d = h.shape[1]
    n_tiles = plan["n_main"] // tm
    n_p_tiles = plan["n_p"] // tm
    kern = functools.partial(_combine_kernel, tm=tm, n_tiles=n_tiles, n_p_tiles=n_p_tiles)
    return pl.pallas_call(
        kern,
        grid=(n_tiles,),
        in_specs=[
            pl.BlockSpec((1, 3, N_EXPERTS), lambda i: (i, 0, 0), memory_space=pltpu.SMEM),
            pl.BlockSpec((1, 3, N_EXPERTS), lambda i: (jnp.minimum(i + 1, n_tiles - 1), 0, 0),
                         memory_space=pltpu.SMEM),
            pl.BlockSpec((tm, 2), lambda i: (i, 0)),
            pl.BlockSpec((tm, d), lambda i: (i, 0)),
            _const_spec(g.shape),
            pl.BlockSpec(memory_space=pl.ANY),
        ],
        out_specs=[
            pl.BlockSpec((tm, d), lambda i: (jnp.minimum(i, n_p_tiles - 1), 0)),
            pl.BlockSpec((tm, d), lambda i: (jnp.maximum(i - n_p_tiles, 0), 0)),
        ],
        out_shape=[jax.ShapeDtypeStruct((plan["n_p"], d), F32),
                   jax.ShapeDtypeStruct((plan["n_main"] - plan["n_p"], d), F32)],
        scratch_shapes=[pltpu.VMEM((2, plan["s_rows"], d), F32),
                        pltpu.SemaphoreType.DMA((2,))],
        compiler_params=pltpu.CompilerParams(
            dimension_semantics=("arbitrary",), vmem_limit_bytes=VMEM_LIMIT),
        name="moe_combine_norm",
    )(tab, tab, slot_c, h, g, ys)


def _make_plan(b_p, t_p, b_s, t_s, d_ff_expert):
    tm = 512 if t_p % 512 == 0 and t_s % 512 == 0 else 128
    te = 1024 if tm == 512 else 256
    n_p = b_p * t_p
    n_main = n_p + b_s * t_s
    n_seq = b_p + b_s
    assert n_seq * N_META <= tm and t_p % tm == 0 and t_s % tm == 0
    np_rows = n_main + tm
    rows_p, rows_s = t_p // GRID_W, t_s // GRID_W
    cr = 64 if rows_p % 64 == 0 and rows_s % 64 == 0 else 4 * Q_ROWS
    assert rows_p % cr == 0 and rows_s % cr == 0 and cr % (4 * Q_ROWS) == 0
    fc = 1792 if d_ff_expert % 1792 == 0 else d_ff_expert
    fsub = 2 * MXU_DIM
    run_pad = N_EXPERTS * (SEG_ALIGN - 1)
    s_rows = -(-(2 * tm + run_pad) // SEG_ALIGN) * SEG_ALIGN
    rows_sorted = -(-(2 * np_rows + run_pad * (np_rows // tm)) // te) * te + N_EXPERTS * te
    return dict(tm=tm, te=te, n_p=n_p, n_main=n_main, n_seq=n_seq, np_rows=np_rows, t_p=t_p,
                t_s=t_s, cr=cr, fc=fc, fsub=fsub, s_rows=s_rows, rows_sorted=rows_sorted)


def kernel(x_prompt, x_sample, meta_tokens, norm_mix, norm_ffn, norm_final, even_w_in, pool_w,
           pool_scale, conv_w, even_w_out, ffn_w_gate, ffn_w_up, ffn_w_down, na_w_qkv,
           na_rel_bias, na_meta_bias, na_w_out, moe_w_router, moe_b_router, moe_w_gate,
           moe_w_up, moe_w_down):
    b_p, t_p, d = x_prompt.shape
    b_s, t_s, _ = x_sample.shape
    plan = _make_plan(b_p, t_p, b_s, t_s, moe_w_gate.shape[-1])
    tm, te = plan["tm"], plan["te"]
    n_main, n_seq, np_rows = plan["n_main"], plan["n_seq"], plan["np_rows"]
    n_meta_rows = n_seq * N_META

    xfirst = jnp.concatenate([x_prompt[:, :EXT].reshape(-1, d), x_sample[:, :EXT].reshape(-1, d)])

    row = lambda v: v.reshape(1, -1).astype(F32)
    h = _even_layer(x_prompt.reshape(-1, d), x_sample.reshape(-1, d), meta_tokens, xfirst,
                    row(norm_mix[0]), even_w_in[0].astype(BF16), pool_w[0].astype(BF16),
                    row(pool_scale[0]), conv_w[0], even_w_out[0].astype(BF16), plan)
    h = _ffn_layer(h, row(norm_ffn[0]), ffn_w_gate[0].astype(BF16), ffn_w_up[0].astype(BF16),
                   ffn_w_down[0].astype(BF16), plan)

    width = NA_HEADS * HEAD_DIM
    log2e = float(np.log2(np.e))
    qscale = jnp.concatenate([jnp.full((width,), HEAD_DIM ** -0.5 * log2e, F32),
                              jnp.ones((width,), F32)])
    qk, vt = _qkv_layer(h, row(norm_mix[1]), (na_w_qkv[0][:, :2 * width] * qscale).astype(BF16),
                        na_w_qkv[0][:, 2 * width:].T.astype(BF16), plan)
    vt_meta = vt[:, n_main:n_main + n_meta_rows]
    n_hb = NA_HEADS // HEADS_PER_BLOCK
    o_main, o_meta = _attn_layer(
        qk, vt, vt_meta.T, vt_meta.reshape(width, n_seq, N_META).transpose(1, 0, 2),
        _attn_bias_table(na_rel_bias[0] * log2e, na_meta_bias[0] * log2e),
        jnp.pad((na_meta_bias[0] * log2e).reshape(n_hb, HEADS_PER_BLOCK, 1, N_META),
                ((0, 0), (0, 0), (0, 7), (0, LANES - N_META))), plan)
    o_meta = jnp.concatenate([o_meta, jnp.zeros((tm - n_meta_rows, width), BF16)])

    h, xn, route_t, tile_cnt = _route_layer(
        o_main, o_meta, h, na_w_out[0].astype(BF16), row(norm_ffn[1]),
        moe_w_router[0].T.astype(BF16), moe_b_router[0].reshape(-1, 1).astype(F32), plan)

    cnt = tile_cnt[:, :, 0].astype(jnp.int32)
    seg = (cnt + SEG_ALIGN - 1) // SEG_ALIGN * SEG_ALIGN
    local = jnp.cumsum(seg, axis=1) - seg
    before = jnp.cumsum(seg, axis=0) - seg
    padded = (jnp.sum(seg, axis=0) + te - 1) // te * te
    ends = jnp.cumsum(padded)
    tab = jnp.stack([local, seg, (ends - padded)[None, :] + before], axis=1).astype(jnp.int32)
    experts_t = route_t[0:2].astype(jnp.int32)
    local_tok = jnp.repeat(local, tm, axis=0).T
    pick = experts_t[:, None, :] == jnp.arange(N_EXPERTS)[None, :, None]
    slot_t = jnp.sum(jnp.where(pick, local_tok[None], 0), axis=1) + route_t[2:4].astype(jnp.int32)
    tails = jnp.concatenate([jnp.where(padded > 0, ends - te, -1), ends[-1:] // te])
    tails = tails.astype(jnp.int32)
    n_row_tiles = plan["rows_sorted"] // te
    tile_expert = jnp.minimum(
        jnp.sum(ends[None, :] <= (jnp.arange(n_row_tiles) * te)[:, None], axis=1),
        N_EXPERTS - 1).astype(jnp.int32)
    n_used = (ends[-1:] // te).astype(jnp.int32)

    xs = _dispatch(xn, slot_t, route_t, tab, tails, plan)
    ys = _experts(xs, tile_expert, n_used, moe_w_gate[0].astype(BF16), moe_w_up[0].astype(BF16),
                  moe_w_down[0].astype(BF16), plan)
    y_p, y_s = _combine(h, slot_t.T, tab, ys, row(norm_final), plan)
    return (y_p.reshape(b_p, t_p, d), y_s.reshape(b_s, t_s, d))
```

```python
import functools

import jax
import jax.numpy as jnp
import numpy as np
from jax import lax
from jax.experimental import pallas as pl
from jax.experimental.pallas import tpu as pltpu

F32 = jnp.float32
BF16 = jnp.bfloat16

N_META = 16
GRID_W = 64
RMS_EPS = 1e-6
POOL_WINDOWS = (2, 4, 8, 16)
POOL_GROUP = 128
POOL_WIDTH = 512
CONV_WIDTH = 512
NA_HEADS = 16
HEAD_DIM = 64
NA_WIN_ROWS = 8
NA_WIN_COLS = 16
N_EXPERTS = 8
LANES = 128
MXU_DIM = 256
SEG_ALIGN = 8
HEADS_PER_BLOCK = LANES // HEAD_DIM
HALO_ROWS = NA_WIN_ROWS // 2
Q_ROWS = 4
WIN_ROWS = Q_ROWS + NA_WIN_ROWS
NQ = Q_ROWS * GRID_W
NK = WIN_ROWS * GRID_W
V_ROWS = HEAD_DIM + 16
EXT = 16
VMEM_LIMIT = 56 * 1024 * 1024


def _const_spec(shape):
    nd = len(shape)
    return pl.BlockSpec(shape, lambda *_: (0,) * nd, pipeline_mode=pl.Buffered(1))


def _rms(x, g):
    ms = jnp.mean(x * x, axis=-1, keepdims=True)
    return x * lax.rsqrt(ms + RMS_EPS) * g


def _dot(a, b):
    return jnp.dot(a, b, preferred_element_type=F32)


def _dot_nt(a, b):
    return lax.dot_general(a, b, (((1,), (1,)), ((), ())), preferred_element_type=F32)


def _silu(x):
    return x * jax.nn.sigmoid(x)


def _even_mix(xe, pos, seq_len, g, win_ref, pw_ref, ps, cw, wout_ref):
    n = xe.shape[0]
    xn = _rms(xe, g).astype(BF16)
    z = _dot(xn, win_ref[...])
    a_parts = []
    for gi, k in enumerate(POOL_WINDOWS):
        p = z[:, gi * POOL_GROUP:(gi + 1) * POOL_GROUP]
        s = p
        w = 1
        while w < k:
            s = s + pltpu.roll(s, n - w, 0)
            w *= 2
        left = k // 2
        right = k - 1 - left
        s = pltpu.roll(s, left, 0)
        cnt = jnp.minimum(pos + right + 1, seq_len) - jnp.maximum(pos - left, 0)
        inv = 1.0 / jnp.maximum(cnt, 1).astype(F32)
        pooled = (s * inv - p).astype(BF16)
        a_parts.append(_dot(pooled, pw_ref[gi]))
    a = jnp.concatenate(a_parts, axis=-1) * ps
    gb = z[:, POOL_WIDTH:POOL_WIDTH + CONV_WIDTH]
    gc = z[:, POOL_WIDTH + CONV_WIDTH:POOL_WIDTH + 2 * CONV_WIDTH]
    hc = z[:, POOL_WIDTH + 2 * CONV_WIDTH:]
    u = gc * hc
    conv = pltpu.roll(u, 1, 0) * cw[0:1] + u * cw[1:2] + pltpu.roll(u, n - 1, 0) * cw[2:3]
    c = gb * conv
    return (_dot(a.astype(BF16), wout_ref[0:POOL_WIDTH, :])
            + _dot(c.astype(BF16), wout_ref[POOL_WIDTH:, :]))


def _even_kernel(xp_ref, xpp_ref, xpn_ref, xs_ref, xsp_ref, xsn_ref, meta_ref, first_ref, g_ref,
                 win_ref, pw_ref, ps_ref, cw_ref, wout_ref, out_ref, xe_ref, *, tm, n_tiles,
                 tiles_p, n_p_tiles, tiles_s, t_p, t_s, n_seq):
    i = pl.program_id(0)
    weights = (g_ref[...], win_ref, pw_ref, ps_ref[...], cw_ref[...], wout_ref)
    d = xe_ref.shape[1]

    def token_tile(tile_ref, prev_ref, next_ref, tin, last, t_seq):
        n = tm + 2 * EXT

        @pl.when(tin == 0)
        def _():
            xe_ref[0:EXT, :] = meta_ref[...]

        @pl.when(tin != 0)
        def _():
            xe_ref[0:EXT, :] = prev_ref[...]

        xe_ref[EXT:EXT + tm, :] = tile_ref[...]

        @pl.when(tin == last)
        def _():
            xe_ref[EXT + tm:n, :] = jnp.zeros((EXT, d), F32)

        @pl.when(tin != last)
        def _():
            xe_ref[EXT + tm:n, :] = next_ref[...]

        pos = lax.broadcasted_iota(jnp.int32, (n, 1), 0) + tin * tm
        y = _even_mix(xe_ref[0:n, :], pos, t_seq + N_META, *weights)
        out_ref[...] = xe_ref[EXT:EXT + tm, :] + y[EXT:EXT + tm]

    @pl.when(i < n_p_tiles)
    def _prompt():
        token_tile(xp_ref, xpp_ref, xpn_ref, i % tiles_p, tiles_p - 1, t_p)

    @pl.when((i >= n_p_tiles) & (i < n_tiles))
    def _sample():
        token_tile(xs_ref, xsp_ref, xsn_ref, (i - n_p_tiles) % tiles_s, tiles_s - 1, t_s)

    @pl.when(i == n_tiles)
    def _meta():
        grp = 3 * EXT
        n = grp * n_seq
        for b in range(n_seq):
            xe_ref[grp * b:grp * b + EXT, :] = jnp.zeros((EXT, d), F32)
            xe_ref[grp * b + EXT:grp * b + 2 * EXT, :] = meta_ref[...]
            xe_ref[grp * b + 2 * EXT:grp * (b + 1), :] = first_ref[EXT * b:EXT * (b + 1), :]
        pos = lax.broadcasted_iota(jnp.int32, (n, 1), 0) % grp - EXT
        y = _even_mix(xe_ref[0:n, :], pos, jnp.int32(1 << 30), *weights)
        out_ref[...] = jnp.zeros(out_ref.shape, F32)
        for b in range(n_seq):
            out_ref[EXT * b:EXT * (b + 1), :] = meta_ref[...] + y[grp * b + EXT:grp * b + 2 * EXT]


def _even_layer(xp, xs, meta, xfirst, g, win, pw, ps, cw, wout, plan):
    tm = plan["tm"]
    d = xp.shape[1]
    n_tiles = plan["n_main"] // tm
    n_p_tiles = plan["n_p"] // tm
    n_s_tiles = n_tiles - n_p_tiles
    blk = tm // EXT
    kern = functools.partial(
        _even_kernel, tm=tm, n_tiles=n_tiles, tiles_p=plan["t_p"] // tm, n_p_tiles=n_p_tiles,
        tiles_s=plan["t_s"] // tm, t_p=plan["t_p"], t_s=plan["t_s"], n_seq=plan["n_seq"])
    xe_rows = max(tm + 2 * EXT, 3 * EXT * plan["n_seq"])

    def specs(first_tile, count):
        last = count * blk - 1
        tile = lambda i: jnp.clip(i - first_tile, 0, count - 1)
        return [
            pl.BlockSpec((tm, d), lambda i: (tile(i), 0)),
            pl.BlockSpec((EXT, d), lambda i: (jnp.clip(tile(i) * blk - 1, 0, last), 0)),
            pl.BlockSpec((EXT, d), lambda i: (jnp.clip((tile(i) + 1) * blk, 0, last), 0)),
        ]

    return pl.pallas_call(
        kern,
        grid=(n_tiles + 1,),
        in_specs=specs(0, n_p_tiles) + specs(n_p_tiles, n_s_tiles) + [
            _const_spec(meta.shape), _const_spec(xfirst.shape), _const_spec(g.shape),
            _const_spec(win.shape), _const_spec(pw.shape), _const_spec(ps.shape),
            _const_spec(cw.shape), _const_spec(wout.shape),
        ],
        out_specs=pl.BlockSpec((tm, d), lambda i: (i, 0)),
        out_shape=jax.ShapeDtypeStruct((plan["np_rows"], d), F32),
        scratch_shapes=[pltpu.VMEM((xe_rows, d), F32)],
        compiler_params=pltpu.CompilerParams(
            dimension_semantics=("arbitrary",), vmem_limit_bytes=VMEM_LIMIT),
        name="even_mixer",
    )(xp, xp, xp, xs, xs, xs, meta, xfirst, g, win, pw, ps, cw, wout)


def _ffn_kernel(x_ref, g_ref, wg_ref, wu_ref, wd_ref, out_ref):
    x = x_ref[...]
    xn = _rms(x, g_ref[...]).astype(BF16)
    act = (_silu(_dot(xn, wg_ref[...])) * _dot(xn, wu_ref[...])).astype(BF16)
    out_ref[...] = x + _dot(act, wd_ref[...])


def _ffn_layer(h, g, wg, wu, wd, plan):
    tm = plan["tm"]
    d = h.shape[1]
    return pl.pallas_call(
        _ffn_kernel,
        grid=(h.shape[0] // tm,),
        in_specs=[pl.BlockSpec((tm, d), lambda i: (i, 0)), _const_spec(g.shape),
                  _const_spec(wg.shape), _const_spec(wu.shape), _const_spec(wd.shape)],
        out_specs=pl.BlockSpec((tm, d), lambda i: (i, 0)),
        out_shape=jax.ShapeDtypeStruct(h.shape, F32),
        compiler_params=pltpu.CompilerParams(
            dimension_semantics=("arbitrary",), vmem_limit_bytes=VMEM_LIMIT),
        name="dense_swiglu",
    )(h, g, wg, wu, wd)


def _qkv_kernel(x_ref, g_ref, wqk_ref, wvt_ref, qk_ref, vt_ref):
    xn = _rms(x_ref[...], g_ref[...]).astype(BF16)
    qk_ref[...] = _dot(xn, wqk_ref[...]).astype(BF16)
    vt_ref[...] = _dot_nt(wvt_ref[...], xn).astype(BF16)


def _qkv_layer(h, g, wqk, wvt, plan):
    tm = plan["tm"]
    d = h.shape[1]
    return pl.pallas_call(
        _qkv_kernel,
        grid=(h.shape[0] // tm,),
        in_specs=[pl.BlockSpec((tm, d), lambda i: (i, 0)), _const_spec(g.shape),
                  _const_spec(wqk.shape), _const_spec(wvt.shape)],
        out_specs=[pl.BlockSpec((tm, wqk.shape[1]), lambda i: (i, 0)),
                   pl.BlockSpec((wvt.shape[0], tm), lambda i: (0, i))],
        out_shape=[jax.ShapeDtypeStruct((h.shape[0], wqk.shape[1]), BF16),
                   jax.ShapeDtypeStruct((wvt.shape[0], h.shape[0]), BF16)],
        compiler_params=pltpu.CompilerParams(
            dimension_semantics=("arbitrary",), vmem_limit_bytes=VMEM_LIMIT),
        name="qkv_proj",
    )(h, g, wqk, wvt)


def _attn_kernel(q_ref, kc_ref, kp_ref, kn_ref, vc_ref, vp_ref, vn_ref, qm_ref, km_ref, vm_ref,
                 vmt_ref, tab_ref, mb_ref, o_ref, om_ref, kx_ref, vx_ref, s_ref, *, cr, chunks_p,
                 n_p_chunks, chunks_s, rows_p, rows_s):
    ci = pl.program_id(1)
    is_p = ci < n_p_chunks
    cin = jnp.where(is_p, ci % chunks_p, (ci - n_p_chunks) % chunks_s)
    rows = jnp.where(is_p, rows_p, rows_s)
    halo = HALO_ROWS * GRID_W
    ct = cr * GRID_W

    kx_ref[0:halo, :] = kp_ref[...]
    kx_ref[halo:halo + ct, :] = kc_ref[...]
    kx_ref[halo + ct:, :] = kn_ref[...]
    ones_rows = V_ROWS - HEAD_DIM
    vmt_aug = []
    for hh in range(HEADS_PER_BLOCK):
        hs = slice(hh * HEAD_DIM, (hh + 1) * HEAD_DIM)
        vx_ref[hh, 0:HEAD_DIM, 0:halo] = vp_ref[hs, :]
        vx_ref[hh, 0:HEAD_DIM, halo:halo + ct] = vc_ref[hs, :]
        vx_ref[hh, 0:HEAD_DIM, halo + ct:] = vn_ref[hs, :]
        vx_ref[hh, HEAD_DIM:, :] = jnp.ones((ones_rows, ct + 2 * halo), BF16)
        vmt_aug.append(jnp.concatenate(
            [vmt_ref[hs, :], jnp.ones((ones_rows, N_META), BF16)], axis=0))

    lane = lax.broadcasted_iota(jnp.int32, (1, LANES), 1)
    head_masks = [(lane // HEAD_DIM) == hh for hh in range(HEADS_PER_BLOCK)]
    km = km_ref[...]
    r0c = cin * cr

    n_blocks = cr // Q_ROWS

    def scores(bi, slot):
        r0 = r0c + bi * Q_ROWS
        edge = jnp.where(r0 == 0, 0, jnp.where(r0 == rows - Q_ROWS, 2, 1))
        off = bi * NQ
        q = q_ref[off:off + NQ, :]
        qq = jnp.concatenate(
            [jnp.where(head_masks[hh], q, jnp.zeros_like(q)) for hh in range(HEADS_PER_BLOCK)],
            axis=0)
        s_ref[slot, 0:NK, :] = _dot_nt(kx_ref[off:off + NK, :], qq) + tab_ref[edge, 0:NK, :]
        s_ref[slot, NK:NK + N_META, :] = _dot_nt(km, qq) + tab_ref[edge, NK:NK + N_META, :]

    def softmax_pv(bi, slot):
        off = bi * NQ
        s = s_ref[slot]
        pb = jnp.exp2(s - jnp.max(s, axis=0, keepdims=True)).astype(BF16)
        outs = []
        for hh in range(HEADS_PER_BLOCK):
            qs = slice(hh * NQ, (hh + 1) * NQ)
            oa = (_dot(vx_ref[hh, :, off:off + NK], pb[0:NK, qs])
                  + _dot(vmt_aug[hh], pb[NK:NK + N_META, qs]))
            outs.append(oa[0:HEAD_DIM] / oa[HEAD_DIM:HEAD_DIM + 1])
        o_ref[off:off + NQ, :] = jnp.concatenate(outs, axis=0).T.astype(BF16)

    scores(0, 0)
    scores(1, 1)
    for bi in range(n_blocks):
        if bi + 2 < n_blocks:
            scores(bi + 2, (bi + 2) % 4)
        softmax_pv(bi, bi % 4)

    qm = qm_ref[...]
    vm = vm_ref[...]
    out = None
    for hh in range(HEADS_PER_BLOCK):
        qh = jnp.where(head_masks[hh], qm, jnp.zeros_like(qm))
        sm = _dot_nt(qh, km) + mb_ref[hh, 0:1, 0:N_META]
        pm = jnp.exp2(sm - jnp.max(sm, axis=-1, keepdims=True))
        o = _dot(pm.astype(BF16), vm) / jnp.sum(pm, axis=-1, keepdims=True)
        out = o if out is None else jnp.where(head_masks[hh], o, out)
    om_ref[...] = out.astype(BF16)


def _attn_layer(qk, vt, vm, vmt, bias_tab, meta_bias, plan):
    cr = plan["cr"]
    ct = cr * GRID_W
    halo = HALO_ROWS * GRID_W
    n_main = plan["n_main"]
    n_chunks = n_main // ct
    n_hb = NA_HEADS // HEADS_PER_BLOCK
    hb = ct // halo
    last_halo = n_main // halo - 1
    meta_blk0 = n_main // N_META
    n_p_chunks = plan["n_p"] // ct
    chunks_p = plan["t_p"] // ct
    chunks_s = plan["t_s"] // ct
    b_p = plan["n_p"] // plan["t_p"]

    def seq_of(ci):
        return jnp.where(ci < n_p_chunks, ci // chunks_p, b_p + (ci - n_p_chunks) // chunks_s)

    def cur(off):
        return pl.BlockSpec((ct, LANES), lambda h, c: (c, off + h))

    def prev(off):
        return pl.BlockSpec((halo, LANES), lambda h, c: (jnp.maximum(c * hb - 1, 0), off + h))

    def nxt(off):
        return pl.BlockSpec((halo, LANES),
                            lambda h, c: (jnp.minimum((c + 1) * hb, last_halo), off + h))

    def meta(off):
        return pl.BlockSpec((N_META, LANES), lambda h, c: (meta_blk0 + seq_of(c), off + h))

    def vt_cur():
        return pl.BlockSpec((LANES, ct), lambda h, c: (h, c))

    def vt_prev():
        return pl.BlockSpec((LANES, halo), lambda h, c: (h, jnp.maximum(c * hb - 1, 0)))

    def vt_next():
        return pl.BlockSpec((LANES, halo), lambda h, c: (h, jnp.minimum((c + 1) * hb, last_halo)))

    kern = functools.partial(
        _attn_kernel, cr=cr, chunks_p=chunks_p, n_p_chunks=n_p_chunks, chunks_s=chunks_s,
        rows_p=plan["t_p"] // GRID_W, rows_s=plan["t_s"] // GRID_W)
    width = NA_HEADS * HEAD_DIM
    return pl.pallas_call(
        kern,
        grid=(n_hb, n_chunks),
        in_specs=[
            cur(0), cur(n_hb), prev(n_hb), nxt(n_hb), vt_cur(), vt_prev(), vt_next(),
            meta(0), meta(n_hb),
            pl.BlockSpec((N_META, LANES), lambda h, c: (seq_of(c), h)),
            pl.BlockSpec((None, LANES, N_META), lambda h, c: (seq_of(c), h, 0)),
            pl.BlockSpec((None,) + bias_tab.shape[1:], lambda h, c: (h, 0, 0, 0)),
            pl.BlockSpec((None, HEADS_PER_BLOCK, 8, LANES), lambda h, c: (h, 0, 0, 0)),
        ],
        out_specs=[
            pl.BlockSpec((ct, LANES), lambda h, c: (c, h)),
            pl.BlockSpec((N_META, LANES), lambda h, c: (seq_of(c), h)),
        ],
        out_shape=[jax.ShapeDtypeStruct((n_main, width), BF16),
                   jax.ShapeDtypeStruct((plan["n_seq"] * N_META, width), BF16)],
        scratch_shapes=[pltpu.VMEM((ct + 2 * halo, LANES), BF16),
                        pltpu.VMEM((HEADS_PER_BLOCK, V_ROWS, ct + 2 * halo), BF16),
                        pltpu.VMEM((4, NK + N_META, HEADS_PER_BLOCK * NQ), F32)],
        compiler_params=pltpu.CompilerParams(
            dimension_semantics=("arbitrary", "arbitrary"), vmem_limit_bytes=VMEM_LIMIT),
        name="neighbourhood_attention",
    )(qk, qk, qk, qk, vt, vt, vt, qk, qk, vm, vmt, bias_tab, meta_bias)


def _attn_bias_table(rel_bias, meta_bias):
    n_heads, n_row_off, n_col_off = rel_bias.shape
    half = n_col_off // 2
    ring = jnp.concatenate([
        rel_bias[:, :, half::-1],
        jnp.zeros((n_heads, n_row_off, 2 * GRID_W - n_col_off), F32),
        rel_bias[:, :, :half:-1]], axis=-1)
    toe = jnp.tile(ring, (1, 1, GRID_W))[:, :, :GRID_W * (2 * GRID_W - 1)]
    toe = toe.reshape(n_heads, n_row_off, GRID_W, 2 * GRID_W - 1)[..., :GRID_W]
    qc = np.arange(GRID_W)[None, :]
    kc = np.arange(GRID_W)[:, None]
    cstart = np.clip(qc - NA_WIN_COLS // 2, 0, GRID_W - NA_WIN_COLS)
    col_ok = (kc >= cstart) & (kc < cstart + NA_WIN_COLS)
    toe = jnp.where(col_ok[None, None], toe, -1e30)
    n_hb = n_heads // HEADS_PER_BLOCK
    toe = toe.reshape(n_hb, HEADS_PER_BLOCK, n_row_off, GRID_W, GRID_W)
    per_a = jnp.stack([toe[:, :, HALO_ROWS - 1 - a:HALO_ROWS - 1 - a + WIN_ROWS]
                       for a in range(Q_ROWS)], axis=2)
    per_a = jnp.transpose(per_a, (0, 3, 4, 1, 2, 5))
    a = np.arange(Q_ROWS)[None, :]
    b = np.arange(WIN_ROWS)[:, None]
    row_ok = np.stack([(b >= HALO_ROWS) & (a >= 0),
                       (b - a >= 0) & (b - a < NA_WIN_ROWS),
                       (b < NA_WIN_ROWS) & (a >= 0)])
    tab = jnp.where(row_ok[None, :, :, None, None, :, None], per_a[:, None], -1e30)
    tab = tab.reshape(n_hb, 3, NK, HEADS_PER_BLOCK * NQ)
    meta = jnp.transpose(meta_bias.reshape(n_hb, HEADS_PER_BLOCK, N_META), (0, 2, 1))
    meta = jnp.broadcast_to(meta[:, None, :, :, None], (n_hb, 3, N_META, HEADS_PER_BLOCK, NQ))
    meta = meta.reshape(n_hb, 3, N_META, HEADS_PER_BLOCK * NQ)
    return jnp.concatenate([tab, meta], axis=2).astype(F32)


def _route_kernel(o_ref, om_ref, h_ref, wout_ref, g_ref, wrt_ref, brc_ref, triu_ref,
                  h_out, xn_out, route_out, cnt_out, *, n_tiles):
    i = pl.program_id(0)

    def body(o):
        h = h_ref[...] + _dot(o, wout_ref[...])
        h_out[...] = h
        xn = _rms(h, g_ref[...]).astype(BF16)
        xn_out[...] = xn
        logits = _dot_nt(wrt_ref[...], xn) + brc_ref[...]
        tm = logits.shape[1]
        sub = lax.broadcasted_iota(jnp.int32, (N_EXPERTS, tm), 0)
        m1 = jnp.max(logits, axis=0, keepdims=True)
        i1 = jnp.min(jnp.where(logits == m1, sub, N_EXPERTS), axis=0, keepdims=True)
        oh1 = sub == i1
        rest = jnp.where(oh1, -jnp.inf, logits)
        m2 = jnp.max(rest, axis=0, keepdims=True)
        i2 = jnp.min(jnp.where(rest == m2, sub, N_EXPERTS), axis=0, keepdims=True)
        oh2 = sub == i2
        t = jnp.exp(m2 - m1)
        g1 = 1.0 / (1.0 + t)
        g2 = t * g1
        oh = jnp.where(oh1 | oh2, 1.0, 0.0)
        cum = _dot(oh.astype(BF16), triu_ref[...])
        lr1 = jnp.sum(jnp.where(oh1, cum, 0.0), axis=0, keepdims=True)
        lr2 = jnp.sum(jnp.where(oh2, cum, 0.0), axis=0, keepdims=True)
        route = jnp.zeros((N_EXPERTS, tm), F32)
        for k, val in enumerate([i1.astype(F32), i2.astype(F32), lr1, lr2, g1, g2]):
            route = jnp.where(sub == k, val, route)
        route_out[...] = route
        cnt_out[0] = jnp.broadcast_to(jnp.sum(oh, axis=1, keepdims=True), (N_EXPERTS, LANES))

    @pl.when(i < n_tiles)
    def _():
        body(o_ref[...])

    @pl.when(i == n_tiles)
    def _():
        body(om_ref[...])


def _route_layer(o_main, o_meta, h, wout, g, wrt, brc, plan):
    tm = plan["tm"]
    d = h.shape[1]
    n_tiles = plan["n_main"] // tm
    triu = (jnp.arange(tm)[:, None] < jnp.arange(tm)[None, :]).astype(BF16)
    kern = functools.partial(_route_kernel, n_tiles=n_tiles)
    rows = (n_tiles + 1) * tm
    return pl.pallas_call(
        kern,
        grid=(n_tiles + 1,),
        in_specs=[
            pl.BlockSpec((tm, d), lambda i: (jnp.minimum(i, n_tiles - 1), 0)),
            _const_spec(o_meta.shape),
            pl.BlockSpec((tm, d), lambda i: (i, 0)),
            _const_spec(wout.shape), _const_spec(g.shape), _const_spec(wrt.shape),
            _const_spec(brc.shape), _const_spec(triu.shape),
        ],
        out_specs=[
            pl.BlockSpec((tm, d), lambda i: (i, 0)),
            pl.BlockSpec((tm, d), lambda i: (i, 0)),
            pl.BlockSpec((N_EXPERTS, tm), lambda i: (0, i)),
            pl.BlockSpec((1, N_EXPERTS, LANES), lambda i: (i, 0, 0)),
        ],
        out_shape=[jax.ShapeDtypeStruct((rows, d), F32),
                   jax.ShapeDtypeStruct((rows, d), BF16),
                   jax.ShapeDtypeStruct((N_EXPERTS, rows), F32),
                   jax.ShapeDtypeStruct((n_tiles + 1, N_EXPERTS, LANES), F32)],
        compiler_params=pltpu.CompilerParams(
            dimension_semantics=("arbitrary",), vmem_limit_bytes=VMEM_LIMIT),
        name="outproj_router",
    )(o_main, o_meta, h, wout, g, wrt, brc, triu)


def _for_each_run(tab_ref, tm, fn):
    for e in range(N_EXPERTS):
        local, length, sorted_row = tab_ref[0, 0, e], tab_ref[0, 1, e], tab_ref[0, 2, e]
        n = tm
        while n >= SEG_ALIGN:
            @pl.when((length & n) != 0)
            def _(n=n):
                done = (length // (2 * n)) * (2 * n)
                fn(pl.multiple_of(local + done, SEG_ALIGN),
                   pl.multiple_of(sorted_row + done, SEG_ALIGN), n)
            n //= 2


def _dispatch_kernel(tab_ref, tabp_ref, tail_ref, slot_ref, route_ref, x_ref, xs_ref, stage_ref,
                     zero_ref, sems, zsem, *, tm, te):
    i = pl.program_id(0)
    d = x_ref.shape[1]
    slot = i % 2

    @pl.when(i == 0)
    def _():
        zero_ref[...] = jnp.zeros(zero_ref.shape, F32)

        def zero_tile(start):
            cp = pltpu.make_async_copy(
                zero_ref, xs_ref.at[pl.ds(pl.multiple_of(start, te), te)], zsem)
            cp.start()
            cp.wait()

        for e in range(N_EXPERTS):
            @pl.when(tail_ref[e] >= 0)
            def _():
                zero_tile(tail_ref[e])

        def zero_unused(u, carry):
            zero_tile(u * te)
            return carry

        lax.fori_loop(tail_ref[N_EXPERTS], xs_ref.shape[0] // te, zero_unused, 0)

    s_rows = stage_ref.shape[1]
    slot_iota = lax.broadcasted_iota(jnp.int32, (s_rows, tm), 0)
    first = slot_iota == slot_ref[0:1, :]
    second = slot_iota == slot_ref[1:2, :]
    perm = jnp.where(first | second, 1.0, 0.0).astype(BF16)
    stage_ref[slot, :, 0:d] = _dot(perm, x_ref[...])
    gate = jnp.sum(jnp.where(first, route_ref[4:5, :], 0.0)
                   + jnp.where(second, route_ref[5:6, :], 0.0), axis=1, keepdims=True)
    stage_ref[slot, :, d:] = jnp.broadcast_to(gate, (s_rows, LANES))

    def piece(buf, local_row, sorted_row, n):
        return pltpu.make_async_copy(
            stage_ref.at[buf, pl.ds(local_row, n)], xs_ref.at[pl.ds(sorted_row, n)],
            sems.at[buf])

    _for_each_run(tab_ref, tm, lambda lr, sr, n: piece(slot, lr, sr, n).start())

    @pl.when(i > 0)
    def _():
        _for_each_run(tabp_ref, tm, lambda lr, sr, n: piece(1 - slot, lr, sr, n).wait())

    @pl.when(i == pl.num_programs(0) - 1)
    def _():
        _for_each_run(tab_ref, tm, lambda lr, sr, n: piece(slot, lr, sr, n).wait())


def _dispatch(xn, slot_t, route_t, tab, tails, plan):
    tm = plan["tm"]
    te = plan["te"]
    d = xn.shape[1]
    kern = functools.partial(_dispatch_kernel, tm=tm, te=te)
    return pl.pallas_call(
        kern,
        grid=(xn.shape[0] // tm,),
        in_specs=[
            pl.BlockSpec((1, 3, N_EXPERTS), lambda i: (i, 0, 0), memory_space=pltpu.SMEM),
            pl.BlockSpec((1, 3, N_EXPERTS), lambda i: (jnp.maximum(i - 1, 0), 0, 0),
                         memory_space=pltpu.SMEM),
            pl.BlockSpec(memory_space=pltpu.SMEM),
            pl.BlockSpec((2, tm), lambda i: (0, i)),
            pl.BlockSpec((N_EXPERTS, tm), lambda i: (0, i)),
            pl.BlockSpec((tm, d), lambda i: (i, 0)),
        ],
        out_specs=pl.BlockSpec(memory_space=pl.ANY),
        out_shape=jax.ShapeDtypeStruct((plan["rows_sorted"], d + LANES), F32),
        scratch_shapes=[pltpu.VMEM((2, plan["s_rows"], d + LANES), F32),
                        pltpu.VMEM((te, d + LANES), F32),
                        pltpu.SemaphoreType.DMA((2,)), pltpu.SemaphoreType.DMA(())],
        compiler_params=pltpu.CompilerParams(
            dimension_semantics=("arbitrary",), vmem_limit_bytes=VMEM_LIMIT),
        name="moe_dispatch",
    )(tab, tab, tails, slot_t, route_t, xn)


def _expert_kernel(te_ref, nu_ref, x_ref, wg_ref, wu_ref, wd_ref, out_ref, *, sub):
    j = pl.program_id(0)
    c = pl.program_id(1)

    @pl.when(c == 0)
    def _():
        out_ref[...] = jnp.zeros(out_ref.shape, F32)

    @pl.when(j < nu_ref[0])
    def _():
        d = out_ref.shape[1]
        xb = x_ref[:, 0:d].astype(BF16)
        fc = wg_ref.shape[1]
        y = None
        for lo in range(0, fc, sub):
            hi = min(lo + sub, fc)
            act = (_silu(_dot(xb, wg_ref[:, lo:hi])) * _dot(xb, wu_ref[:, lo:hi])).astype(BF16)
            part = _dot(act, wd_ref[lo:hi, :])
            y = part if y is None else y + part
        out_ref[...] += x_ref[:, d:d + 1] * y


def _experts(xs, tile_expert, n_used, wg, wu, wd, plan):
    te = plan["te"]
    fc = plan["fc"]
    d = xs.shape[1] - LANES
    n_tiles = xs.shape[0] // te
    n_c = wg.shape[2] // fc

    def row(j, c, te_ref, nu_ref):
        return (jnp.minimum(j, nu_ref[0] - 1), 0)

    def chunk(j, c, nu_ref):
        return jnp.where(j < nu_ref[0], c, n_c - 1)

    def w_in(j, c, te_ref, nu_ref):
        return (te_ref[jnp.minimum(j, nu_ref[0] - 1)], 0, chunk(j, c, nu_ref))

    def w_out(j, c, te_ref, nu_ref):
        return (te_ref[jnp.minimum(j, nu_ref[0] - 1)], chunk(j, c, nu_ref), 0)

    grid_spec = pltpu.PrefetchScalarGridSpec(
        num_scalar_prefetch=2,
        grid=(n_tiles, n_c),
        in_specs=[
            pl.BlockSpec((te, d + LANES), row),
            pl.BlockSpec((None, d, fc), w_in),
            pl.BlockSpec((None, d, fc), w_in),
            pl.BlockSpec((None, fc, d), w_out),
        ],
        out_specs=pl.BlockSpec((te, d), lambda j, c, te_ref, nu_ref: (j, 0)),
    )
    return pl.pallas_call(
        functools.partial(_expert_kernel, sub=plan["fsub"]),
        grid_spec=grid_spec,
        out_shape=jax.ShapeDtypeStruct((xs.shape[0], d), F32),
        compiler_params=pltpu.CompilerParams(
            dimension_semantics=("arbitrary", "arbitrary"), vmem_limit_bytes=VMEM_LIMIT),
        name="moe_experts",
    )(tile_expert, n_used, xs, wg, wu, wd)


def _combine_kernel(tabc_ref, tabn_ref, slot_ref, h_ref, g_ref, ys_ref, outp_ref, outs_ref,
                    y_ref, sems, *, tm, n_tiles, n_p_tiles):
    i = pl.program_id(0)

    def piece(slot, local_row, sorted_row, n):
        return pltpu.make_async_copy(
            ys_ref.at[pl.ds(sorted_row, n)], y_ref.at[slot, pl.ds(local_row, n)], sems.at[slot])

    @pl.when(i == 0)
    def _():
        y_ref[...] = jnp.zeros(y_ref.shape, F32)
        _for_each_run(tabc_ref, tm, lambda lr, sr, n: piece(0, lr, sr, n).start())

    @pl.when(i + 1 < n_tiles)
    def _():
        _for_each_run(tabn_ref, tm, lambda lr, sr, n: piece((i + 1) % 2, lr, sr, n).start())

    slot = i % 2
    _for_each_run(tabc_ref, tm, lambda lr, sr, n: piece(slot, lr, sr, n).wait())

    y = y_ref[slot].astype(BF16)
    s_rows = y.shape[0]
    slot_iota = lax.broadcasted_iota(jnp.int32, (tm, s_rows), 1)
    pick = jnp.where((slot_iota == slot_ref[:, 0:1]) | (slot_iota == slot_ref[:, 1:2]),
                     1.0, 0.0).astype(BF16)
    h = h_ref[...] + _dot(pick, y)
    out = _rms(h, g_ref[...])

    @pl.when(i < n_p_tiles)
    def _():
        outp_ref[...] = out

    @pl.when(i >= n_p_tiles)
    def _():
        outs_ref[...] = out


def _combine(h, slot_c, tab, ys, g, plan):
    tm = plan["tm"]
    d = h.shape[1]
    n_tiles = plan["n_main"] // tm
    n_p_tiles = plan["n_p"] // tm
    kern = functools.partial(_combine_kernel, tm=tm, n_tiles=n_tiles, n_p_tiles=n_p_tiles)
    return pl.pallas_call(
        kern,
        grid=(n_tiles,),
        in_specs=[
            pl.BlockSpec((1, 3, N_EXPERTS), lambda i: (i, 0, 0), memory_space=pltpu.SMEM),
            pl.BlockSpec((1, 3, N_EXPERTS), lambda i: (jnp.minimum(i + 1, n_tiles - 1), 0, 0),
                         memory_space=pltpu.SMEM),
            pl.BlockSpec((tm, 2), lambda i: (i, 0)),
            pl.BlockSpec((tm, d), lambda i: (i, 0)),
            _const_spec(g.shape),
            pl.BlockSpec(memory_space=pl.ANY),
        ],
        out_specs=[
            pl.BlockSpec((tm, d), lambda i: (jnp.minimum(i, n_p_tiles - 1), 0)),
            pl.BlockSpec((tm, d), lambda i: (jnp.maximum(i - n_p_tiles, 0), 0)),
        ],
        out_shape=[jax.ShapeDtypeStruct((plan["n_p"], d), F32),
                   jax.ShapeDtypeStruct((plan["n_main"] - plan["n_p"], d), F32)],
        scratch_shapes=[pltpu.VMEM((2, plan["s_rows"], d), F32),
                        pltpu.SemaphoreType.DMA((2,))],
        compiler_params=pltpu.CompilerParams(
            dimension_semantics=("arbitrary",), vmem_limit_bytes=VMEM_LIMIT),
        name="moe_combine_norm",
    )(tab, tab, slot_c, h, g, ys)


def _make_plan(b_p, t_p, b_s, t_s, d_ff_expert):
    tm = 512 if t_p % 512 == 0 and t_s % 512 == 0 else 128
    te = 1024 if tm == 512 else 256
    n_p = b_p * t_p
    n_main = n_p + b_s * t_s
    n_seq = b_p + b_s
    assert n_seq * N_META <= tm and t_p % tm == 0 and t_s % tm == 0
    np_rows = n_main + tm
    rows_p, rows_s = t_p // GRID_W, t_s // GRID_W
    cr = 64 if rows_p % 64 == 0 and rows_s % 64 == 0 else 4 * Q_ROWS
    assert rows_p % cr == 0 and rows_s % cr == 0 and cr % (4 * Q_ROWS) == 0
    fc = 1792 if d_ff_expert % 1792 == 0 else d_ff_expert
    fsub = 2 * MXU_DIM
    run_pad = N_EXPERTS * (SEG_ALIGN - 1)
    s_rows = -(-(2 * tm + run_pad) // SEG_ALIGN) * SEG_ALIGN
    rows_sorted = -(-(2 * np_rows + run_pad * (np_rows // tm)) // te) * te + N_EXPERTS * te
    return dict(tm=tm, te=te, n_p=n_p, n_main=n_main, n_seq=n_seq, np_rows=np_rows, t_p=t_p,
                t_s=t_s, cr=cr, fc=fc, fsub=fsub, s_rows=s_rows, rows_sorted=rows_sorted)


def kernel(x_prompt, x_sample, meta_tokens, norm_mix, norm_ffn, norm_final, even_w_in, pool_w,
           pool_scale, conv_w, even_w_out, ffn_w_gate, ffn_w_up, ffn_w_down, na_w_qkv,
           na_rel_bias, na_meta_bias, na_w_out, moe_w_router, moe_b_router, moe_w_gate,
           moe_w_up, moe_w_down):
    b_p, t_p, d = x_prompt.shape
    b_s, t_s, _ = x_sample.shape
    plan = _make_plan(b_p, t_p, b_s, t_s, moe_w_gate.shape[-1])
    tm, te = plan["tm"], plan["te"]
    n_main, n_seq, np_rows = plan["n_main"], plan["n_seq"], plan["np_rows"]
    n_meta_rows = n_seq * N_META

    xfirst = jnp.concatenate([x_prompt[:, :EXT].reshape(-1, d), x_sample[:, :EXT].reshape(-1, d)])

    row = lambda v: v.reshape(1, -1).astype(F32)
    h = _even_layer(x_prompt.reshape(-1, d), x_sample.reshape(-1, d), meta_tokens, xfirst,
                    row(norm_mix[0]), even_w_in[0].astype(BF16), pool_w[0].astype(BF16),
                    row(pool_scale[0]), conv_w[0], even_w_out[0].astype(BF16), plan)
    h = _ffn_layer(h, row(norm_ffn[0]), ffn_w_gate[0].astype(BF16), ffn_w_up[0].astype(BF16),
                   ffn_w_down[0].astype(BF16), plan)

    width = NA_HEADS * HEAD_DIM
    log2e = float(np.log2(np.e))
    qscale = jnp.concatenate([jnp.full((width,), HEAD_DIM ** -0.5 * log2e, F32),
                              jnp.ones((width,), F32)])
    qk, vt = _qkv_layer(h, row(norm_mix[1]), (na_w_qkv[0][:, :2 * width] * qscale).astype(BF16),
                        na_w_qkv[0][:, 2 * width:].T.astype(BF16), plan)
    vt_meta = vt[:, n_main:n_main + n_meta_rows]
    n_hb = NA_HEADS // HEADS_PER_BLOCK
    o_main, o_meta = _attn_layer(
        qk, vt, vt_meta.T, vt_meta.reshape(width, n_seq, N_META).transpose(1, 0, 2),
        _attn_bias_table(na_rel_bias[0] * log2e, na_meta_bias[0] * log2e),
        jnp.pad((na_meta_bias[0] * log2e).reshape(n_hb, HEADS_PER_BLOCK, 1, N_META),
                ((0, 0), (0, 0), (0, 7), (0, LANES - N_META))), plan)
    o_meta = jnp.concatenate([o_meta, jnp.zeros((tm - n_meta_rows, width), BF16)])

    h, xn, route_t, tile_cnt = _route_layer(
        o_main, o_meta, h, na_w_out[0].astype(BF16), row(norm_ffn[1]),
        moe_w_router[0].T.astype(BF16), moe_b_router[0].reshape(-1, 1).astype(F32), plan)

    cnt = tile_cnt[:, :, 0].astype(jnp.int32)
    seg = (cnt + SEG_ALIGN - 1) // SEG_ALIGN * SEG_ALIGN
    local = jnp.cumsum(seg, axis=1) - seg
    before = jnp.cumsum(seg, axis=0) - seg
    padded = (jnp.sum(seg, axis=0) + te - 1) // te * te
    ends = jnp.cumsum(padded)
    tab = jnp.stack([local, seg, (ends - padded)[None, :] + before], axis=1).astype(jnp.int32)
    experts_t = route_t[0:2].astype(jnp.int32)
    local_tok = jnp.repeat(local, tm, axis=0).T
    pick = experts_t[:, None, :] == jnp.arange(N_EXPERTS)[None, :, None]
    slot_t = jnp.sum(jnp.where(pick, local_tok[None], 0), axis=1) + route_t[2:4].astype(jnp.int32)
    tails = jnp.concatenate([jnp.where(padded > 0, ends - te, -1), ends[-1:] // te])
    tails = tails.astype(jnp.int32)
    n_row_tiles = plan["rows_sorted"] // te
    tile_expert = jnp.minimum(
        jnp.sum(ends[None, :] <= (jnp.arange(n_row_tiles) * te)[:, None], axis=1),
        N_EXPERTS - 1).astype(jnp.int32)
    n_used = (ends[-1:] // te).astype(jnp.int32)

    xs = _dispatch(xn, slot_t, route_t, tab, tails, plan)
    ys = _experts(xs, tile_expert, n_used, moe_w_gate[0].astype(BF16), moe_w_up[0].astype(BF16),
                  moe_w_down[0].astype(BF16), plan)
    y_p, y_s = _combine(h, slot_t.T, tab, ys, row(norm_final), plan)
    return (y_p.reshape(b_p, t_p, d), y_s.reshape(b_s, t_s, d))
```

```python
import functools

import jax
import jax.numpy as jnp
import numpy as np
from jax import lax
from jax.experimental import pallas as pl
from jax.experimental.pallas import tpu as pltpu

F32 = jnp.float32
BF16 = jnp.bfloat16

N_META = 16
GRID_W = 64
RMS_EPS = 1e-6
POOL_WINDOWS = (2, 4, 8, 16)
POOL_GROUP = 128
POOL_WIDTH = 512
CONV_WIDTH = 512
NA_HEADS = 16
HEAD_DIM = 64
NA_WIN_ROWS = 8
NA_WIN_COLS = 16
N_EXPERTS = 8
LANES = 128
MXU_DIM = 256
SEG_ALIGN = 8
HEADS_PER_BLOCK = LANES // HEAD_DIM
HALO_ROWS = NA_WIN_ROWS // 2
Q_ROWS = 4
WIN_ROWS = Q_ROWS + NA_WIN_ROWS
NQ = Q_ROWS * GRID_W
NK = WIN_ROWS * GRID_W
V_ROWS = HEAD_DIM + 16
EXT = 16
VMEM_LIMIT = 56 * 1024 * 1024


def _const_spec(shape):
    nd = len(shape)
    return pl.BlockSpec(shape, lambda *_: (0,) * nd, pipeline_mode=pl.Buffered(1))


def _rms(x, g):
    ms = jnp.mean(x * x, axis=-1, keepdims=True)
    return x * lax.rsqrt(ms + RMS_EPS) * g


def _dot(a, b):
    return jnp.dot(a, b, preferred_element_type=F32)


def _dot_nt(a, b):
    return lax.dot_general(a, b, (((1,), (1,)), ((), ())), preferred_element_type=F32)


def _silu(x):
    return x * jax.nn.sigmoid(x)


def _even_mix(xe, pos, seq_len, g, win_ref, pw_ref, ps, cw, wout_ref):
    n = xe.shape[0]
    xn = _rms(xe, g).astype(BF16)
    z = _dot(xn, win_ref[...])
    a_parts = []
    for gi, k in enumerate(POOL_WINDOWS):
        p = z[:, gi * POOL_GROUP:(gi + 1) * POOL_GROUP]
        s = p
        w = 1
        while w < k:
            s = s + pltpu.roll(s, n - w, 0)
            w *= 2
        left = k // 2
        right = k - 1 - left
        s = pltpu.roll(s, left, 0)
        cnt = jnp.minimum(pos + right + 1, seq_len) - jnp.maximum(pos - left, 0)
        inv = 1.0 / jnp.maximum(cnt, 1).astype(F32)
        pooled = (s * inv - p).astype(BF16)
        a_parts.append(_dot(pooled, pw_ref[gi]))
    a = jnp.concatenate(a_parts, axis=-1) * ps
    gb = z[:, POOL_WIDTH:POOL_WIDTH + CONV_WIDTH]
    gc = z[:, POOL_WIDTH + CONV_WIDTH:POOL_WIDTH + 2 * CONV_WIDTH]
    hc = z[:, POOL_WIDTH + 2 * CONV_WIDTH:]
    u = gc * hc
    conv = pltpu.roll(u, 1, 0) * cw[0:1] + u * cw[1:2] + pltpu.roll(u, n - 1, 0) * cw[2:3]
    c = gb * conv
    return (_dot(a.astype(BF16), wout_ref[0:POOL_WIDTH, :])
            + _dot(c.astype(BF16), wout_ref[POOL_WIDTH:, :]))


def _even_kernel(xp_ref, xpp_ref, xpn_ref, xs_ref, xsp_ref, xsn_ref, meta_ref, first_ref, g_ref,
                 win_ref, pw_ref, ps_ref, cw_ref, wout_ref, out_ref, xe_ref, *, tm, n_tiles,
                 tiles_p, n_p_tiles, tiles_s, t_p, t_s, n_seq):
    i = pl.program_id(0)
    weights = (g_ref[...], win_ref, pw_ref, ps_ref[...], cw_ref[...], wout_ref)
    d = xe_ref.shape[1]

    def token_tile(tile_ref, prev_ref, next_ref, tin, last, t_seq):
        n = tm + 2 * EXT

        @pl.when(tin == 0)
        def _():
            xe_ref[0:EXT, :] = meta_ref[...]

        @pl.when(tin != 0)
        def _():
            xe_ref[0:EXT, :] = prev_ref[...]

        xe_ref[EXT:EXT + tm, :] = tile_ref[...]

        @pl.when(tin == last)
        def _():
            xe_ref[EXT + tm:n, :] = jnp.zeros((EXT, d), F32)

        @pl.when(tin != last)
        def _():
            xe_ref[EXT + tm:n, :] = next_ref[...]

        pos = lax.broadcasted_iota(jnp.int32, (n, 1), 0) + tin * tm
        y = _even_mix(xe_ref[0:n, :], pos, t_seq + N_META, *weights)
        out_ref[...] = xe_ref[EXT:EXT + tm, :] + y[EXT:EXT + tm]

    @pl.when(i < n_p_tiles)
    def _prompt():
        token_tile(xp_ref, xpp_ref, xpn_ref, i % tiles_p, tiles_p - 1, t_p)

    @pl.when((i >= n_p_tiles) & (i < n_tiles))
    def _sample():
        token_tile(xs_ref, xsp_ref, xsn_ref, (i - n_p_tiles) % tiles_s, tiles_s - 1, t_s)

    @pl.when(i == n_tiles)
    def _meta():
        grp = 3 * EXT
        n = grp * n_seq
        for b in range(n_seq):
            xe_ref[grp * b:grp * b + EXT, :] = jnp.zeros((EXT, d), F32)
            xe_ref[grp * b + EXT:grp * b + 2 * EXT, :] = meta_ref[...]
            xe_ref[grp * b + 2 * EXT:grp * (b + 1), :] = first_ref[EXT * b:EXT * (b + 1), :]
        pos = lax.broadcasted_iota(jnp.int32, (n, 1), 0) % grp - EXT
        y = _even_mix(xe_ref[0:n, :], pos, jnp.int32(1 << 30), *weights)
        out_ref[...] = jnp.zeros(out_ref.shape, F32)
        for b in range(n_seq):
            out_ref[EXT * b:EXT * (b + 1), :] = meta_ref[...] + y[grp * b + EXT:grp * b + 2 * EXT]


def _even_layer(xp, xs, meta, xfirst, g, win, pw, ps, cw, wout, plan):
    tm = plan["tm"]
    d = xp.shape[1]
    n_tiles = plan["n_main"] // tm
    n_p_tiles = plan["n_p"] // tm
    n_s_tiles = n_tiles - n_p_tiles
    blk = tm // EXT
    kern = functools.partial(
        _even_kernel, tm=tm, n_tiles=n_tiles, tiles_p=plan["t_p"] // tm, n_p_tiles=n_p_tiles,
        tiles_s=plan["t_s"] // tm, t_p=plan["t_p"], t_s=plan["t_s"], n_seq=plan["n_seq"])
    xe_rows = max(tm + 2 * EXT, 3 * EXT * plan["n_seq"])

    def specs(first_tile, count):
        last = count * blk - 1
        tile = lambda i: jnp.clip(i - first_tile, 0, count - 1)
        return [
            pl.BlockSpec((tm, d), lambda i: (tile(i), 0)),
            pl.BlockSpec((EXT, d), lambda i: (jnp.clip(tile(i) * blk - 1, 0, last), 0)),
            pl.BlockSpec((EXT, d), lambda i: (jnp.clip((tile(i) + 1) * blk, 0, last), 0)),
        ]

    return pl.pallas_call(
        kern,
        grid=(n_tiles + 1,),
        in_specs=specs(0, n_p_tiles) + specs(n_p_tiles, n_s_tiles) + [
            _const_spec(meta.shape), _const_spec(xfirst.shape), _const_spec(g.shape),
            _const_spec(win.shape), _const_spec(pw.shape), _const_spec(ps.shape),
            _const_spec(cw.shape), _const_spec(wout.shape),
        ],
        out_specs=pl.BlockSpec((tm, d), lambda i: (i, 0)),
        out_shape=jax.ShapeDtypeStruct((plan["np_rows"], d), F32),
        scratch_shapes=[pltpu.VMEM((xe_rows, d), F32)],
        compiler_params=pltpu.CompilerParams(
            dimension_semantics=("arbitrary",), vmem_limit_bytes=VMEM_LIMIT),
        name="even_mixer",
    )(xp, xp, xp, xs, xs, xs, meta, xfirst, g, win, pw, ps, cw, wout)


def _ffn_kernel(x_ref, g_ref, wg_ref, wu_ref, wd_ref, out_ref):
    half = x_ref.shape[0] // 2
    for rows in (slice(0, half), slice(half, 2 * half)):
        x = x_ref[rows, :]
        xn = _rms(x, g_ref[...]).astype(BF16)
        act = (_silu(_dot(xn, wg_ref[...])) * _dot(xn, wu_ref[...])).astype(BF16)
        out_ref[rows, :] = x + _dot(act, wd_ref[...])


def _ffn_layer(h, g, wg, wu, wd, plan):
    tm = plan["tm"]
    d = h.shape[1]
    return pl.pallas_call(
        _ffn_kernel,
        grid=(h.shape[0] // tm,),
        in_specs=[pl.BlockSpec((tm, d), lambda i: (i, 0)), _const_spec(g.shape),
                  _const_spec(wg.shape), _const_spec(wu.shape), _const_spec(wd.shape)],
        out_specs=pl.BlockSpec((tm, d), lambda i: (i, 0)),
        out_shape=jax.ShapeDtypeStruct(h.shape, F32),
        compiler_params=pltpu.CompilerParams(
            dimension_semantics=("arbitrary",), vmem_limit_bytes=VMEM_LIMIT),
        name="dense_swiglu",
    )(h, g, wg, wu, wd)


def _qkv_kernel(x_ref, g_ref, wqk_ref, wvt_ref, qk_ref, vt_ref):
    xn = _rms(x_ref[...], g_ref[...]).astype(BF16)
    qk_ref[...] = _dot(xn, wqk_ref[...]).astype(BF16)
    vt_ref[...] = _dot_nt(wvt_ref[...], xn).astype(BF16)


def _qkv_layer(h, g, wqk, wvt, plan):
    tm = plan["tm"]
    d = h.shape[1]
    return pl.pallas_call(
        _qkv_kernel,
        grid=(h.shape[0] // tm,),
        in_specs=[pl.BlockSpec((tm, d), lambda i: (i, 0)), _const_spec(g.shape),
                  _const_spec(wqk.shape), _const_spec(wvt.shape)],
        out_specs=[pl.BlockSpec((tm, wqk.shape[1]), lambda i: (i, 0)),
                   pl.BlockSpec((wvt.shape[0], tm), lambda i: (0, i))],
        out_shape=[jax.ShapeDtypeStruct((h.shape[0], wqk.shape[1]), BF16),
                   jax.ShapeDtypeStruct((wvt.shape[0], h.shape[0]), BF16)],
        compiler_params=pltpu.CompilerParams(
            dimension_semantics=("arbitrary",), vmem_limit_bytes=VMEM_LIMIT),
        name="qkv_proj",
    )(h, g, wqk, wvt)


def _attn_kernel(q_ref, kc_ref, kp_ref, kn_ref, vc_ref, vp_ref, vn_ref, qm_ref, km_ref, vm_ref,
                 vmt_ref, tab_ref, mb_ref, o_ref, om_ref, kx_ref, vx_ref, s_ref, *, cr, chunks_p,
                 n_p_chunks, chunks_s, rows_p, rows_s):
    ci = pl.program_id(1)
    is_p = ci < n_p_chunks
    cin = jnp.where(is_p, ci % chunks_p, (ci - n_p_chunks) % chunks_s)
    rows = jnp.where(is_p, rows_p, rows_s)
    halo = HALO_ROWS * GRID_W
    ct = cr * GRID_W

    kx_ref[0:halo, :] = kp_ref[...]
    kx_ref[halo:halo + ct, :] = kc_ref[...]
    kx_ref[halo + ct:, :] = kn_ref[...]
    ones_rows = V_ROWS - HEAD_DIM
    vmt_aug = []
    for hh in range(HEADS_PER_BLOCK):
        hs = slice(hh * HEAD_DIM, (hh + 1) * HEAD_DIM)
        vx_ref[hh, 0:HEAD_DIM, 0:halo] = vp_ref[hs, :]
        vx_ref[hh, 0:HEAD_DIM, halo:halo + ct] = vc_ref[hs, :]
        vx_ref[hh, 0:HEAD_DIM, halo + ct:] = vn_ref[hs, :]
        vx_ref[hh, HEAD_DIM:, :] = jnp.ones((ones_rows, ct + 2 * halo), BF16)
        vmt_aug.append(jnp.concatenate(
            [vmt_ref[hs, :], jnp.ones((ones_rows, N_META), BF16)], axis=0))

    lane = lax.broadcasted_iota(jnp.int32, (1, LANES), 1)
    head_masks = [(lane // HEAD_DIM) == hh for hh in range(HEADS_PER_BLOCK)]
    km = km_ref[...]
    r0c = cin * cr

    n_blocks = cr // Q_ROWS

    def scores(bi, slot):
        r0 = r0c + bi * Q_ROWS
        edge = jnp.where(r0 == 0, 0, jnp.where(r0 == rows - Q_ROWS, 2, 1))
        off = bi * NQ
        q = q_ref[off:off + NQ, :]
        qq = jnp.concatenate(
            [jnp.where(head_masks[hh], q, jnp.zeros_like(q)) for hh in range(HEADS_PER_BLOCK)],
            axis=0)
        s_ref[slot, 0:NK, :] = _dot_nt(kx_ref[off:off + NK, :], qq) + tab_ref[edge, 0:NK, :]
        s_ref[slot, NK:NK + N_META, :] = _dot_nt(km, qq) + tab_ref[edge, NK:NK + N_META, :]

    def softmax_pv(bi, slot):
        off = bi * NQ
        s = s_ref[slot]
        pb = jnp.exp2(s - jnp.max(s, axis=0, keepdims=True)).astype(BF16)
        outs = []
        for hh in range(HEADS_PER_BLOCK):
            qs = slice(hh * NQ, (hh + 1) * NQ)
            oa = (_dot(vx_ref[hh, :, off:off + NK], pb[0:NK, qs])
                  + _dot(vmt_aug[hh], pb[NK:NK + N_META, qs]))
            outs.append(oa[0:HEAD_DIM] / oa[HEAD_DIM:HEAD_DIM + 1])
        o_ref[off:off + NQ, :] = jnp.concatenate(outs, axis=0).T.astype(BF16)

    scores(0, 0)
    scores(1, 1)
    for bi in range(n_blocks):
        if bi + 2 < n_blocks:
            scores(bi + 2, (bi + 2) % 4)
        softmax_pv(bi, bi % 4)

    qm = qm_ref[...]
    vm = vm_ref[...]
    out = None
    for hh in range(HEADS_PER_BLOCK):
        qh = jnp.where(head_masks[hh], qm, jnp.zeros_like(qm))
        sm = _dot_nt(qh, km) + mb_ref[hh, 0:1, 0:N_META]
        pm = jnp.exp2(sm - jnp.max(sm, axis=-1, keepdims=True))
        o = _dot(pm.astype(BF16), vm) / jnp.sum(pm, axis=-1, keepdims=True)
        out = o if out is None else jnp.where(head_masks[hh], o, out)
    om_ref[...] = out.astype(BF16)


def _attn_layer(qk, vt, vm, vmt, bias_tab, meta_bias, plan):
    cr = plan["cr"]
    ct = cr * GRID_W
    halo = HALO_ROWS * GRID_W
    n_main = plan["n_main"]
    n_chunks = n_main // ct
    n_hb = NA_HEADS // HEADS_PER_BLOCK
    hb = ct // halo
    last_halo = n_main // halo - 1
    meta_blk0 = n_main // N_META
    n_p_chunks = plan["n_p"] // ct
    chunks_p = plan["t_p"] // ct
    chunks_s = plan["t_s"] // ct
    b_p = plan["n_p"] // plan["t_p"]

    def seq_of(ci):
        return jnp.where(ci < n_p_chunks, ci // chunks_p, b_p + (ci - n_p_chunks) // chunks_s)

    def cur(off):
        return pl.BlockSpec((ct, LANES), lambda h, c: (c, off + h))

    def prev(off):
        return pl.BlockSpec((halo, LANES), lambda h, c: (jnp.maximum(c * hb - 1, 0), off + h))

    def nxt(off):
        return pl.BlockSpec((halo, LANES),
                            lambda h, c: (jnp.minimum((c + 1) * hb, last_halo), off + h))

    def meta(off):
        return pl.BlockSpec((N_META, LANES), lambda h, c: (meta_blk0 + seq_of(c), off + h))

    def vt_cur():
        return pl.BlockSpec((LANES, ct), lambda h, c: (h, c))

    def vt_prev():
        return pl.BlockSpec((LANES, halo), lambda h, c: (h, jnp.maximum(c * hb - 1, 0)))

    def vt_next():
        return pl.BlockSpec((LANES, halo), lambda h, c: (h, jnp.minimum((c + 1) * hb, last_halo)))

    kern = functools.partial(
        _attn_kernel, cr=cr, chunks_p=chunks_p, n_p_chunks=n_p_chunks, chunks_s=chunks_s,
        rows_p=plan["t_p"] // GRID_W, rows_s=plan["t_s"] // GRID_W)
    width = NA_HEADS * HEAD_DIM
    return pl.pallas_call(
        kern,
        grid=(n_hb, n_chunks),
        in_specs=[
            cur(0), cur(n_hb), prev(n_hb), nxt(n_hb), vt_cur(), vt_prev(), vt_next(),
            meta(0), meta(n_hb),
            pl.BlockSpec((N_META, LANES), lambda h, c: (seq_of(c), h)),
            pl.BlockSpec((None, LANES, N_META), lambda h, c: (seq_of(c), h, 0)),
            pl.BlockSpec((None,) + bias_tab.shape[1:], lambda h, c: (h, 0, 0, 0)),
            pl.BlockSpec((None, HEADS_PER_BLOCK, 8, LANES), lambda h, c: (h, 0, 0, 0)),
        ],
        out_specs=[
            pl.BlockSpec((ct, LANES), lambda h, c: (c, h)),
            pl.BlockSpec((N_META, LANES), lambda h, c: (seq_of(c), h)),
        ],
        out_shape=[jax.ShapeDtypeStruct((n_main, width), BF16),
                   jax.ShapeDtypeStruct((plan["n_seq"] * N_META, width), BF16)],
        scratch_shapes=[pltpu.VMEM((ct + 2 * halo, LANES), BF16),
                        pltpu.VMEM((HEADS_PER_BLOCK, V_ROWS, ct + 2 * halo), BF16),
                        pltpu.VMEM((4, NK + N_META, HEADS_PER_BLOCK * NQ), F32)],
        compiler_params=pltpu.CompilerParams(
            dimension_semantics=("arbitrary", "arbitrary"), vmem_limit_bytes=VMEM_LIMIT),
        name="neighbourhood_attention",
    )(qk, qk, qk, qk, vt, vt, vt, qk, qk, vm, vmt, bias_tab, meta_bias)


def _attn_bias_table(rel_bias, meta_bias):
    n_heads, n_row_off, n_col_off = rel_bias.shape
    half = n_col_off // 2
    ring = jnp.concatenate([
        rel_bias[:, :, half::-1],
        jnp.zeros((n_heads, n_row_off, 2 * GRID_W - n_col_off), F32),
        rel_bias[:, :, :half:-1]], axis=-1)
    toe = jnp.tile(ring, (1, 1, GRID_W))[:, :, :GRID_W * (2 * GRID_W - 1)]
    toe = toe.reshape(n_heads, n_row_off, GRID_W, 2 * GRID_W - 1)[..., :GRID_W]
    qc = np.arange(GRID_W)[None, :]
    kc = np.arange(GRID_W)[:, None]
    cstart = np.clip(qc - NA_WIN_COLS // 2, 0, GRID_W - NA_WIN_COLS)
    col_ok = (kc >= cstart) & (kc < cstart + NA_WIN_COLS)
    toe = jnp.where(col_ok[None, None], toe, -1e30)
    n_hb = n_heads // HEADS_PER_BLOCK
    toe = toe.reshape(n_hb, HEADS_PER_BLOCK, n_row_off, GRID_W, GRID_W)
    per_a = jnp.stack([toe[:, :, HALO_ROWS - 1 - a:HALO_ROWS - 1 - a + WIN_ROWS]
                       for a in range(Q_ROWS)], axis=2)
    per_a = jnp.transpose(per_a, (0, 3, 4, 1, 2, 5))
    a = np.arange(Q_ROWS)[None, :]
    b = np.arange(WIN_ROWS)[:, None]
    row_ok = np.stack([(b >= HALO_ROWS) & (a >= 0),
                       (b - a >= 0) & (b - a < NA_WIN_ROWS),
                       (b < NA_WIN_ROWS) & (a >= 0)])
    tab = jnp.where(row_ok[None, :, :, None, None, :, None], per_a[:, None], -1e30)
    tab = tab.reshape(n_hb, 3, NK, HEADS_PER_BLOCK * NQ)
    meta = jnp.transpose(meta_bias.reshape(n_hb, HEADS_PER_BLOCK, N_META), (0, 2, 1))
    meta = jnp.broadcast_to(meta[:, None, :, :, None], (n_hb, 3, N_META, HEADS_PER_BLOCK, NQ))
    meta = meta.reshape(n_hb, 3, N_META, HEADS_PER_BLOCK * NQ)
    return jnp.concatenate([tab, meta], axis=2).astype(F32)


def _route_kernel(o_ref, om_ref, h_ref, wout_ref, g_ref, wrt_ref, brc_ref, triu_ref,
                  h_out, xn_out, route_out, cnt_out, *, n_tiles):
    i = pl.program_id(0)

    def body(o):
        h = h_ref[...] + _dot(o, wout_ref[...])
        h_out[...] = h
        xn = _rms(h, g_ref[...]).astype(BF16)
        xn_out[...] = xn
        logits = _dot_nt(wrt_ref[...], xn) + brc_ref[...]
        tm = logits.shape[1]
        sub = lax.broadcasted_iota(jnp.int32, (N_EXPERTS, tm), 0)
        m1 = jnp.max(logits, axis=0, keepdims=True)
        i1 = jnp.min(jnp.where(logits == m1, sub, N_EXPERTS), axis=0, keepdims=True)
        oh1 = sub == i1
        rest = jnp.where(oh1, -jnp.inf, logits)
        m2 = jnp.max(rest, axis=0, keepdims=True)
        i2 = jnp.min(jnp.where(rest == m2, sub, N_EXPERTS), axis=0, keepdims=True)
        oh2 = sub == i2
        t = jnp.exp(m2 - m1)
        g1 = 1.0 / (1.0 + t)
        g2 = t * g1
        oh = jnp.where(oh1 | oh2, 1.0, 0.0)
        cum = _dot(oh.astype(BF16), triu_ref[...])
        lr1 = jnp.sum(jnp.where(oh1, cum, 0.0), axis=0, keepdims=True)
        lr2 = jnp.sum(jnp.where(oh2, cum, 0.0), axis=0, keepdims=True)
        route = jnp.zeros((N_EXPERTS, tm), F32)
        for k, val in enumerate([i1.astype(F32), i2.astype(F32), lr1, lr2, g1, g2]):
            route = jnp.where(sub == k, val, route)
        route_out[...] = route
        cnt_out[0] = jnp.broadcast_to(jnp.sum(oh, axis=1, keepdims=True), (N_EXPERTS, LANES))

    @pl.when(i < n_tiles)
    def _():
        body(o_ref[...])

    @pl.when(i == n_tiles)
    def _():
        body(om_ref[...])


def _route_layer(o_main, o_meta, h, wout, g, wrt, brc, plan):
    tm = plan["tm"]
    d = h.shape[1]
    n_tiles = plan["n_main"] // tm
    triu = (jnp.arange(tm)[:, None] < jnp.arange(tm)[None, :]).astype(BF16)
    kern = functools.partial(_route_kernel, n_tiles=n_tiles)
    rows = (n_tiles + 1) * tm
    return pl.pallas_call(
        kern,
        grid=(n_tiles + 1,),
        in_specs=[
            pl.BlockSpec((tm, d), lambda i: (jnp.minimum(i, n_tiles - 1), 0)),
            _const_spec(o_meta.shape),
            pl.BlockSpec((tm, d), lambda i: (i, 0)),
            _const_spec(wout.shape), _const_spec(g.shape), _const_spec(wrt.shape),
            _const_spec(brc.shape), _const_spec(triu.shape),
        ],
        out_specs=[
            pl.BlockSpec((tm, d), lambda i: (i, 0)),
            pl.BlockSpec((tm, d), lambda i: (i, 0)),
            pl.BlockSpec((N_EXPERTS, tm), lambda i: (0, i)),
            pl.BlockSpec((1, N_EXPERTS, LANES), lambda i: (i, 0, 0)),
        ],
        out_shape=[jax.ShapeDtypeStruct((rows, d), F32),
                   jax.ShapeDtypeStruct((rows, d), BF16),
                   jax.ShapeDtypeStruct((N_EXPERTS, rows), F32),
                   jax.ShapeDtypeStruct((n_tiles + 1, N_EXPERTS, LANES), F32)],
        compiler_params=pltpu.CompilerParams(
            dimension_semantics=("arbitrary",), vmem_limit_bytes=VMEM_LIMIT),
        name="outproj_router",
    )(o_main, o_meta, h, wout, g, wrt, brc, triu)


def _for_each_run(tab_ref, tm, fn):
    for e in range(N_EXPERTS):
        local, length, sorted_row = tab_ref[0, 0, e], tab_ref[0, 1, e], tab_ref[0, 2, e]
        n = tm
        while n >= SEG_ALIGN:
            @pl.when((length & n) != 0)
            def _(n=n):
                done = (length // (2 * n)) * (2 * n)
                fn(pl.multiple_of(local + done, SEG_ALIGN),
                   pl.multiple_of(sorted_row + done, SEG_ALIGN), n)
            n //= 2


def _dispatch_kernel(tab_ref, tabp_ref, tail_ref, slot_ref, route_ref, x_ref, xs_ref, stage_ref,
                     zero_ref, sems, zsem, *, tm, te):
    i = pl.program_id(0)
    d = x_ref.shape[1]
    slot = i % 2

    @pl.when(i == 0)
    def _():
        zero_ref[...] = jnp.zeros(zero_ref.shape, F32)

        def zero_tile(start):
            cp = pltpu.make_async_copy(
                zero_ref, xs_ref.at[pl.ds(pl.multiple_of(start, te), te)], zsem)
            cp.start()
            cp.wait()

        for e in range(N_EXPERTS):
            @pl.when(tail_ref[e] >= 0)
            def _():
                zero_tile(tail_ref[e])

        def zero_unused(u, carry):
            zero_tile(u * te)
            return carry

        lax.fori_loop(tail_ref[N_EXPERTS], xs_ref.shape[0] // te, zero_unused, 0)

    s_rows = stage_ref.shape[1]
    slot_iota = lax.broadcasted_iota(jnp.int32, (s_rows, tm), 0)
    first = slot_iota == slot_ref[0:1, :]
    second = slot_iota == slot_ref[1:2, :]
    perm = jnp.where(first | second, 1.0, 0.0).astype(BF16)
    stage_ref[slot, :, 0:d] = _dot(perm, x_ref[...])
    gate = jnp.sum(jnp.where(first, route_ref[4:5, :], 0.0)
                   + jnp.where(second, route_ref[5:6, :], 0.0), axis=1, keepdims=True)
    stage_ref[slot, :, d:] = jnp.broadcast_to(gate, (s_rows, LANES))

    def piece(buf, local_row, sorted_row, n):
        return pltpu.make_async_copy(
            stage_ref.at[buf, pl.ds(local_row, n)], xs_ref.at[pl.ds(sorted_row, n)],
            sems.at[buf])

    _for_each_run(tab_ref, tm, lambda lr, sr, n: piece(slot, lr, sr, n).start())

    @pl.when(i > 0)
    def _():
        _for_each_run(tabp_ref, tm, lambda lr, sr, n: piece(1 - slot, lr, sr, n).wait())

    @pl.when(i == pl.num_programs(0) - 1)
    def _():
        _for_each_run(tab_ref, tm, lambda lr, sr, n: piece(slot, lr, sr, n).wait())


def _dispatch(xn, slot_t, route_t, tab, tails, plan):
    tm = plan["tm"]
    te = plan["te"]
    d = xn.shape[1]
    kern = functools.partial(_dispatch_kernel, tm=tm, te=te)
    return pl.pallas_call(
        kern,
        grid=(xn.shape[0] // tm,),
        in_specs=[
            pl.BlockSpec((1, 3, N_EXPERTS), lambda i: (i, 0, 0), memory_space=pltpu.SMEM),
            pl.BlockSpec((1, 3, N_EXPERTS), lambda i: (jnp.maximum(i - 1, 0), 0, 0),
                         memory_space=pltpu.SMEM),
            pl.BlockSpec(memory_space=pltpu.SMEM),
            pl.BlockSpec((2, tm), lambda i: (0, i)),
            pl.BlockSpec((N_EXPERTS, tm), lambda i: (0, i)),
            pl.BlockSpec((tm, d), lambda i: (i, 0)),
        ],
        out_specs=pl.BlockSpec(memory_space=pl.ANY),
        out_shape=jax.ShapeDtypeStruct((plan["rows_sorted"], d + LANES), F32),
        scratch_shapes=[pltpu.VMEM((2, plan["s_rows"], d + LANES), F32),
                        pltpu.VMEM((te, d + LANES), F32),
                        pltpu.SemaphoreType.DMA((2,)), pltpu.SemaphoreType.DMA(())],
        compiler_params=pltpu.CompilerParams(
            dimension_semantics=("arbitrary",), vmem_limit_bytes=VMEM_LIMIT),
        name="moe_dispatch",
    )(tab, tab, tails, slot_t, route_t, xn)


def _expert_kernel(te_ref, nu_ref, x_ref, wg_ref, wu_ref, wd_ref, out_ref, *, sub):
    j = pl.program_id(0)
    c = pl.program_id(1)

    @pl.when(c == 0)
    def _():
        out_ref[...] = jnp.zeros(out_ref.shape, F32)

    @pl.when(j < nu_ref[0])
    def _():
        d = out_ref.shape[1]
        xb = x_ref[:, 0:d].astype(BF16)
        fc = wg_ref.shape[1]
        y = None
        for lo in range(0, fc, sub):
            hi = min(lo + sub, fc)
            act = (_silu(_dot(xb, wg_ref[:, lo:hi])) * _dot(xb, wu_ref[:, lo:hi])).astype(BF16)
            part = _dot(act, wd_ref[lo:hi, :])
            y = part if y is None else y + part
        out_ref[...] += x_ref[:, d:d + 1] * y


def _experts(xs, tile_expert, n_used, wg, wu, wd, plan):
    te = plan["te"]
    fc = plan["fc"]
    d = xs.shape[1] - LANES
    n_tiles = xs.shape[0] // te
    n_c = wg.shape[2] // fc

    def row(j, c, te_ref, nu_ref):
        return (jnp.minimum(j, nu_ref[0] - 1), 0)

    def chunk(j, c, nu_ref):
        return jnp.where(j < nu_ref[0], c, n_c - 1)

    def w_in(j, c, te_ref, nu_ref):
        return (te_ref[jnp.minimum(j, nu_ref[0] - 1)], 0, chunk(j, c, nu_ref))

    def w_out(j, c, te_ref, nu_ref):
        return (te_ref[jnp.minimum(j, nu_ref[0] - 1)], chunk(j, c, nu_ref), 0)

    grid_spec = pltpu.PrefetchScalarGridSpec(
        num_scalar_prefetch=2,
        grid=(n_tiles, n_c),
        in_specs=[
            pl.BlockSpec((te, d + LANES), row),
            pl.BlockSpec((None, d, fc), w_in),
            pl.BlockSpec((None, d, fc), w_in),
            pl.BlockSpec((None, fc, d), w_out),
        ],
        out_specs=pl.BlockSpec((te, d), lambda j, c, te_ref, nu_ref: (j, 0)),
    )
    return pl.pallas_call(
        functools.partial(_expert_kernel, sub=plan["fsub"]),
        grid_spec=grid_spec,
        out_shape=jax.ShapeDtypeStruct((xs.shape[0], d), F32),
        compiler_params=pltpu.CompilerParams(
            dimension_semantics=("arbitrary", "arbitrary"), vmem_limit_bytes=VMEM_LIMIT),
        name="moe_experts",
    )(tile_expert, n_used, xs, wg, wu, wd)


def _combine_kernel(tabc_ref, tabn_ref, slot_ref, h_ref, g_ref, ys_ref, outp_ref, outs_ref,
                    y_ref, sems, *, tm, n_tiles, n_p_tiles):
    i = pl.program_id(0)

    def piece(slot, local_row, sorted_row, n):
        return pltpu.make_async_copy(
            ys_ref.at[pl.ds(sorted_row, n)], y_ref.at[slot, pl.ds(local_row, n)], sems.at[slot])

    @pl.when(i == 0)
    def _():
        y_ref[...] = jnp.zeros(y_ref.shape, F32)
        _for_each_run(tabc_ref, tm, lambda lr, sr, n: piece(0, lr, sr, n).start())

    @pl.when(i + 1 < n_tiles)
    def _():
        _for_each_run(tabn_ref, tm, lambda lr, sr, n: piece((i + 1) % 2, lr, sr, n).start())

    slot = i % 2
    _for_each_run(tabc_ref, tm, lambda lr, sr, n: piece(slot, lr, sr, n).wait())

    y = y_ref[slot].astype(BF16)
    s_rows = y.shape[0]
    slot_iota = lax.broadcasted_iota(jnp.int32, (tm, s_rows), 1)
    pick = jnp.where((slot_iota == slot_ref[:, 0:1]) | (slot_iota == slot_ref[:, 1:2]),
                     1.0, 0.0).astype(BF16)
    h = h_ref[...] + _dot(pick, y)
    out = _rms(h, g_ref[...])

    @pl.when(i < n_p_tiles)
    def _():
        outp_ref[...] = out

    @pl.when(i >= n_p_tiles)
    def _():
        outs_ref[...] = out


def _combine(h, slot_c, tab, ys, g, plan):
    tm = plan["tm"]
    d = h.shape[1]
    n_tiles = plan["n_main"] // tm
    n_p_tiles = plan["n_p"] // tm
    kern = functools.partial(_combine_kernel, tm=tm, n_tiles=n_tiles, n_p_tiles=n_p_tiles)
    return pl.pallas_call(
        kern,
        grid=(n_tiles,),
        in_specs=[
            pl.BlockSpec((1, 3, N_EXPERTS), lambda i: (i, 0, 0), memory_space=pltpu.SMEM),
            pl.BlockSpec((1, 3, N_EXPERTS), lambda i: (jnp.minimum(i + 1, n_tiles - 1), 0, 0),
                         memory_space=pltpu.SMEM),
            pl.BlockSpec((tm, 2), lambda i: (i, 0)),
            pl.BlockSpec((tm, d), lambda i: (i, 0)),
            _const_spec(g.shape),
            pl.BlockSpec(memory_space=pl.ANY),
        ],
        out_specs=[
            pl.BlockSpec((tm, d), lambda i: (jnp.minimum(i, n_p_tiles - 1), 0)),
            pl.BlockSpec((tm, d), lambda i: (jnp.maximum(i - n_p_tiles, 0), 0)),
        ],
        out_shape=[jax.ShapeDtypeStruct((plan["n_p"], d), F32),
                   jax.ShapeDtypeStruct((plan["n_main"] - plan["n_p"], d), F32)],
        scratch_shapes=[pltpu.VMEM((2, plan["s_rows"], d), F32),
                        pltpu.SemaphoreType.DMA((2,))],
        compiler_params=pltpu.CompilerParams(
            dimension_semantics=("arbitrary",), vmem_limit_bytes=VMEM_LIMIT),
        name="moe_combine_norm",
    )(tab, tab, slot_c, h, g, ys)


def _make_plan(b_p, t_p, b_s, t_s, d_ff_expert):
    tm = 512
    te = 1024
    n_p = b_p * t_p
    n_main = n_p + b_s * t_s
    n_seq = b_p + b_s
    assert n_seq * N_META <= tm and t_p % tm == 0 and t_s % tm == 0
    np_rows = n_main + tm
    rows_p, rows_s = t_p // GRID_W, t_s // GRID_W
    cr = 64 if rows_p % 64 == 0 and rows_s % 64 == 0 else 4 * Q_ROWS
    assert rows_p % cr == 0 and rows_s % cr == 0 and cr % (4 * Q_ROWS) == 0
    fc = 1792 if d_ff_expert % 1792 == 0 else d_ff_expert
    fsub = MXU_DIM
    run_pad = N_EXPERTS * (SEG_ALIGN - 1)
    s_rows = -(-(2 * tm + run_pad) // SEG_ALIGN) * SEG_ALIGN
    rows_sorted = -(-(2 * np_rows + run_pad * (np_rows // tm)) // te) * te + N_EXPERTS * te
    return dict(tm=tm, te=te, n_p=n_p, n_main=n_main, n_seq=n_seq, np_rows=np_rows, t_p=t_p,
                t_s=t_s, cr=cr, fc=fc, fsub=fsub, s_rows=s_rows, rows_sorted=rows_sorted)


def kernel(x_prompt, x_sample, meta_tokens, norm_mix, norm_ffn, norm_final, even_w_in, pool_w,
           pool_scale, conv_w, even_w_out, ffn_w_gate, ffn_w_up, ffn_w_down, na_w_qkv,
           na_rel_bias, na_meta_bias, na_w_out, moe_w_router, moe_b_router, moe_w_gate,
           moe_w_up, moe_w_down):
    b_p, t_p, d = x_prompt.shape
    b_s, t_s, _ = x_sample.shape
    plan = _make_plan(b_p, t_p, b_s, t_s, moe_w_gate.shape[-1])
    tm, te = plan["tm"], plan["te"]
    n_main, n_seq, np_rows = plan["n_main"], plan["n_seq"], plan["np_rows"]
    n_meta_rows = n_seq * N_META

    xfirst = jnp.concatenate([x_prompt[:, :EXT].reshape(-1, d), x_sample[:, :EXT].reshape(-1, d)])

    row = lambda v: v.reshape(1, -1).astype(F32)
    h = _even_layer(x_prompt.reshape(-1, d), x_sample.reshape(-1, d), meta_tokens, xfirst,
                    row(norm_mix[0]), even_w_in[0].astype(BF16), pool_w[0].astype(BF16),
                    row(pool_scale[0]), conv_w[0], even_w_out[0].astype(BF16), plan)
    h = _ffn_layer(h, row(norm_ffn[0]), ffn_w_gate[0].astype(BF16), ffn_w_up[0].astype(BF16),
                   ffn_w_down[0].astype(BF16), plan)

    width = NA_HEADS * HEAD_DIM
    log2e = float(np.log2(np.e))
    qscale = jnp.concatenate([jnp.full((width,), HEAD_DIM ** -0.5 * log2e, F32),
                              jnp.ones((width,), F32)])
    qk, vt = _qkv_layer(h, row(norm_mix[1]), (na_w_qkv[0][:, :2 * width] * qscale).astype(BF16),
                        na_w_qkv[0][:, 2 * width:].T.astype(BF16), plan)
    vt_meta = vt[:, n_main:n_main + n_meta_rows]
    n_hb = NA_HEADS // HEADS_PER_BLOCK
    o_main, o_meta = _attn_layer(
        qk, vt, vt_meta.T, vt_meta.reshape(width, n_seq, N_META).transpose(1, 0, 2),
        _attn_bias_table(na_rel_bias[0] * log2e, na_meta_bias[0] * log2e),
        jnp.pad((na_meta_bias[0] * log2e).reshape(n_hb, HEADS_PER_BLOCK, 1, N_META),
                ((0, 0), (0, 0), (0, 7), (0, LANES - N_META))), plan)
    o_meta = jnp.concatenate([o_meta, jnp.zeros((tm - n_meta_rows, width), BF16)])

    h, xn, route_t, tile_cnt = _route_layer(
        o_main, o_meta, h, na_w_out[0].astype(BF16), row(norm_ffn[1]),
        moe_w_router[0].T.astype(BF16), moe_b_router[0].reshape(-1, 1).astype(F32), plan)

    cnt = tile_cnt[:, :, 0].astype(jnp.int32)
    seg = (cnt + SEG_ALIGN - 1) // SEG_ALIGN * SEG_ALIGN
    local = jnp.cumsum(seg, axis=1) - seg
    before = jnp.cumsum(seg, axis=0) - seg
    padded = (jnp.sum(seg, axis=0) + te - 1) // te * te
    ends = jnp.cumsum(padded)
    tab = jnp.stack([local, seg, (ends - padded)[None, :] + before], axis=1).astype(jnp.int32)
    experts_t = route_t[0:2].astype(jnp.int32)
    local_tok = jnp.repeat(local, tm, axis=0).T
    pick = experts_t[:, None, :] == jnp.arange(N_EXPERTS)[None, :, None]
    slot_t = jnp.sum(jnp.where(pick, local_tok[None], 0), axis=1) + route_t[2:4].astype(jnp.int32)
    tails = jnp.concatenate([jnp.where(padded > 0, ends - te, -1), ends[-1:] // te])
    tails = tails.astype(jnp.int32)
    n_row_tiles = plan["rows_sorted"] // te
    tile_expert = jnp.minimum(
        jnp.sum(ends[None, :] <= (jnp.arange(n_row_tiles) * te)[:, None], axis=1),
        N_EXPERTS - 1).astype(jnp.int32)
    n_used = (ends[-1:] // te).astype(jnp.int32)

    xs = _dispatch(xn, slot_t, route_t, tab, tails, plan)
    ys = _experts(xs, tile_expert, n_used, moe_w_gate[0].astype(BF16), moe_w_up[0].astype(BF16),
                  moe_w_down[0].astype(BF16), plan)
    y_p, y_s = _combine(h, slot_t.T, tab, ys, row(norm_final), plan)
    return (y_p.reshape(b_p, t_p, d), y_s.reshape(b_s, t_s, d))
```

```python
import functools

import jax
import jax.numpy as jnp
import numpy as np
from jax import lax
from jax.experimental import pallas as pl
from jax.experimental.pallas import tpu as pltpu

F32 = jnp.float32
BF16 = jnp.bfloat16

N_META = 16
GRID_W = 64
RMS_EPS = 1e-6
POOL_WINDOWS = (2, 4, 8, 16)
POOL_GROUP = 128
POOL_WIDTH = 512
CONV_WIDTH = 512
NA_HEADS = 16
HEAD_DIM = 64
NA_WIN_ROWS = 8
NA_WIN_COLS = 16
N_EXPERTS = 8
LANES = 128
MXU_DIM = 256
SEG_ALIGN = 8
HEADS_PER_BLOCK = LANES // HEAD_DIM
HALO_ROWS = NA_WIN_ROWS // 2
Q_ROWS = 4
WIN_ROWS = Q_ROWS + NA_WIN_ROWS
NQ = Q_ROWS * GRID_W
NK = WIN_ROWS * GRID_W
V_ROWS = HEAD_DIM + 16
EXT = 16
VMEM_LIMIT = 56 * 1024 * 1024


def _const_spec(shape):
    nd = len(shape)
    return pl.BlockSpec(shape, lambda *_: (0,) * nd, pipeline_mode=pl.Buffered(1))


def _rms(x, g):
    ms = jnp.mean(x * x, axis=-1, keepdims=True)
    return x * lax.rsqrt(ms + RMS_EPS) * g


def _dot(a, b):
    return jnp.dot(a, b, preferred_element_type=F32)


def _dot_nt(a, b):
    return lax.dot_general(a, b, (((1,), (1,)), ((), ())), preferred_element_type=F32)


def _silu(x):
    return x * jax.nn.sigmoid(x)


def _even_mix(xe, pos, seq_len, g, win_ref, pw_ref, ps, cw, wout_ref):
    n = xe.shape[0]
    xn = _rms(xe, g).astype(BF16)
    z = _dot(xn, win_ref[...])
    a_parts = []
    for gi, k in enumerate(POOL_WINDOWS):
        p = z[:, gi * POOL_GROUP:(gi + 1) * POOL_GROUP]
        s = p
        w = 1
        while w < k:
            s = s + pltpu.roll(s, n - w, 0)
            w *= 2
        left = k // 2
        right = k - 1 - left
        s = pltpu.roll(s, left, 0)
        cnt = jnp.minimum(pos + right + 1, seq_len) - jnp.maximum(pos - left, 0)
        inv = 1.0 / jnp.maximum(cnt, 1).astype(F32)
        pooled = (s * inv - p).astype(BF16)
        a_parts.append(_dot(pooled, pw_ref[gi]))
    a = jnp.concatenate(a_parts, axis=-1) * ps
    gb = z[:, POOL_WIDTH:POOL_WIDTH + CONV_WIDTH]
    gc = z[:, POOL_WIDTH + CONV_WIDTH:POOL_WIDTH + 2 * CONV_WIDTH]
    hc = z[:, POOL_WIDTH + 2 * CONV_WIDTH:]
    u = gc * hc
    conv = pltpu.roll(u, 1, 0) * cw[0:1] + u * cw[1:2] + pltpu.roll(u, n - 1, 0) * cw[2:3]
    c = gb * conv
    return (_dot(a.astype(BF16), wout_ref[0:POOL_WIDTH, :])
            + _dot(c.astype(BF16), wout_ref[POOL_WIDTH:, :]))


def _even_kernel(xp_ref, xpp_ref, xpn_ref, xs_ref, xsp_ref, xsn_ref, meta_ref, first_ref, g_ref,
                 win_ref, pw_ref, ps_ref, cw_ref, wout_ref, out_ref, xe_ref, *, tm, n_tiles,
                 tiles_p, n_p_tiles, tiles_s, t_p, t_s, n_seq):
    i = pl.program_id(0)
    weights = (g_ref[...], win_ref, pw_ref, ps_ref[...], cw_ref[...], wout_ref)
    d = xe_ref.shape[1]

    def token_tile(tile_ref, prev_ref, next_ref, tin, last, t_seq):
        n = tm + 2 * EXT

        @pl.when(tin == 0)
        def _():
            xe_ref[0:EXT, :] = meta_ref[...]

        @pl.when(tin != 0)
        def _():
            xe_ref[0:EXT, :] = prev_ref[...]

        xe_ref[EXT:EXT + tm, :] = tile_ref[...]

        @pl.when(tin == last)
        def _():
            xe_ref[EXT + tm:n, :] = jnp.zeros((EXT, d), F32)

        @pl.when(tin != last)
        def _():
            xe_ref[EXT + tm:n, :] = next_ref[...]

        pos = lax.broadcasted_iota(jnp.int32, (n, 1), 0) + tin * tm
        y = _even_mix(xe_ref[0:n, :], pos, t_seq + N_META, *weights)
        out_ref[...] = xe_ref[EXT:EXT + tm, :] + y[EXT:EXT + tm]

    @pl.when(i < n_p_tiles)
    def _prompt():
        token_tile(xp_ref, xpp_ref, xpn_ref, i % tiles_p, tiles_p - 1, t_p)

    @pl.when((i >= n_p_tiles) & (i < n_tiles))
    def _sample():
        token_tile(xs_ref, xsp_ref, xsn_ref, (i - n_p_tiles) % tiles_s, tiles_s - 1, t_s)

    @pl.when(i == n_tiles)
    def _meta():
        grp = 3 * EXT
        n = grp * n_seq
        for b in range(n_seq):
            xe_ref[grp * b:grp * b + EXT, :] = jnp.zeros((EXT, d), F32)
            xe_ref[grp * b + EXT:grp * b + 2 * EXT, :] = meta_ref[...]
            xe_ref[grp * b + 2 * EXT:grp * (b + 1), :] = first_ref[EXT * b:EXT * (b + 1), :]
        pos = lax.broadcasted_iota(jnp.int32, (n, 1), 0) % grp - EXT
        y = _even_mix(xe_ref[0:n, :], pos, jnp.int32(1 << 30), *weights)
        out_ref[...] = jnp.zeros(out_ref.shape, F32)
        for b in range(n_seq):
            out_ref[EXT * b:EXT * (b + 1), :] = meta_ref[...] + y[grp * b + EXT:grp * b + 2 * EXT]


def _even_layer(xp, xs, meta, xfirst, g, win, pw, ps, cw, wout, plan):
    tm = plan["tm"]
    d = xp.shape[1]
    n_tiles = plan["n_main"] // tm
    n_p_tiles = plan["n_p"] // tm
    n_s_tiles = n_tiles - n_p_tiles
    blk = tm // EXT
    kern = functools.partial(
        _even_kernel, tm=tm, n_tiles=n_tiles, tiles_p=plan["t_p"] // tm, n_p_tiles=n_p_tiles,
        tiles_s=plan["t_s"] // tm, t_p=plan["t_p"], t_s=plan["t_s"], n_seq=plan["n_seq"])
    xe_rows = max(tm + 2 * EXT, 3 * EXT * plan["n_seq"])

    def specs(first_tile, count):
        last = count * blk - 1
        tile = lambda i: jnp.clip(i - first_tile, 0, count - 1)
        return [
            pl.BlockSpec((tm, d), lambda i: (tile(i), 0)),
            pl.BlockSpec((EXT, d), lambda i: (jnp.clip(tile(i) * blk - 1, 0, last), 0)),
            pl.BlockSpec((EXT, d), lambda i: (jnp.clip((tile(i) + 1) * blk, 0, last), 0)),
        ]

    return pl.pallas_call(
        kern,
        grid=(n_tiles + 1,),
        in_specs=specs(0, n_p_tiles) + specs(n_p_tiles, n_s_tiles) + [
            _const_spec(meta.shape), _const_spec(xfirst.shape), _const_spec(g.shape),
            _const_spec(win.shape), _const_spec(pw.shape), _const_spec(ps.shape),
            _const_spec(cw.shape), _const_spec(wout.shape),
        ],
        out_specs=pl.BlockSpec((tm, d), lambda i: (i, 0)),
        out_shape=jax.ShapeDtypeStruct((plan["np_rows"], d), F32),
        scratch_shapes=[pltpu.VMEM((xe_rows, d), F32)],
        compiler_params=pltpu.CompilerParams(
            dimension_semantics=("arbitrary",), vmem_limit_bytes=VMEM_LIMIT),
        name="even_mixer",
    )(xp, xp, xp, xs, xs, xs, meta, xfirst, g, win, pw, ps, cw, wout)


def _ffn_kernel(x_ref, g_ref, wg_ref, wu_ref, wd_ref, out_ref):
    half = x_ref.shape[0] // 2
    for rows in (slice(0, half), slice(half, 2 * half)):
        x = x_ref[rows, :]
        xn = _rms(x, g_ref[...]).astype(BF16)
        act = (_silu(_dot(xn, wg_ref[...])) * _dot(xn, wu_ref[...])).astype(BF16)
        out_ref[rows, :] = x + _dot(act, wd_ref[...])


def _ffn_layer(h, g, wg, wu, wd, plan):
    tm = plan["tm"]
    d = h.shape[1]
    return pl.pallas_call(
        _ffn_kernel,
        grid=(h.shape[0] // tm,),
        in_specs=[pl.BlockSpec((tm, d), lambda i: (i, 0)), _const_spec(g.shape),
                  _const_spec(wg.shape), _const_spec(wu.shape), _const_spec(wd.shape)],
        out_specs=pl.BlockSpec((tm, d), lambda i: (i, 0)),
        out_shape=jax.ShapeDtypeStruct(h.shape, F32),
        compiler_params=pltpu.CompilerParams(
            dimension_semantics=("arbitrary",), vmem_limit_bytes=VMEM_LIMIT),
        name="dense_swiglu",
    )(h, g, wg, wu, wd)


def _qkv_kernel(x_ref, g_ref, wqk_ref, wvt_ref, qk_ref, vt_ref):
    xn = _rms(x_ref[...], g_ref[...]).astype(BF16)
    qk_ref[...] = _dot(xn, wqk_ref[...]).astype(BF16)
    vt_ref[...] = _dot_nt(wvt_ref[...], xn).astype(BF16)


def _qkv_layer(h, g, wqk, wvt, plan):
    tm = plan["tm"]
    d = h.shape[1]
    return pl.pallas_call(
        _qkv_kernel,
        grid=(h.shape[0] // tm,),
        in_specs=[pl.BlockSpec((tm, d), lambda i: (i, 0)), _const_spec(g.shape),
                  _const_spec(wqk.shape), _const_spec(wvt.shape)],
        out_specs=[pl.BlockSpec((tm, wqk.shape[1]), lambda i: (i, 0)),
                   pl.BlockSpec((wvt.shape[0], tm), lambda i: (0, i))],
        out_shape=[jax.ShapeDtypeStruct((h.shape[0], wqk.shape[1]), BF16),
                   jax.ShapeDtypeStruct((wvt.shape[0], h.shape[0]), BF16)],
        compiler_params=pltpu.CompilerParams(
            dimension_semantics=("arbitrary",), vmem_limit_bytes=VMEM_LIMIT),
        name="qkv_proj",
    )(h, g, wqk, wvt)


def _attn_kernel(q_ref, kc_ref, kp_ref, kn_ref, vc_ref, vp_ref, vn_ref, qm_ref, km_ref, vm_ref,
                 vmt_ref, tab_ref, tabm_ref, mb_ref, o_ref, om_ref, kx_ref, vx_ref, s_ref, *, cr,
                 chunks_p,
                 n_p_chunks, chunks_s, rows_p, rows_s):
    ci = pl.program_id(1)
    is_p = ci < n_p_chunks
    cin = jnp.where(is_p, ci % chunks_p, (ci - n_p_chunks) % chunks_s)
    rows = jnp.where(is_p, rows_p, rows_s)
    halo = HALO_ROWS * GRID_W
    ct = cr * GRID_W

    kx_ref[0:halo, :] = kp_ref[...]
    kx_ref[halo:halo + ct, :] = kc_ref[...]
    kx_ref[halo + ct:, :] = kn_ref[...]
    ones_rows = V_ROWS - HEAD_DIM
    vmt_aug = []
    for hh in range(HEADS_PER_BLOCK):
        hs = slice(hh * HEAD_DIM, (hh + 1) * HEAD_DIM)
        vx_ref[hh, 0:HEAD_DIM, 0:halo] = vp_ref[hs, :]
        vx_ref[hh, 0:HEAD_DIM, halo:halo + ct] = vc_ref[hs, :]
        vx_ref[hh, 0:HEAD_DIM, halo + ct:] = vn_ref[hs, :]
        vx_ref[hh, HEAD_DIM:, :] = jnp.ones((ones_rows, ct + 2 * halo), BF16)
        vmt_aug.append(jnp.concatenate(
            [vmt_ref[hs, :], jnp.ones((ones_rows, N_META), BF16)], axis=0))

    lane = lax.broadcasted_iota(jnp.int32, (1, LANES), 1)
    head_masks = [(lane // HEAD_DIM) == hh for hh in range(HEADS_PER_BLOCK)]
    km = km_ref[...]
    r0c = cin * cr

    n_blocks = cr // Q_ROWS

    def scores(bi, slot):
        r0 = r0c + bi * Q_ROWS
        edge = jnp.where(r0 == 0, 0, jnp.where(r0 == rows - Q_ROWS, 2, 1))
        off = bi * NQ
        q = q_ref[off:off + NQ, :]
        qq = jnp.concatenate(
            [jnp.where(head_masks[hh], q, jnp.zeros_like(q)) for hh in range(HEADS_PER_BLOCK)],
            axis=0)
        s_ref[slot, 0:NK, :] = _dot_nt(kx_ref[off:off + NK, :], qq) + tab_ref[edge, 0:NK, :]
        s_ref[slot, NK:NK + N_META, :] = _dot_nt(km, qq) + tabm_ref[...]

    def softmax_pv(bi, slot):
        off = bi * NQ
        s = s_ref[slot]
        pb = jnp.exp2(s - jnp.max(s, axis=0, keepdims=True)).astype(BF16)
        outs = []
        for hh in range(HEADS_PER_BLOCK):
            qs = slice(hh * NQ, (hh + 1) * NQ)
            oa = (_dot(vx_ref[hh, :, off:off + NK], pb[0:NK, qs])
                  + _dot(vmt_aug[hh], pb[NK:NK + N_META, qs]))
            outs.append(oa[0:HEAD_DIM] / oa[HEAD_DIM:HEAD_DIM + 1])
        o_ref[off:off + NQ, :] = jnp.concatenate(outs, axis=0).T.astype(BF16)

    scores(0, 0)
    scores(1, 1)
    for bi in range(n_blocks):
        if bi + 2 < n_blocks:
            scores(bi + 2, (bi + 2) % 4)
        softmax_pv(bi, bi % 4)

    qm = qm_ref[...]
    vm = vm_ref[...]
    out = None
    for hh in range(HEADS_PER_BLOCK):
        qh = jnp.where(head_masks[hh], qm, jnp.zeros_like(qm))
        sm = _dot_nt(qh, km) + mb_ref[hh, 0:1, 0:N_META]
        pm = jnp.exp2(sm - jnp.max(sm, axis=-1, keepdims=True))
        o = _dot(pm.astype(BF16), vm) / jnp.sum(pm, axis=-1, keepdims=True)
        out = o if out is None else jnp.where(head_masks[hh], o, out)
    om_ref[...] = out.astype(BF16)


def _attn_layer(qk, vt, vm, vmt, bias_tabs, meta_bias, plan):
    bias_tab, bias_meta = bias_tabs
    cr = plan["cr"]
    ct = cr * GRID_W
    halo = HALO_ROWS * GRID_W
    n_main = plan["n_main"]
    n_chunks = n_main // ct
    n_hb = NA_HEADS // HEADS_PER_BLOCK
    hb = ct // halo
    last_halo = n_main // halo - 1
    meta_blk0 = n_main // N_META
    n_p_chunks = plan["n_p"] // ct
    chunks_p = plan["t_p"] // ct
    chunks_s = plan["t_s"] // ct
    b_p = plan["n_p"] // plan["t_p"]

    def seq_of(ci):
        return jnp.where(ci < n_p_chunks, ci // chunks_p, b_p + (ci - n_p_chunks) // chunks_s)

    def cur(off):
        return pl.BlockSpec((ct, LANES), lambda h, c: (c, off + h))

    def prev(off):
        return pl.BlockSpec((halo, LANES), lambda h, c: (jnp.maximum(c * hb - 1, 0), off + h))

    def nxt(off):
        return pl.BlockSpec((halo, LANES),
                            lambda h, c: (jnp.minimum((c + 1) * hb, last_halo), off + h))

    def meta(off):
        return pl.BlockSpec((N_META, LANES), lambda h, c: (meta_blk0 + seq_of(c), off + h))

    def vt_cur():
        return pl.BlockSpec((LANES, ct), lambda h, c: (h, c))

    def vt_prev():
        return pl.BlockSpec((LANES, halo), lambda h, c: (h, jnp.maximum(c * hb - 1, 0)))

    def vt_next():
        return pl.BlockSpec((LANES, halo), lambda h, c: (h, jnp.minimum((c + 1) * hb, last_halo)))

    kern = functools.partial(
        _attn_kernel, cr=cr, chunks_p=chunks_p, n_p_chunks=n_p_chunks, chunks_s=chunks_s,
        rows_p=plan["t_p"] // GRID_W, rows_s=plan["t_s"] // GRID_W)
    width = NA_HEADS * HEAD_DIM
    return pl.pallas_call(
        kern,
        grid=(n_hb, n_chunks),
        in_specs=[
            cur(0), cur(n_hb), prev(n_hb), nxt(n_hb), vt_cur(), vt_prev(), vt_next(),
            meta(0), meta(n_hb),
            pl.BlockSpec((N_META, LANES), lambda h, c: (seq_of(c), h)),
            pl.BlockSpec((None, LANES, N_META), lambda h, c: (seq_of(c), h, 0)),
            pl.BlockSpec((None,) + bias_tab.shape[1:], lambda h, c: (h, 0, 0, 0)),
            pl.BlockSpec((None,) + bias_meta.shape[1:], lambda h, c: (h, 0, 0)),
            pl.BlockSpec((None, HEADS_PER_BLOCK, 8, LANES), lambda h, c: (h, 0, 0, 0)),
        ],
        out_specs=[
            pl.BlockSpec((ct, LANES), lambda h, c: (c, h)),
            pl.BlockSpec((N_META, LANES), lambda h, c: (seq_of(c), h)),
        ],
        out_shape=[jax.ShapeDtypeStruct((n_main, width), BF16),
                   jax.ShapeDtypeStruct((plan["n_seq"] * N_META, width), BF16)],
        scratch_shapes=[pltpu.VMEM((ct + 2 * halo, LANES), BF16),
                        pltpu.VMEM((HEADS_PER_BLOCK, V_ROWS, ct + 2 * halo), BF16),
                        pltpu.VMEM((4, NK + N_META, HEADS_PER_BLOCK * NQ), F32)],
        compiler_params=pltpu.CompilerParams(
            dimension_semantics=("arbitrary", "arbitrary"), vmem_limit_bytes=VMEM_LIMIT),
        name="neighbourhood_attention",
    )(qk, qk, qk, qk, vt, vt, vt, qk, qk, vm, vmt, bias_tab, bias_meta, meta_bias)


def _attn_bias_table(rel_bias, meta_bias):
    n_heads, n_row_off, n_col_off = rel_bias.shape
    half = n_col_off // 2
    qc = np.arange(GRID_W)[None, :]
    kc = np.arange(GRID_W)[:, None]
    cstart = np.clip(qc - NA_WIN_COLS // 2, 0, GRID_W - NA_WIN_COLS)
    col_ok = (kc >= cstart) & (kc < cstart + NA_WIN_COLS)
    toe = jnp.full((n_heads, n_row_off, GRID_W, GRID_W), -1e30, F32)
    for c in range(n_col_off):
        toe = jnp.where((col_ok & (kc - qc + half == c))[None, None],
                        rel_bias[:, :, c, None, None], toe)
    n_hb = n_heads // HEADS_PER_BLOCK
    toe = toe.reshape(n_hb, HEADS_PER_BLOCK, n_row_off, GRID_W, GRID_W)
    per_a = jnp.stack([toe[:, :, HALO_ROWS - 1 - a:HALO_ROWS - 1 - a + WIN_ROWS]
                       for a in range(Q_ROWS)], axis=2)
    per_a = jnp.transpose(per_a, (0, 3, 4, 1, 2, 5))
    a = np.arange(Q_ROWS)[None, :]
    b = np.arange(WIN_ROWS)[:, None]
    row_ok = np.stack([(b >= HALO_ROWS) & (a >= 0),
                       (b - a >= 0) & (b - a < NA_WIN_ROWS),
                       (b < NA_WIN_ROWS) & (a >= 0)])
    tab = jnp.where(row_ok[None, :, :, None, None, :, None], per_a[:, None], -1e30)
    tab = tab.reshape(n_hb, 3, NK, HEADS_PER_BLOCK * NQ)
    meta = jnp.transpose(meta_bias.reshape(n_hb, HEADS_PER_BLOCK, N_META), (0, 2, 1))
    meta = jnp.broadcast_to(meta[:, :, :, None], (n_hb, N_META, HEADS_PER_BLOCK, NQ))
    return tab.astype(F32), meta.reshape(n_hb, N_META, HEADS_PER_BLOCK * NQ).astype(F32)


def _route_kernel(o_ref, om_ref, h_ref, wout_ref, g_ref, wrt_ref, brc_ref, triu_ref,
                  h_out, xn_out, route_out, cnt_out, *, n_tiles):
    i = pl.program_id(0)

    def body(o):
        h = h_ref[...] + _dot(o, wout_ref[...])
        h_out[...] = h
        xn = _rms(h, g_ref[...]).astype(BF16)
        xn_out[...] = xn
        logits = _dot_nt(wrt_ref[...], xn) + brc_ref[...]
        tm = logits.shape[1]
        sub = lax.broadcasted_iota(jnp.int32, (N_EXPERTS, tm), 0)
        m1 = jnp.max(logits, axis=0, keepdims=True)
        i1 = jnp.min(jnp.where(logits == m1, sub, N_EXPERTS), axis=0, keepdims=True)
        oh1 = sub == i1
        rest = jnp.where(oh1, -jnp.inf, logits)
        m2 = jnp.max(rest, axis=0, keepdims=True)
        i2 = jnp.min(jnp.where(rest == m2, sub, N_EXPERTS), axis=0, keepdims=True)
        oh2 = sub == i2
        t = jnp.exp(m2 - m1)
        g1 = 1.0 / (1.0 + t)
        g2 = t * g1
        oh = jnp.where(oh1 | oh2, 1.0, 0.0)
        cum = _dot(oh.astype(BF16), triu_ref[...])
        lr1 = jnp.sum(jnp.where(oh1, cum, 0.0), axis=0, keepdims=True)
        lr2 = jnp.sum(jnp.where(oh2, cum, 0.0), axis=0, keepdims=True)
        route = jnp.zeros((N_EXPERTS, tm), F32)
        for k, val in enumerate([i1.astype(F32), i2.astype(F32), lr1, lr2, g1, g2]):
            route = jnp.where(sub == k, val, route)
        route_out[...] = route
        cnt_out[0] = jnp.broadcast_to(jnp.sum(oh, axis=1, keepdims=True), (N_EXPERTS, LANES))

    @pl.when(i < n_tiles)
    def _():
        body(o_ref[...])

    @pl.when(i == n_tiles)
    def _():
        body(om_ref[...])


def _route_layer(o_main, o_meta, h, wout, g, wrt, brc, plan):
    tm = plan["tm"]
    d = h.shape[1]
    n_tiles = plan["n_main"] // tm
    triu = (jnp.arange(tm)[:, None] < jnp.arange(tm)[None, :]).astype(BF16)
    kern = functools.partial(_route_kernel, n_tiles=n_tiles)
    rows = (n_tiles + 1) * tm
    return pl.pallas_call(
        kern,
        grid=(n_tiles + 1,),
        in_specs=[
            pl.BlockSpec((tm, d), lambda i: (jnp.minimum(i, n_tiles - 1), 0)),
            _const_spec(o_meta.shape),
            pl.BlockSpec((tm, d), lambda i: (i, 0)),
            _const_spec(wout.shape), _const_spec(g.shape), _const_spec(wrt.shape),
            _const_spec(brc.shape), _const_spec(triu.shape),
        ],
        out_specs=[
            pl.BlockSpec((tm, d), lambda i: (i, 0)),
            pl.BlockSpec((tm, d), lambda i: (i, 0)),
            pl.BlockSpec((N_EXPERTS, tm), lambda i: (0, i)),
            pl.BlockSpec((1, N_EXPERTS, LANES), lambda i: (i, 0, 0)),
        ],
        out_shape=[jax.ShapeDtypeStruct((rows, d), F32),
                   jax.ShapeDtypeStruct((rows, d), BF16),
                   jax.ShapeDtypeStruct((N_EXPERTS, rows), F32),
                   jax.ShapeDtypeStruct((n_tiles + 1, N_EXPERTS, LANES), F32)],
        compiler_params=pltpu.CompilerParams(
            dimension_semantics=("arbitrary",), vmem_limit_bytes=VMEM_LIMIT),
        name="outproj_router",
    )(o_main, o_meta, h, wout, g, wrt, brc, triu)


def _for_each_run(tab_ref, tm, fn):
    for e in range(N_EXPERTS):
        local, length, sorted_row = tab_ref[0, 0, e], tab_ref[0, 1, e], tab_ref[0, 2, e]
        n = tm
        while n >= SEG_ALIGN:
            @pl.when((length & n) != 0)
            def _(n=n):
                done = (length // (2 * n)) * (2 * n)
                fn(pl.multiple_of(local + done, SEG_ALIGN),
                   pl.multiple_of(sorted_row + done, SEG_ALIGN), n)
            n //= 2


def _dispatch_kernel(tab_ref, tabp_ref, tail_ref, slot_ref, route_ref, x_ref, xs_ref, stage_ref,
                     zero_ref, sems, zsem, *, tm, te):
    i = pl.program_id(0)
    d = x_ref.shape[1]
    slot = i % 2

    @pl.when(i == 0)
    def _():
        zero_ref[...] = jnp.zeros(zero_ref.shape, F32)

        def zero_tile(start):
            cp = pltpu.make_async_copy(
                zero_ref, xs_ref.at[pl.ds(pl.multiple_of(start, te), te)], zsem)
            cp.start()
            cp.wait()

        for e in range(N_EXPERTS):
            @pl.when(tail_ref[e] >= 0)
            def _():
                zero_tile(tail_ref[e])

        def zero_unused(u, carry):
            zero_tile(u * te)
            return carry

        lax.fori_loop(tail_ref[N_EXPERTS], xs_ref.shape[0] // te, zero_unused, 0)

    s_rows = stage_ref.shape[1]
    slot_iota = lax.broadcasted_iota(jnp.int32, (s_rows, tm), 0)
    first = slot_iota == slot_ref[0:1, :]
    second = slot_iota == slot_ref[1:2, :]
    perm = jnp.where(first | second, 1.0, 0.0).astype(BF16)
    stage_ref[slot, :, 0:d] = _dot(perm, x_ref[...])
    gate = jnp.sum(jnp.where(first, route_ref[4:5, :], 0.0)
                   + jnp.where(second, route_ref[5:6, :], 0.0), axis=1, keepdims=True)
    stage_ref[slot, :, d:] = jnp.broadcast_to(gate, (s_rows, LANES))

    def piece(buf, local_row, sorted_row, n):
        return pltpu.make_async_copy(
            stage_ref.at[buf, pl.ds(local_row, n)], xs_ref.at[pl.ds(sorted_row, n)],
            sems.at[buf])

    _for_each_run(tab_ref, tm, lambda lr, sr, n: piece(slot, lr, sr, n).start())

    @pl.when(i > 0)
    def _():
        _for_each_run(tabp_ref, tm, lambda lr, sr, n: piece(1 - slot, lr, sr, n).wait())

    @pl.when(i == pl.num_programs(0) - 1)
    def _():
        _for_each_run(tab_ref, tm, lambda lr, sr, n: piece(slot, lr, sr, n).wait())


def _dispatch(xn, slot_t, route_t, tab, tails, plan):
    tm = plan["tm"]
    te = plan["te"]
    d = xn.shape[1]
    kern = functools.partial(_dispatch_kernel, tm=tm, te=te)
    return pl.pallas_call(
        kern,
        grid=(xn.shape[0] // tm,),
        in_specs=[
            pl.BlockSpec((1, 3, N_EXPERTS), lambda i: (i, 0, 0), memory_space=pltpu.SMEM),
            pl.BlockSpec((1, 3, N_EXPERTS), lambda i: (jnp.maximum(i - 1, 0), 0, 0),
                         memory_space=pltpu.SMEM),
            pl.BlockSpec(memory_space=pltpu.SMEM),
            pl.BlockSpec((2, tm), lambda i: (0, i)),
            pl.BlockSpec((N_EXPERTS, tm), lambda i: (0, i)),
            pl.BlockSpec((tm, d), lambda i: (i, 0)),
        ],
        out_specs=pl.BlockSpec(memory_space=pl.ANY),
        out_shape=jax.ShapeDtypeStruct((plan["rows_sorted"], d + LANES), F32),
        scratch_shapes=[pltpu.VMEM((2, plan["s_rows"], d + LANES), F32),
                        pltpu.VMEM((te, d + LANES), F32),
                        pltpu.SemaphoreType.DMA((2,)), pltpu.SemaphoreType.DMA(())],
        compiler_params=pltpu.CompilerParams(
            dimension_semantics=("arbitrary",), vmem_limit_bytes=VMEM_LIMIT),
        name="moe_dispatch",
    )(tab, tab, tails, slot_t, route_t, xn)


def _expert_kernel(te_ref, nu_ref, x_ref, wg_ref, wu_ref, wd_ref, out_ref, *, sub):
    j = pl.program_id(0)
    c = pl.program_id(1)

    @pl.when(c == 0)
    def _():
        out_ref[...] = jnp.zeros(out_ref.shape, F32)

    @pl.when(j < nu_ref[0])
    def _():
        d = out_ref.shape[1]
        xb = x_ref[:, 0:d].astype(BF16)
        fc = wg_ref.shape[1]
        y = None
        for lo in range(0, fc, sub):
            hi = min(lo + sub, fc)
            act = (_silu(_dot(xb, wg_ref[:, lo:hi])) * _dot(xb, wu_ref[:, lo:hi])).astype(BF16)
            part = _dot(act, wd_ref[lo:hi, :])
            y = part if y is None else y + part
        out_ref[...] += x_ref[:, d:d + 1] * y


def _experts(xs, tile_expert, n_used, wg, wu, wd, plan):
    te = plan["te"]
    fc = plan["fc"]
    d = xs.shape[1] - LANES
    n_tiles = xs.shape[0] // te
    n_c = wg.shape[2] // fc

    def row(j, c, te_ref, nu_ref):
        return (jnp.minimum(j, nu_ref[0] - 1), 0)

    def chunk(j, c, nu_ref):
        return jnp.where(j < nu_ref[0], c, n_c - 1)

    def w_in(j, c, te_ref, nu_ref):
        return (te_ref[jnp.minimum(j, nu_ref[0] - 1)], 0, chunk(j, c, nu_ref))

    def w_out(j, c, te_ref, nu_ref):
        return (te_ref[jnp.minimum(j, nu_ref[0] - 1)], chunk(j, c, nu_ref), 0)

    grid_spec = pltpu.PrefetchScalarGridSpec(
        num_scalar_prefetch=2,
        grid=(n_tiles, n_c),
        in_specs=[
            pl.BlockSpec((te, d + LANES), row),
            pl.BlockSpec((None, d, fc), w_in),
            pl.BlockSpec((None, d, fc), w_in),
            pl.BlockSpec((None, fc, d), w_out),
        ],
        out_specs=pl.BlockSpec((te, d), lambda j, c, te_ref, nu_ref: (j, 0)),
    )
    return pl.pallas_call(
        functools.partial(_expert_kernel, sub=plan["fsub"]),
        grid_spec=grid_spec,
        out_shape=jax.ShapeDtypeStruct((xs.shape[0], d), F32),
        compiler_params=pltpu.CompilerParams(
            dimension_semantics=("arbitrary", "arbitrary"), vmem_limit_bytes=VMEM_LIMIT),
        name="moe_experts",
    )(tile_expert, n_used, xs, wg, wu, wd)


def _combine_kernel(tabc_ref, tabn_ref, slot_ref, h_ref, g_ref, ys_ref, outp_ref, outs_ref,
                    y_ref, sems, *, tm, n_tiles, n_p_tiles):
    i = pl.program_id(0)

    def piece(slot, local_row, sorted_row, n):
        return pltpu.make_async_copy(
            ys_ref.at[pl.ds(sorted_row, n)], y_ref.at[slot, pl.ds(local_row, n)], sems.at[slot])

    @pl.when(i == 0)
    def _():
        y_ref[...] = jnp.zeros(y_ref.shape, F32)
        _for_each_run(tabc_ref, tm, lambda lr, sr, n: piece(0, lr, sr, n).start())

    @pl.when(i + 1 < n_tiles)
    def _():
        _for_each_run(tabn_ref, tm, lambda lr, sr, n: piece((i + 1) % 2, lr, sr, n).start())

    slot = i % 2
    _for_each_run(tabc_ref, tm, lambda lr, sr, n: piece(slot, lr, sr, n).wait())

    y = y_ref[slot].astype(BF16)
    s_rows = y.shape[0]
    slot_iota = lax.broadcasted_iota(jnp.int32, (tm, s_rows), 1)
    pick = jnp.where((slot_iota == slot_ref[:, 0:1]) | (slot_iota == slot_ref[:, 1:2]),
                     1.0, 0.0).astype(BF16)
    h = h_ref[...] + _dot(pick, y)
    out = _rms(h, g_ref[...])

    @pl.when(i < n_p_tiles)
    def _():
        outp_ref[...] = out

    @pl.when(i >= n_p_tiles)
    def _():
        outs_ref[...] = out


def _combine(h, slot_c, tab, ys, g, plan):
    tm = plan["tm"]
    d = h.shape[1]
    n_tiles = plan["n_main"] // tm
    n_p_tiles = plan["n_p"] // tm
    kern = functools.partial(_combine_kernel, tm=tm, n_tiles=n_tiles, n_p_tiles=n_p_tiles)
    return pl.pallas_call(
        kern,
        grid=(n_tiles,),
        in_specs=[
            pl.BlockSpec((1, 3, N_EXPERTS), lambda i: (i, 0, 0), memory_space=pltpu.SMEM),
            pl.BlockSpec((1, 3, N_EXPERTS), lambda i: (jnp.minimum(i + 1, n_tiles - 1), 0, 0),
                         memory_space=pltpu.SMEM),
            pl.BlockSpec((tm, 2), lambda i: (i, 0)),
            pl.BlockSpec((tm, d), lambda i: (i, 0)),
            _const_spec(g.shape),
            pl.BlockSpec(memory_space=pl.ANY),
        ],
        out_specs=[
            pl.BlockSpec((tm, d), lambda i: (jnp.minimum(i, n_p_tiles - 1), 0)),
            pl.BlockSpec((tm, d), lambda i: (jnp.maximum(i - n_p_tiles, 0), 0)),
        ],
        out_shape=[jax.ShapeDtypeStruct((plan["n_p"], d), F32),
                   jax.ShapeDtypeStruct((plan["n_main"] - plan["n_p"], d), F32)],
        scratch_shapes=[pltpu.VMEM((2, plan["s_rows"], d), F32),
                        pltpu.SemaphoreType.DMA((2,))],
        compiler_params=pltpu.CompilerParams(
            dimension_semantics=("arbitrary",), vmem_limit_bytes=VMEM_LIMIT),
        name="moe_combine_norm",
    )(tab, tab, slot_c, h, g, ys)


def _make_plan(b_p, t_p, b_s, t_s, d_ff_expert):
    tm = 512
    te = 1024
    n_p = b_p * t_p
    n_main = n_p + b_s * t_s
    n_seq = b_p + b_s
    assert n_seq * N_META <= tm and t_p % tm == 0 and t_s % tm == 0
    np_rows = n_main + tm
    rows_p, rows_s = t_p // GRID_W, t_s // GRID_W
    cr = 64 if rows_p % 64 == 0 and rows_s % 64 == 0 else 4 * Q_ROWS
    assert rows_p % cr == 0 and rows_s % cr == 0 and cr % (4 * Q_ROWS) == 0
    fc = 1792 if d_ff_expert % 1792 == 0 else d_ff_expert
    fsub = MXU_DIM
    run_pad = N_EXPERTS * (SEG_ALIGN - 1)
    s_rows = -(-(2 * tm + run_pad) // SEG_ALIGN) * SEG_ALIGN
    rows_sorted = -(-(2 * np_rows + run_pad * (np_rows // tm)) // te) * te + N_EXPERTS * te
    return dict(tm=tm, te=te, n_p=n_p, n_main=n_main, n_seq=n_seq, np_rows=np_rows, t_p=t_p,
                t_s=t_s, cr=cr, fc=fc, fsub=fsub, s_rows=s_rows, rows_sorted=rows_sorted)


def kernel(x_prompt, x_sample, meta_tokens, norm_mix, norm_ffn, norm_final, even_w_in, pool_w,
           pool_scale, conv_w, even_w_out, ffn_w_gate, ffn_w_up, ffn_w_down, na_w_qkv,
           na_rel_bias, na_meta_bias, na_w_out, moe_w_router, moe_b_router, moe_w_gate,
           moe_w_up, moe_w_down):
    b_p, t_p, d = x_prompt.shape
    b_s, t_s, _ = x_sample.shape
    plan = _make_plan(b_p, t_p, b_s, t_s, moe_w_gate.shape[-1])
    tm, te = plan["tm"], plan["te"]
    n_main, n_seq, np_rows = plan["n_main"], plan["n_seq"], plan["np_rows"]
    n_meta_rows = n_seq * N_META

    xfirst = jnp.concatenate([x_prompt[:, :EXT].reshape(-1, d), x_sample[:, :EXT].reshape(-1, d)])

    row = lambda v: v.reshape(1, -1).astype(F32)
    h = _even_layer(x_prompt.reshape(-1, d), x_sample.reshape(-1, d), meta_tokens, xfirst,
                    row(norm_mix[0]), even_w_in[0].astype(BF16), pool_w[0].astype(BF16),
                    row(pool_scale[0]), conv_w[0], even_w_out[0].astype(BF16), plan)
    h = _ffn_layer(h, row(norm_ffn[0]), ffn_w_gate[0].astype(BF16), ffn_w_up[0].astype(BF16),
                   ffn_w_down[0].astype(BF16), plan)

    width = NA_HEADS * HEAD_DIM
    log2e = float(np.log2(np.e))
    qscale = jnp.concatenate([jnp.full((width,), HEAD_DIM ** -0.5 * log2e, F32),
                              jnp.ones((width,), F32)])
    qk, vt = _qkv_layer(h, row(norm_mix[1]), (na_w_qkv[0][:, :2 * width] * qscale).astype(BF16),
                        na_w_qkv[0][:, 2 * width:].T.astype(BF16), plan)
    vt_meta = vt[:, n_main:n_main + n_meta_rows]
    n_hb = NA_HEADS // HEADS_PER_BLOCK
    o_main, o_meta = _attn_layer(
        qk, vt, vt_meta.T, vt_meta.reshape(width, n_seq, N_META).transpose(1, 0, 2),
        _attn_bias_table(na_rel_bias[0] * log2e, na_meta_bias[0] * log2e),
        jnp.pad((na_meta_bias[0] * log2e).reshape(n_hb, HEADS_PER_BLOCK, 1, N_META),
                ((0, 0), (0, 0), (0, 7), (0, LANES - N_META))), plan)
    o_meta = jnp.concatenate([o_meta, jnp.zeros((tm - n_meta_rows, width), BF16)])

    h, xn, route_t, tile_cnt = _route_layer(
        o_main, o_meta, h, na_w_out[0].astype(BF16), row(norm_ffn[1]),
        moe_w_router[0].T.astype(BF16), moe_b_router[0].reshape(-1, 1).astype(F32), plan)

    cnt = tile_cnt[:, :, 0].astype(jnp.int32)
    seg = (cnt + SEG_ALIGN - 1) // SEG_ALIGN * SEG_ALIGN
    local = jnp.cumsum(seg, axis=1) - seg
    before = jnp.cumsum(seg, axis=0) - seg
    padded = (jnp.sum(seg, axis=0) + te - 1) // te * te
    ends = jnp.cumsum(padded)
    tab = jnp.stack([local, seg, (ends - padded)[None, :] + before], axis=1).astype(jnp.int32)
    experts_t = route_t[0:2].astype(jnp.int32)
    local_tok = jnp.repeat(local, tm, axis=0).T
    pick = experts_t[:, None, :] == jnp.arange(N_EXPERTS)[None, :, None]
    slot_t = jnp.sum(jnp.where(pick, local_tok[None], 0), axis=1) + route_t[2:4].astype(jnp.int32)
    tails = jnp.concatenate([jnp.where(padded > 0, ends - te, -1), ends[-1:] // te])
    tails = tails.astype(jnp.int32)
    n_row_tiles = plan["rows_sorted"] // te
    tile_expert = jnp.minimum(
        jnp.sum(ends[None, :] <= (jnp.arange(n_row_tiles) * te)[:, None], axis=1),
        N_EXPERTS - 1).astype(jnp.int32)
    n_used = (ends[-1:] // te).astype(jnp.int32)

    xs = _dispatch(xn, slot_t, route_t, tab, tails, plan)
    ys = _experts(xs, tile_expert, n_used, moe_w_gate[0].astype(BF16), moe_w_up[0].astype(BF16),
                  moe_w_down[0].astype(BF16), plan)
    y_p, y_s = _combine(h, slot_t.T, tab, ys, row(norm_final), plan)
    return (y_p.reshape(b_p, t_p, d), y_s.reshape(b_s, t_s, d))
```

```python
import functools

import jax
import jax.numpy as jnp
import numpy as np
from jax import lax
from jax.experimental import pallas as pl
from jax.experimental.pallas import tpu as pltpu

F32 = jnp.float32
BF16 = jnp.bfloat16

N_META = 16
GRID_W = 64
RMS_EPS = 1e-6
POOL_WINDOWS = (2, 4, 8, 16)
POOL_GROUP = 128
POOL_WIDTH = 512
CONV_WIDTH = 512
NA_HEADS = 16
HEAD_DIM = 64
NA_WIN_ROWS = 8
NA_WIN_COLS = 16
N_EXPERTS = 8
LANES = 128
MXU_DIM = 256
SEG_ALIGN = 8
HEADS_PER_BLOCK = LANES // HEAD_DIM
HALO_ROWS = NA_WIN_ROWS // 2
Q_ROWS = 4
WIN_ROWS = Q_ROWS + NA_WIN_ROWS
NQ = Q_ROWS * GRID_W
NK = WIN_ROWS * GRID_W
V_ROWS = HEAD_DIM + 16
EXT = 16
VMEM_LIMIT = 56 * 1024 * 1024


def _const_spec(shape):
    nd = len(shape)
    return pl.BlockSpec(shape, lambda *_: (0,) * nd, pipeline_mode=pl.Buffered(1))


def _rms(x, g):
    ms = jnp.mean(x * x, axis=-1, keepdims=True)
    return x * lax.rsqrt(ms + RMS_EPS) * g


def _dot(a, b):
    return jnp.dot(a, b, preferred_element_type=F32)


def _dot_nt(a, b):
    return lax.dot_general(a, b, (((1,), (1,)), ((), ())), preferred_element_type=F32)


def _silu(x):
    return x * jax.nn.sigmoid(x)


def _even_mix(xe, pos, seq_len, g, win_ref, pw_ref, ps, cw, wout_ref):
    n = xe.shape[0]
    xn = _rms(xe, g).astype(BF16)
    z = _dot(xn, win_ref[...])
    a_parts = []
    for gi, k in enumerate(POOL_WINDOWS):
        p = z[:, gi * POOL_GROUP:(gi + 1) * POOL_GROUP]
        s = p
        w = 1
        while w < k:
            s = s + pltpu.roll(s, n - w, 0)
            w *= 2
        left = k // 2
        right = k - 1 - left
        s = pltpu.roll(s, left, 0)
        cnt = jnp.minimum(pos + right + 1, seq_len) - jnp.maximum(pos - left, 0)
        inv = 1.0 / jnp.maximum(cnt, 1).astype(F32)
        pooled = (s * inv - p).astype(BF16)
        a_parts.append(_dot(pooled, pw_ref[gi]))
    a = jnp.concatenate(a_parts, axis=-1) * ps
    gb = z[:, POOL_WIDTH:POOL_WIDTH + CONV_WIDTH]
    gc = z[:, POOL_WIDTH + CONV_WIDTH:POOL_WIDTH + 2 * CONV_WIDTH]
    hc = z[:, POOL_WIDTH + 2 * CONV_WIDTH:]
    u = gc * hc
    conv = pltpu.roll(u, 1, 0) * cw[0:1] + u * cw[1:2] + pltpu.roll(u, n - 1, 0) * cw[2:3]
    c = gb * conv
    return (_dot(a.astype(BF16), wout_ref[0:POOL_WIDTH, :])
            + _dot(c.astype(BF16), wout_ref[POOL_WIDTH:, :]))


def _even_kernel(xp_ref, xpp_ref, xpn_ref, xs_ref, xsp_ref, xsn_ref, meta_ref, first_ref, g_ref,
                 win_ref, pw_ref, ps_ref, cw_ref, wout_ref, out_ref, xe_ref, *, tm, n_tiles,
                 tiles_p, n_p_tiles, tiles_s, t_p, t_s, n_seq):
    i = pl.program_id(0)
    weights = (g_ref[...], win_ref, pw_ref, ps_ref[...], cw_ref[...], wout_ref)
    d = xe_ref.shape[1]

    def token_tile(tile_ref, prev_ref, next_ref, tin, last, t_seq):
        n = tm + 2 * EXT

        @pl.when(tin == 0)
        def _():
            xe_ref[0:EXT, :] = meta_ref[...]

        @pl.when(tin != 0)
        def _():
            xe_ref[0:EXT, :] = prev_ref[...]

        xe_ref[EXT:EXT + tm, :] = tile_ref[...]

        @pl.when(tin == last)
        def _():
            xe_ref[EXT + tm:n, :] = jnp.zeros((EXT, d), F32)

        @pl.when(tin != last)
        def _():
            xe_ref[EXT + tm:n, :] = next_ref[...]

        pos = lax.broadcasted_iota(jnp.int32, (n, 1), 0) + tin * tm
        y = _even_mix(xe_ref[0:n, :], pos, t_seq + N_META, *weights)
        out_ref[...] = xe_ref[EXT:EXT + tm, :] + y[EXT:EXT + tm]

    @pl.when(i < n_p_tiles)
    def _prompt():
        token_tile(xp_ref, xpp_ref, xpn_ref, i % tiles_p, tiles_p - 1, t_p)

    @pl.when((i >= n_p_tiles) & (i < n_tiles))
    def _sample():
        token_tile(xs_ref, xsp_ref, xsn_ref, (i - n_p_tiles) % tiles_s, tiles_s - 1, t_s)

    @pl.when(i == n_tiles)
    def _meta():
        grp = 3 * EXT
        n = grp * n_seq
        for b in range(n_seq):
            xe_ref[grp * b:grp * b + EXT, :] = jnp.zeros((EXT, d), F32)
            xe_ref[grp * b + EXT:grp * b + 2 * EXT, :] = meta_ref[...]
            xe_ref[grp * b + 2 * EXT:grp * (b + 1), :] = first_ref[EXT * b:EXT * (b + 1), :]
        pos = lax.broadcasted_iota(jnp.int32, (n, 1), 0) % grp - EXT
        y = _even_mix(xe_ref[0:n, :], pos, jnp.int32(1 << 30), *weights)
        out_ref[...] = jnp.zeros(out_ref.shape, F32)
        for b in range(n_seq):
            out_ref[EXT * b:EXT * (b + 1), :] = meta_ref[...] + y[grp * b + EXT:grp * b + 2 * EXT]


def _even_layer(xp, xs, meta, xfirst, g, win, pw, ps, cw, wout, plan):
    tm = plan["tm"]
    d = xp.shape[1]
    n_tiles = plan["n_main"] // tm
    n_p_tiles = plan["n_p"] // tm
    n_s_tiles = n_tiles - n_p_tiles
    blk = tm // EXT
    kern = functools.partial(
        _even_kernel, tm=tm, n_tiles=n_tiles, tiles_p=plan["t_p"] // tm, n_p_tiles=n_p_tiles,
        tiles_s=plan["t_s"] // tm, t_p=plan["t_p"], t_s=plan["t_s"], n_seq=plan["n_seq"])
    xe_rows = max(tm + 2 * EXT, 3 * EXT * plan["n_seq"])

    def specs(first_tile, count):
        last = count * blk - 1
        tile = lambda i: jnp.clip(i - first_tile, 0, count - 1)
        return [
            pl.BlockSpec((tm, d), lambda i: (tile(i), 0)),
            pl.BlockSpec((EXT, d), lambda i: (jnp.clip(tile(i) * blk - 1, 0, last), 0)),
            pl.BlockSpec((EXT, d), lambda i: (jnp.clip((tile(i) + 1) * blk, 0, last), 0)),
        ]

    return pl.pallas_call(
        kern,
        grid=(n_tiles + 1,),
        in_specs=specs(0, n_p_tiles) + specs(n_p_tiles, n_s_tiles) + [
            _const_spec(meta.shape), _const_spec(xfirst.shape), _const_spec(g.shape),
            _const_spec(win.shape), _const_spec(pw.shape), _const_spec(ps.shape),
            _const_spec(cw.shape), _const_spec(wout.shape),
        ],
        out_specs=pl.BlockSpec((tm, d), lambda i: (i, 0)),
        out_shape=jax.ShapeDtypeStruct((plan["np_rows"], d), F32),
        scratch_shapes=[pltpu.VMEM((xe_rows, d), F32)],
        compiler_params=pltpu.CompilerParams(
            dimension_semantics=("arbitrary",), vmem_limit_bytes=VMEM_LIMIT),
        name="even_mixer",
    )(xp, xp, xp, xs, xs, xs, meta, xfirst, g, win, pw, ps, cw, wout)


def _ffn_kernel(x_ref, g_ref, wg_ref, wu_ref, wd_ref, out_ref):
    half = x_ref.shape[0] // 2
    for rows in (slice(0, half), slice(half, 2 * half)):
        x = x_ref[rows, :]
        xn = _rms(x, g_ref[...]).astype(BF16)
        act = (_silu(_dot(xn, wg_ref[...])) * _dot(xn, wu_ref[...])).astype(BF16)
        out_ref[rows, :] = x + _dot(act, wd_ref[...])


def _ffn_layer(h, g, wg, wu, wd, plan):
    tm = plan["tm"]
    d = h.shape[1]
    return pl.pallas_call(
        _ffn_kernel,
        grid=(h.shape[0] // tm,),
        in_specs=[pl.BlockSpec((tm, d), lambda i: (i, 0)), _const_spec(g.shape),
                  _const_spec(wg.shape), _const_spec(wu.shape), _const_spec(wd.shape)],
        out_specs=pl.BlockSpec((tm, d), lambda i: (i, 0)),
        out_shape=jax.ShapeDtypeStruct(h.shape, F32),
        compiler_params=pltpu.CompilerParams(
            dimension_semantics=("arbitrary",), vmem_limit_bytes=VMEM_LIMIT),
        name="dense_swiglu",
    )(h, g, wg, wu, wd)


def _qkv_kernel(x_ref, g_ref, wqk_ref, wvt_ref, qk_ref, vt_ref):
    xn = _rms(x_ref[...], g_ref[...]).astype(BF16)
    qk_ref[...] = _dot(xn, wqk_ref[...]).astype(BF16)
    vt_ref[...] = _dot_nt(wvt_ref[...], xn).astype(BF16)


def _qkv_layer(h, g, wqk, wvt, plan):
    tm = plan["tm"]
    d = h.shape[1]
    return pl.pallas_call(
        _qkv_kernel,
        grid=(h.shape[0] // tm,),
        in_specs=[pl.BlockSpec((tm, d), lambda i: (i, 0)), _const_spec(g.shape),
                  _const_spec(wqk.shape), _const_spec(wvt.shape)],
        out_specs=[pl.BlockSpec((tm, wqk.shape[1]), lambda i: (i, 0)),
                   pl.BlockSpec((wvt.shape[0], tm), lambda i: (0, i))],
        out_shape=[jax.ShapeDtypeStruct((h.shape[0], wqk.shape[1]), BF16),
                   jax.ShapeDtypeStruct((wvt.shape[0], h.shape[0]), BF16)],
        compiler_params=pltpu.CompilerParams(
            dimension_semantics=("arbitrary",), vmem_limit_bytes=VMEM_LIMIT),
        name="qkv_proj",
    )(h, g, wqk, wvt)


def _attn_kernel(q_ref, kc_ref, kp_ref, kn_ref, vc_ref, vp_ref, vn_ref, qm_ref, km_ref, vm_ref,
                 vmt_ref, tab_ref, tabm_ref, mb_ref, o_ref, om_ref, kx_ref, vx_ref, s_ref, *, cr,
                 chunks_p,
                 n_p_chunks, chunks_s, rows_p, rows_s):
    ci = pl.program_id(1)
    is_p = ci < n_p_chunks
    cin = jnp.where(is_p, ci % chunks_p, (ci - n_p_chunks) % chunks_s)
    rows = jnp.where(is_p, rows_p, rows_s)
    halo = HALO_ROWS * GRID_W
    ct = cr * GRID_W

    kx_ref[0:halo, :] = kp_ref[...]
    kx_ref[halo:halo + ct, :] = kc_ref[...]
    kx_ref[halo + ct:, :] = kn_ref[...]
    ones_rows = V_ROWS - HEAD_DIM
    vmt_aug = []
    for hh in range(HEADS_PER_BLOCK):
        hs = slice(hh * HEAD_DIM, (hh + 1) * HEAD_DIM)
        vx_ref[hh, 0:HEAD_DIM, 0:halo] = vp_ref[hs, :]
        vx_ref[hh, 0:HEAD_DIM, halo:halo + ct] = vc_ref[hs, :]
        vx_ref[hh, 0:HEAD_DIM, halo + ct:] = vn_ref[hs, :]
        vx_ref[hh, HEAD_DIM:, :] = jnp.ones((ones_rows, ct + 2 * halo), BF16)
        vmt_aug.append(jnp.concatenate(
            [vmt_ref[hs, :], jnp.ones((ones_rows, N_META), BF16)], axis=0))

    lane = lax.broadcasted_iota(jnp.int32, (1, LANES), 1)
    head_masks = [(lane // HEAD_DIM) == hh for hh in range(HEADS_PER_BLOCK)]
    km = km_ref[...]
    r0c = cin * cr

    n_blocks = cr // Q_ROWS

    def scores(bi, slot):
        r0 = r0c + bi * Q_ROWS
        edge = jnp.where(r0 == 0, 0, jnp.where(r0 == rows - Q_ROWS, 2, 1))
        off = bi * NQ
        q = q_ref[off:off + NQ, :]
        qq = jnp.concatenate(
            [jnp.where(head_masks[hh], q, jnp.zeros_like(q)) for hh in range(HEADS_PER_BLOCK)],
            axis=0)
        s_ref[slot, 0:NK, :] = _dot_nt(kx_ref[off:off + NK, :], qq) + tab_ref[edge, 0:NK, :]
        s_ref[slot, NK:NK + N_META, :] = _dot_nt(km, qq) + tabm_ref[...]

    def softmax_pv(bi, slot):
        off = bi * NQ
        s = s_ref[slot]
        pb = jnp.exp2(s - jnp.max(s, axis=0, keepdims=True)).astype(BF16)
        outs = []
        for hh in range(HEADS_PER_BLOCK):
            qs = slice(hh * NQ, (hh + 1) * NQ)
            oa = (_dot(vx_ref[hh, :, off:off + NK], pb[0:NK, qs])
                  + _dot(vmt_aug[hh], pb[NK:NK + N_META, qs]))
            outs.append(oa[0:HEAD_DIM] / oa[HEAD_DIM:HEAD_DIM + 1])
        o_ref[off:off + NQ, :] = jnp.concatenate(outs, axis=0).T.astype(BF16)

    scores(0, 0)
    scores(1, 1)
    for bi in range(n_blocks):
        if bi + 2 < n_blocks:
            scores(bi + 2, (bi + 2) % 4)
        softmax_pv(bi, bi % 4)

    qm = qm_ref[...]
    vm = vm_ref[...]
    out = None
    for hh in range(HEADS_PER_BLOCK):
        qh = jnp.where(head_masks[hh], qm, jnp.zeros_like(qm))
        sm = _dot_nt(qh, km) + mb_ref[hh, 0:1, 0:N_META]
        pm = jnp.exp2(sm - jnp.max(sm, axis=-1, keepdims=True))
        o = _dot(pm.astype(BF16), vm) / jnp.sum(pm, axis=-1, keepdims=True)
        out = o if out is None else jnp.where(head_masks[hh], o, out)
    om_ref[...] = out.astype(BF16)


def _attn_layer(qk, vt, vm, vmt, bias_tabs, meta_bias, plan):
    bias_tab, bias_meta = bias_tabs
    cr = plan["cr"]
    ct = cr * GRID_W
    halo = HALO_ROWS * GRID_W
    n_main = plan["n_main"]
    n_chunks = n_main // ct
    n_hb = NA_HEADS // HEADS_PER_BLOCK
    hb = ct // halo
    last_halo = n_main // halo - 1
    meta_blk0 = n_main // N_META
    n_p_chunks = plan["n_p"] // ct
    chunks_p = plan["t_p"] // ct
    chunks_s = plan["t_s"] // ct
    b_p = plan["n_p"] // plan["t_p"]

    def seq_of(ci):
        return jnp.where(ci < n_p_chunks, ci // chunks_p, b_p + (ci - n_p_chunks) // chunks_s)

    def cur(off):
        return pl.BlockSpec((ct, LANES), lambda h, c: (c, off + h))

    def prev(off):
        return pl.BlockSpec((halo, LANES), lambda h, c: (jnp.maximum(c * hb - 1, 0), off + h))

    def nxt(off):
        return pl.BlockSpec((halo, LANES),
                            lambda h, c: (jnp.minimum((c + 1) * hb, last_halo), off + h))

    def meta(off):
        return pl.BlockSpec((N_META, LANES), lambda h, c: (meta_blk0 + seq_of(c), off + h))

    def vt_cur():
        return pl.BlockSpec((LANES, ct), lambda h, c: (h, c))

    def vt_prev():
        return pl.BlockSpec((LANES, halo), lambda h, c: (h, jnp.maximum(c * hb - 1, 0)))

    def vt_next():
        return pl.BlockSpec((LANES, halo), lambda h, c: (h, jnp.minimum((c + 1) * hb, last_halo)))

    kern = functools.partial(
        _attn_kernel, cr=cr, chunks_p=chunks_p, n_p_chunks=n_p_chunks, chunks_s=chunks_s,
        rows_p=plan["t_p"] // GRID_W, rows_s=plan["t_s"] // GRID_W)
    width = NA_HEADS * HEAD_DIM
    return pl.pallas_call(
        kern,
        grid=(n_hb, n_chunks),
        in_specs=[
            cur(0), cur(n_hb), prev(n_hb), nxt(n_hb), vt_cur(), vt_prev(), vt_next(),
            meta(0), meta(n_hb),
            pl.BlockSpec((N_META, LANES), lambda h, c: (seq_of(c), h)),
            pl.BlockSpec((None, LANES, N_META), lambda h, c: (seq_of(c), h, 0)),
            pl.BlockSpec((None,) + bias_tab.shape[1:], lambda h, c: (h, 0, 0, 0)),
            pl.BlockSpec((None,) + bias_meta.shape[1:], lambda h, c: (h, 0, 0)),
            pl.BlockSpec((None, HEADS_PER_BLOCK, 8, LANES), lambda h, c: (h, 0, 0, 0)),
        ],
        out_specs=[
            pl.BlockSpec((ct, LANES), lambda h, c: (c, h)),
            pl.BlockSpec((N_META, LANES), lambda h, c: (seq_of(c), h)),
        ],
        out_shape=[jax.ShapeDtypeStruct((n_main, width), BF16),
                   jax.ShapeDtypeStruct((plan["n_seq"] * N_META, width), BF16)],
        scratch_shapes=[pltpu.VMEM((ct + 2 * halo, LANES), BF16),
                        pltpu.VMEM((HEADS_PER_BLOCK, V_ROWS, ct + 2 * halo), BF16),
                        pltpu.VMEM((4, NK + N_META, HEADS_PER_BLOCK * NQ), F32)],
        compiler_params=pltpu.CompilerParams(
            dimension_semantics=("arbitrary", "arbitrary"), vmem_limit_bytes=VMEM_LIMIT),
        name="neighbourhood_attention",
    )(qk, qk, qk, qk, vt, vt, vt, qk, qk, vm, vmt, bias_tab, bias_meta, meta_bias)


def _attn_bias_table(rel_bias, meta_bias):
    n_heads, n_row_off, n_col_off = rel_bias.shape
    half = n_col_off // 2
    qc = np.arange(GRID_W)[None, :]
    kc = np.arange(GRID_W)[:, None]
    cstart = np.clip(qc - NA_WIN_COLS // 2, 0, GRID_W - NA_WIN_COLS)
    col_ok = (kc >= cstart) & (kc < cstart + NA_WIN_COLS)
    onehot = (kc - qc + half == np.arange(n_col_off)[:, None, None]).astype(np.float32)
    toe = jnp.einsum("hrc,ckq->hrkq", rel_bias, onehot, precision=lax.Precision.HIGHEST)
    toe = jnp.where(col_ok[None, None], toe, -1e30)
    n_hb = n_heads // HEADS_PER_BLOCK
    toe = toe.reshape(n_hb, HEADS_PER_BLOCK, n_row_off, GRID_W, GRID_W)
    per_a = jnp.stack([toe[:, :, HALO_ROWS - 1 - a:HALO_ROWS - 1 - a + WIN_ROWS]
                       for a in range(Q_ROWS)], axis=2)
    per_a = jnp.transpose(per_a, (0, 3, 4, 1, 2, 5))
    a = np.arange(Q_ROWS)[None, :]
    b = np.arange(WIN_ROWS)[:, None]
    row_ok = np.stack([(b >= HALO_ROWS) & (a >= 0),
                       (b - a >= 0) & (b - a < NA_WIN_ROWS),
                       (b < NA_WIN_ROWS) & (a >= 0)])
    tab = jnp.where(row_ok[None, :, :, None, None, :, None], per_a[:, None], -1e30)
    tab = tab.reshape(n_hb, 3, NK, HEADS_PER_BLOCK * NQ)
    meta = jnp.transpose(meta_bias.reshape(n_hb, HEADS_PER_BLOCK, N_META), (0, 2, 1))
    meta = jnp.broadcast_to(meta[:, :, :, None], (n_hb, N_META, HEADS_PER_BLOCK, NQ))
    return tab.astype(F32), meta.reshape(n_hb, N_META, HEADS_PER_BLOCK * NQ).astype(F32)


def _route_kernel(o_ref, om_ref, h_ref, wout_ref, g_ref, wrt_ref, brc_ref, triu_ref,
                  h_out, xn_out, route_out, cnt_out, *, n_tiles):
    i = pl.program_id(0)

    def body(o):
        h = h_ref[...] + _dot(o, wout_ref[...])
        h_out[...] = h
        xn = _rms(h, g_ref[...]).astype(BF16)
        xn_out[...] = xn
        logits = _dot_nt(wrt_ref[...], xn) + brc_ref[...]
        tm = logits.shape[1]
        sub = lax.broadcasted_iota(jnp.int32, (N_EXPERTS, tm), 0)
        m1 = jnp.max(logits, axis=0, keepdims=True)
        i1 = jnp.min(jnp.where(logits == m1, sub, N_EXPERTS), axis=0, keepdims=True)
        oh1 = sub == i1
        rest = jnp.where(oh1, -jnp.inf, logits)
        m2 = jnp.max(rest, axis=0, keepdims=True)
        i2 = jnp.min(jnp.where(rest == m2, sub, N_EXPERTS), axis=0, keepdims=True)
        oh2 = sub == i2
        t = jnp.exp(m2 - m1)
        g1 = 1.0 / (1.0 + t)
        g2 = t * g1
        oh = jnp.where(oh1 | oh2, 1.0, 0.0)
        cum = _dot(oh.astype(BF16), triu_ref[...])
        lr1 = jnp.sum(jnp.where(oh1, cum, 0.0), axis=0, keepdims=True)
        lr2 = jnp.sum(jnp.where(oh2, cum, 0.0), axis=0, keepdims=True)
        route = jnp.zeros((N_EXPERTS, tm), F32)
        for k, val in enumerate([i1.astype(F32), i2.astype(F32), lr1, lr2, g1, g2]):
            route = jnp.where(sub == k, val, route)
        route_out[...] = route
        cnt_out[0] = jnp.broadcast_to(jnp.sum(oh, axis=1, keepdims=True), (N_EXPERTS, LANES))

    @pl.when(i < n_tiles)
    def _():
        body(o_ref[...])

    @pl.when(i == n_tiles)
    def _():
        body(om_ref[...])


def _route_layer(o_main, o_meta, h, wout, g, wrt, brc, plan):
    tm = plan["tm"]
    d = h.shape[1]
    n_tiles = plan["n_main"] // tm
    triu = (jnp.arange(tm)[:, None] < jnp.arange(tm)[None, :]).astype(BF16)
    kern = functools.partial(_route_kernel, n_tiles=n_tiles)
    rows = (n_tiles + 1) * tm
    return pl.pallas_call(
        kern,
        grid=(n_tiles + 1,),
        in_specs=[
            pl.BlockSpec((tm, d), lambda i: (jnp.minimum(i, n_tiles - 1), 0)),
            _const_spec(o_meta.shape),
            pl.BlockSpec((tm, d), lambda i: (i, 0)),
            _const_spec(wout.shape), _const_spec(g.shape), _const_spec(wrt.shape),
            _const_spec(brc.shape), _const_spec(triu.shape),
        ],
        out_specs=[
            pl.BlockSpec((tm, d), lambda i: (i, 0)),
            pl.BlockSpec((tm, d), lambda i: (i, 0)),
            pl.BlockSpec((N_EXPERTS, tm), lambda i: (0, i)),
            pl.BlockSpec((1, N_EXPERTS, LANES), lambda i: (i, 0, 0)),
        ],
        out_shape=[jax.ShapeDtypeStruct((rows, d), F32),
                   jax.ShapeDtypeStruct((rows, d), BF16),
                   jax.ShapeDtypeStruct((N_EXPERTS, rows), F32),
                   jax.ShapeDtypeStruct((n_tiles + 1, N_EXPERTS, LANES), F32)],
        compiler_params=pltpu.CompilerParams(
            dimension_semantics=("arbitrary",), vmem_limit_bytes=VMEM_LIMIT),
        name="outproj_router",
    )(o_main, o_meta, h, wout, g, wrt, brc, triu)


def _for_each_run(tab_ref, tm, fn):
    for e in range(N_EXPERTS):
        local, length, sorted_row = tab_ref[0, 0, e], tab_ref[0, 1, e], tab_ref[0, 2, e]
        n = tm
        while n >= SEG_ALIGN:
            @pl.when((length & n) != 0)
            def _(n=n):
                done = (length // (2 * n)) * (2 * n)
                fn(pl.multiple_of(local + done, SEG_ALIGN),
                   pl.multiple_of(sorted_row + done, SEG_ALIGN), n)
            n //= 2


def _dispatch_kernel(tab_ref, tabp_ref, tail_ref, slot_ref, route_ref, x_ref, xs_ref, stage_ref,
                     zero_ref, sems, zsem, *, tm, te):
    i = pl.program_id(0)
    d = x_ref.shape[1]
    slot = i % 2

    @pl.when(i == 0)
    def _():
        zero_ref[...] = jnp.zeros(zero_ref.shape, F32)

        def zero_tile(start):
            cp = pltpu.make_async_copy(
                zero_ref, xs_ref.at[pl.ds(pl.multiple_of(start, te), te)], zsem)
            cp.start()
            cp.wait()

        for e in range(N_EXPERTS):
            @pl.when(tail_ref[e] >= 0)
            def _():
                zero_tile(tail_ref[e])

        def zero_unused(u, carry):
            zero_tile(u * te)
            return carry

        lax.fori_loop(tail_ref[N_EXPERTS], xs_ref.shape[0] // te, zero_unused, 0)

    s_rows = stage_ref.shape[1]
    slot_iota = lax.broadcasted_iota(jnp.int32, (s_rows, tm), 0)
    first = slot_iota == slot_ref[0:1, :]
    second = slot_iota == slot_ref[1:2, :]
    perm = jnp.where(first | second, 1.0, 0.0).astype(BF16)
    stage_ref[slot, :, 0:d] = _dot(perm, x_ref[...])
    gate = jnp.sum(jnp.where(first, route_ref[4:5, :], 0.0)
                   + jnp.where(second, route_ref[5:6, :], 0.0), axis=1, keepdims=True)
    stage_ref[slot, :, d:] = jnp.broadcast_to(gate, (s_rows, LANES))

    def piece(buf, local_row, sorted_row, n):
        return pltpu.make_async_copy(
            stage_ref.at[buf, pl.ds(local_row, n)], xs_ref.at[pl.ds(sorted_row, n)],
            sems.at[buf])

    _for_each_run(tab_ref, tm, lambda lr, sr, n: piece(slot, lr, sr, n).start())

    @pl.when(i > 0)
    def _():
        _for_each_run(tabp_ref, tm, lambda lr, sr, n: piece(1 - slot, lr, sr, n).wait())

    @pl.when(i == pl.num_programs(0) - 1)
    def _():
        _for_each_run(tab_ref, tm, lambda lr, sr, n: piece(slot, lr, sr, n).wait())


def _dispatch(xn, slot_t, route_t, tab, tails, plan):
    tm = plan["tm"]
    te = plan["te"]
    d = xn.shape[1]
    kern = functools.partial(_dispatch_kernel, tm=tm, te=te)
    return pl.pallas_call(
        kern,
        grid=(xn.shape[0] // tm,),
        in_specs=[
            pl.BlockSpec((1, 3, N_EXPERTS), lambda i: (i, 0, 0), memory_space=pltpu.SMEM),
            pl.BlockSpec((1, 3, N_EXPERTS), lambda i: (jnp.maximum(i - 1, 0), 0, 0),
                         memory_space=pltpu.SMEM),
            pl.BlockSpec(memory_space=pltpu.SMEM),
            pl.BlockSpec((2, tm), lambda i: (0, i)),
            pl.BlockSpec((N_EXPERTS, tm), lambda i: (0, i)),
            pl.BlockSpec((tm, d), lambda i: (i, 0)),
        ],
        out_specs=pl.BlockSpec(memory_space=pl.ANY),
        out_shape=jax.ShapeDtypeStruct((plan["rows_sorted"], d + LANES), F32),
        scratch_shapes=[pltpu.VMEM((2, plan["s_rows"], d + LANES), F32),
                        pltpu.VMEM((te, d + LANES), F32),
                        pltpu.SemaphoreType.DMA((2,)), pltpu.SemaphoreType.DMA(())],
        compiler_params=pltpu.CompilerParams(
            dimension_semantics=("arbitrary",), vmem_limit_bytes=VMEM_LIMIT),
        name="moe_dispatch",
    )(tab, tab, tails, slot_t, route_t, xn)


def _expert_kernel(te_ref, nu_ref, x_ref, wg_ref, wu_ref, wd_ref, out_ref, *, sub):
    j = pl.program_id(0)
    c = pl.program_id(1)

    @pl.when(c == 0)
    def _():
        out_ref[...] = jnp.zeros(out_ref.shape, F32)

    @pl.when(j < nu_ref[0])
    def _():
        d = out_ref.shape[1]
        xb = x_ref[:, 0:d].astype(BF16)
        fc = wg_ref.shape[1]
        y = None
        for lo in range(0, fc, sub):
            hi = min(lo + sub, fc)
            act = (_silu(_dot(xb, wg_ref[:, lo:hi])) * _dot(xb, wu_ref[:, lo:hi])).astype(BF16)
            part = _dot(act, wd_ref[lo:hi, :])
            y = part if y is None else y + part
        out_ref[...] += x_ref[:, d:d + 1] * y


def _experts(xs, tile_expert, n_used, wg, wu, wd, plan):
    te = plan["te"]
    fc = plan["fc"]
    d = xs.shape[1] - LANES
    n_tiles = xs.shape[0] // te
    n_c = wg.shape[2] // fc

    def row(j, c, te_ref, nu_ref):
        return (jnp.minimum(j, nu_ref[0] - 1), 0)

    def chunk(j, c, nu_ref):
        return jnp.where(j < nu_ref[0], c, n_c - 1)

    def w_in(j, c, te_ref, nu_ref):
        return (te_ref[jnp.minimum(j, nu_ref[0] - 1)], 0, chunk(j, c, nu_ref))

    def w_out(j, c, te_ref, nu_ref):
        return (te_ref[jnp.minimum(j, nu_ref[0] - 1)], chunk(j, c, nu_ref), 0)

    grid_spec = pltpu.PrefetchScalarGridSpec(
        num_scalar_prefetch=2,
        grid=(n_tiles, n_c),
        in_specs=[
            pl.BlockSpec((te, d + LANES), row),
            pl.BlockSpec((None, d, fc), w_in),
            pl.BlockSpec((None, d, fc), w_in),
            pl.BlockSpec((None, fc, d), w_out),
        ],
        out_specs=pl.BlockSpec((te, d), lambda j, c, te_ref, nu_ref: (j, 0)),
    )
    return pl.pallas_call(
        functools.partial(_expert_kernel, sub=plan["fsub"]),
        grid_spec=grid_spec,
        out_shape=jax.ShapeDtypeStruct((xs.shape[0], d), F32),
        compiler_params=pltpu.CompilerParams(
            dimension_semantics=("arbitrary", "arbitrary"), vmem_limit_bytes=VMEM_LIMIT),
        name="moe_experts",
    )(tile_expert, n_used, xs, wg, wu, wd)


def _combine_kernel(tabc_ref, tabn_ref, slot_ref, h_ref, g_ref, ys_ref, outp_ref, outs_ref,
                    y_ref, sems, *, tm, n_tiles, n_p_tiles):
    i = pl.program_id(0)

    def piece(slot, local_row, sorted_row, n):
        return pltpu.make_async_copy(
            ys_ref.at[pl.ds(sorted_row, n)], y_ref.at[slot, pl.ds(local_row, n)], sems.at[slot])

    @pl.when(i == 0)
    def _():
        y_ref[...] = jnp.zeros(y_ref.shape, F32)
        _for_each_run(tabc_ref, tm, lambda lr, sr, n: piece(0, lr, sr, n).start())

    @pl.when(i + 1 < n_tiles)
    def _():
        _for_each_run(tabn_ref, tm, lambda lr, sr, n: piece((i + 1) % 2, lr, sr, n).start())

    slot = i % 2
    _for_each_run(tabc_ref, tm, lambda lr, sr, n: piece(slot, lr, sr, n).wait())

    y = y_ref[slot].astype(BF16)
    s_rows = y.shape[0]
    slot_iota = lax.broadcasted_iota(jnp.int32, (tm, s_rows), 1)
    pick = jnp.where((slot_iota == slot_ref[:, 0:1]) | (slot_iota == slot_ref[:, 1:2]),
                     1.0, 0.0).astype(BF16)
    h = h_ref[...] + _dot(pick, y)
    out = _rms(h, g_ref[...])

    @pl.when(i < n_p_tiles)
    def _():
        outp_ref[...] = out

    @pl.when(i >= n_p_tiles)
    def _():
        outs_ref[...] = out


def _combine(h, slot_c, tab, ys, g, plan):
    tm = plan["tm"]
    d = h.shape[1]
    n_tiles = plan["n_main"] // tm
    n_p_tiles = plan["n_p"] // tm
    kern = functools.partial(_combine_kernel, tm=tm, n_tiles=n_tiles, n_p_tiles=n_p_tiles)
    return pl.pallas_call(
        kern,
        grid=(n_tiles,),
        in_specs=[
            pl.BlockSpec((1, 3, N_EXPERTS), lambda i: (i, 0, 0), memory_space=pltpu.SMEM),
            pl.BlockSpec((1, 3, N_EXPERTS), lambda i: (jnp.minimum(i + 1, n_tiles - 1), 0, 0),
                         memory_space=pltpu.SMEM),
            pl.BlockSpec((tm, 2), lambda i: (i, 0)),
            pl.BlockSpec((tm, d), lambda i: (i, 0)),
            _const_spec(g.shape),
            pl.BlockSpec(memory_space=pl.ANY),
        ],
        out_specs=[
            pl.BlockSpec((tm, d), lambda i: (jnp.minimum(i, n_p_tiles - 1), 0)),
            pl.BlockSpec((tm, d), lambda i: (jnp.maximum(i - n_p_tiles, 0), 0)),
        ],
        out_shape=[jax.ShapeDtypeStruct((plan["n_p"], d), F32),
                   jax.ShapeDtypeStruct((plan["n_main"] - plan["n_p"], d), F32)],
        scratch_shapes=[pltpu.VMEM((2, plan["s_rows"], d), F32),
                        pltpu.SemaphoreType.DMA((2,))],
        compiler_params=pltpu.CompilerParams(
            dimension_semantics=("arbitrary",), vmem_limit_bytes=VMEM_LIMIT),
        name="moe_combine_norm",
    )(tab, tab, slot_c, h, g, ys)


def _make_plan(b_p, t_p, b_s, t_s, d_ff_expert):
    tm = 512
    te = 1024
    n_p = b_p * t_p
    n_main = n_p + b_s * t_s
    n_seq = b_p + b_s
    assert n_seq * N_META <= tm and t_p % tm == 0 and t_s % tm == 0
    np_rows = n_main + tm
    rows_p, rows_s = t_p // GRID_W, t_s // GRID_W
    cr = 64 if rows_p % 64 == 0 and rows_s % 64 == 0 else 4 * Q_ROWS
    assert rows_p % cr == 0 and rows_s % cr == 0 and cr % (4 * Q_ROWS) == 0
    fc = 1792 if d_ff_expert % 1792 == 0 else d_ff_expert
    fsub = MXU_DIM
    run_pad = N_EXPERTS * (SEG_ALIGN - 1)
    s_rows = -(-(2 * tm + run_pad) // SEG_ALIGN) * SEG_ALIGN
    rows_sorted = -(-(2 * np_rows + run_pad * (np_rows // tm)) // te) * te + N_EXPERTS * te
    return dict(tm=tm, te=te, n_p=n_p, n_main=n_main, n_seq=n_seq, np_rows=np_rows, t_p=t_p,
                t_s=t_s, cr=cr, fc=fc, fsub=fsub, s_rows=s_rows, rows_sorted=rows_sorted)


def kernel(x_prompt, x_sample, meta_tokens, norm_mix, norm_ffn, norm_final, even_w_in, pool_w,
           pool_scale, conv_w, even_w_out, ffn_w_gate, ffn_w_up, ffn_w_down, na_w_qkv,
           na_rel_bias, na_meta_bias, na_w_out, moe_w_router, moe_b_router, moe_w_gate,
           moe_w_up, moe_w_down):
    b_p, t_p, d = x_prompt.shape
    b_s, t_s, _ = x_sample.shape
    plan = _make_plan(b_p, t_p, b_s, t_s, moe_w_gate.shape[-1])
    tm, te = plan["tm"], plan["te"]
    n_main, n_seq, np_rows = plan["n_main"], plan["n_seq"], plan["np_rows"]
    n_meta_rows = n_seq * N_META

    xfirst = jnp.concatenate([x_prompt[:, :EXT].reshape(-1, d), x_sample[:, :EXT].reshape(-1, d)])

    row = lambda v: v.reshape(1, -1).astype(F32)
    h = _even_layer(x_prompt.reshape(-1, d), x_sample.reshape(-1, d), meta_tokens, xfirst,
                    row(norm_mix[0]), even_w_in[0].astype(BF16), pool_w[0].astype(BF16),
                    row(pool_scale[0]), conv_w[0], even_w_out[0].astype(BF16), plan)
    h = _ffn_layer(h, row(norm_ffn[0]), ffn_w_gate[0].astype(BF16), ffn_w_up[0].astype(BF16),
                   ffn_w_down[0].astype(BF16), plan)

    width = NA_HEADS * HEAD_DIM
    log2e = float(np.log2(np.e))
    qscale = jnp.concatenate([jnp.full((width,), HEAD_DIM ** -0.5 * log2e, F32),
                              jnp.ones((width,), F32)])
    qk, vt = _qkv_layer(h, row(norm_mix[1]), (na_w_qkv[0][:, :2 * width] * qscale).astype(BF16),
                        na_w_qkv[0][:, 2 * width:].T.astype(BF16), plan)
    vt_meta = vt[:, n_main:n_main + n_meta_rows]
    n_hb = NA_HEADS // HEADS_PER_BLOCK
    o_main, o_meta = _attn_layer(
        qk, vt, vt_meta.T, vt_meta.reshape(width, n_seq, N_META).transpose(1, 0, 2),
        _attn_bias_table(na_rel_bias[0] * log2e, na_meta_bias[0] * log2e),
        jnp.pad((na_meta_bias[0] * log2e).reshape(n_hb, HEADS_PER_BLOCK, 1, N_META),
                ((0, 0), (0, 0), (0, 7), (0, LANES - N_META))), plan)
    o_meta = jnp.concatenate([o_meta, jnp.zeros((tm - n_meta_rows, width), BF16)])

    h, xn, route_t, tile_cnt = _route_layer(
        o_main, o_meta, h, na_w_out[0].astype(BF16), row(norm_ffn[1]),
        moe_w_router[0].T.astype(BF16), moe_b_router[0].reshape(-1, 1).astype(F32), plan)

    cnt = tile_cnt[:, :, 0].astype(jnp.int32)
    seg = (cnt + SEG_ALIGN - 1) // SEG_ALIGN * SEG_ALIGN
    local = jnp.cumsum(seg, axis=1) - seg
    before = jnp.cumsum(seg, axis=0) - seg
    padded = (jnp.sum(seg, axis=0) + te - 1) // te * te
    ends = jnp.cumsum(padded)
    tab = jnp.stack([local, seg, (ends - padded)[None, :] + before], axis=1).astype(jnp.int32)
    experts_t = route_t[0:2].astype(jnp.int32)
    local_tok = jnp.repeat(local, tm, axis=0).T
    pick = experts_t[:, None, :] == jnp.arange(N_EXPERTS)[None, :, None]
    slot_t = jnp.sum(jnp.where(pick, local_tok[None], 0), axis=1) + route_t[2:4].astype(jnp.int32)
    tails = jnp.concatenate([jnp.where(padded > 0, ends - te, -1), ends[-1:] // te])
    tails = tails.astype(jnp.int32)
    n_row_tiles = plan["rows_sorted"] // te
    tile_expert = jnp.minimum(
        jnp.sum(ends[None, :] <= (jnp.arange(n_row_tiles) * te)[:, None], axis=1),
        N_EXPERTS - 1).astype(jnp.int32)
    n_used = (ends[-1:] // te).astype(jnp.int32)

    xs = _dispatch(xn, slot_t, route_t, tab, tails, plan)
    ys = _experts(xs, tile_expert, n_used, moe_w_gate[0].astype(BF16), moe_w_up[0].astype(BF16),
                  moe_w_down[0].astype(BF16), plan)
    y_p, y_s = _combine(h, slot_t.T, tab, ys, row(norm_final), plan)
    return (y_p.reshape(b_p, t_p, d), y_s.reshape(b_s, t_s, d))
```
